```python
import math
import jax, jax.numpy as jnp
from jax import lax
import numpy as np

D_MODEL = 2048
BATCH = 32
SEQ = 256
DEPTH = 2
DEC_BATCH = 8
DEC_SEQ = 2048
PAST_LEN = 256

GRID_W = 64
H_A = 8
DK = 128
DV = 128
W_A = H_A * DV
CHUNK = 32
W_B = 512
W_C = 512
CONV_K = 31
W_IN = 3 * H_A * DK + 2 * W_A + W_B + 2 * W_C
N_BRANCH = 3
N_EXP = 32
TOP_K = 4
D_FF = 2048
SWIGLU_LIMIT = 7.0
SWIGLU_ALPHA = 1.702
MOE_BLOCK = 256
LN_EPS = 1e-5
RMS_EPS = 1e-6
DEEPNORM_ALPHA = (2 * DEPTH) ** 0.25
DEEPNORM_BETA = (8 * DEPTH) ** -0.25

kernel_name = "hybrid_hgrn2_fnet_conformer_moe_diffusion_step"


def layer_norm(x, g=None, b=None):
    xf = x.astype(jnp.float32)
    mu = jnp.mean(xf, axis=-1, keepdims=True)
    var = jnp.mean(jnp.square(xf - mu), axis=-1, keepdims=True)
    y = (xf - mu) * lax.rsqrt(var + LN_EPS)
    if g is not None:
        y = y * g.astype(jnp.float32) + b.astype(jnp.float32)
    return y.astype(x.dtype)


def grid_pos_embed(n_tok):
    rows = n_tok // GRID_W
    r = jnp.repeat(jnp.arange(rows, dtype=jnp.float32), GRID_W)
    col = jnp.tile(jnp.arange(GRID_W, dtype=jnp.float32), rows)
    quarter = D_MODEL // 4
    omega = 1.0 / (10000.0 ** (jnp.arange(quarter, dtype=jnp.float32) / quarter))
    def emb(pos):
        a = pos[:, None] * omega[None, :]
        return jnp.concatenate([jnp.sin(a), jnp.cos(a)], axis=-1)
    return jnp.concatenate([emb(r), emb(col)], axis=-1)


def hgrn2_chunk_scan(q, k, v, logf, s0):
    B, L = q.shape[:2]
    n = L // CHUNK
    def to_chunks(t):
        return t.reshape(B, n, CHUNK, H_A, t.shape[-1]).transpose(1, 0, 3, 2, 4)
    causal = jnp.tril(jnp.ones((CHUNK, CHUNK), dtype=bool))[:, :, None]
    def step(S, inp):
        qc, kc, vc, fc = inp
        b = jnp.cumsum(fc, axis=2)
        o_inter = jnp.einsum('bhtk,bhkv->bhtv', qc * jnp.exp(b), S)
        diff = b[:, :, :, None, :] - b[:, :, None, :, :]
        decay = jnp.exp(jnp.where(causal, diff, -jnp.inf))
        scores = jnp.einsum('bhtk,bhsk,bhtsk->bhts', qc, kc, decay)
        o = o_inter + jnp.einsum('bhts,bhsv->bhtv', scores, vc)
        b_end = b[:, :, -1:, :]
        S = jnp.exp(b_end[:, :, 0, :, None]) * S + jnp.einsum('bhsk,bhsv->bhkv', kc * jnp.exp(b_end - b), vc)
        return S, o
    S, o = lax.scan(step, s0, (to_chunks(q), to_chunks(k), to_chunks(v), to_chunks(logf)))
    o = o.transpose(1, 0, 3, 2, 4).reshape(B, L, H_A, DV)
    return S, o


def fourier_mix(u):
    return jnp.real(jnp.fft.fft2(u.astype(jnp.float32), axes=(1, 2), norm="ortho")).astype(u.dtype)


def conformer_conv(u, w_dw, b_dw, ln_g, ln_b):
    a, gt = jnp.split(u, 2, axis=-1)
    z = a * jax.nn.sigmoid(gt)
    z = lax.conv_general_dilated(z, w_dw[:, None, :], window_strides=(1,),
                                 padding=[(CONV_K // 2, CONV_K // 2)],
                                 dimension_numbers=("NWC", "WIO", "NWC"),
                                 feature_group_count=W_C) + b_dw
    return jax.nn.silu(layer_norm(z, ln_g, ln_b))


def token_mixers(h, s0, lb, p):
    B, L, _ = h.shape
    f32 = jnp.float32
    u = h @ p["w_in"] + p["b_in"]
    sizes = (H_A * DK, H_A * DK, H_A * DK, W_A, W_A, W_B, 2 * W_C)
    q, zf, zb, v, g, u_b, u_c = jnp.split(u, np.cumsum(sizes)[:-1].tolist(), axis=-1)
    heads = lambda t: t.reshape(B, L, H_A, t.shape[-1] // H_A)

    def forget(z, lbd):
        z = z.astype(f32)
        logf = jnp.logaddexp(jnp.log(lbd), jnp.log1p(-lbd) + jax.nn.log_sigmoid(z))
        k = (1.0 - lbd) * jax.nn.sigmoid(-z)
        return heads(k), heads(logf)

    k_f, lf_f = forget(zf, lb[0])
    k_b, lf_b = forget(zb, lb[1])
    qh = heads(q.astype(f32)) * (DK ** -0.5)
    vh = heads(v.astype(f32))
    s0 = s0.astype(f32)
    rev = lambda t: t[:, ::-1]
    s_f, o_f = hgrn2_chunk_scan(qh, k_f, vh, lf_f, s0[:, 0])
    s_b, o_b = hgrn2_chunk_scan(rev(qh), rev(k_b), rev(vh), rev(lf_b), s0[:, 1])
    o = o_f + rev(o_b)
    o = o * lax.rsqrt(jnp.mean(o * o, axis=-1, keepdims=True) + RMS_EPS) * p["g_norm_a"].astype(f32)
    y_a = (o * jax.nn.silu(heads(g.astype(f32)))).reshape(B, L, W_A).astype(h.dtype)

    y_b = fourier_mix(u_b)
    y_c = conformer_conv(u_c, p["conv_w"], p["conv_b"], p["conv_ln_g"], p["conv_ln_b"])

    gates = jax.nn.sigmoid((h @ p["w_gate"] + p["b_gate"]).astype(f32)).astype(h.dtype)
    gates = gates.reshape(B, L, N_BRANCH, D_MODEL)
    merged = (gates[:, :, 0] * (y_a @ p["w_br_a"] + p["b_br"][0])
              + gates[:, :, 1] * (y_b @ p["w_br_b"] + p["b_br"][1])
              + gates[:, :, 2] * (y_c @ p["w_br_c"] + p["b_br"][2]))
    out = merged @ p["w_o"] + p["b_o"]
    return out, jnp.stack([s_f, s_b], axis=1)


def moe_ffn(h, w_router, b_router, w_gu, b_gu, w_down, b_down):
    shp = h.shape
    x = h.reshape(-1, D_MODEL)
    T = x.shape[0]
    logits = (x @ w_router + b_router).astype(jnp.float32)
    top_val, top_idx = lax.top_k(logits, TOP_K)
    probs = jax.nn.softmax(top_val, axis=-1)
    n_assign = T * TOP_K
    e_flat = top_idx.reshape(-1)
    order = jnp.argsort(e_flat)
    e_sorted = e_flat[order]
    tok_sorted = order // TOP_K
    counts = jnp.bincount(e_flat, length=N_EXP)
    padded = (counts + MOE_BLOCK - 1) // MOE_BLOCK * MOE_BLOCK
    pad_end = jnp.cumsum(padded)
    pad_start = pad_end - padded
    start = jnp.cumsum(counts) - counts
    dest = pad_start[e_sorted] + jnp.arange(n_assign) - start[e_sorted]
    n_blocks = -(-n_assign // MOE_BLOCK) + N_EXP
    row_tok = jnp.full((n_blocks * MOE_BLOCK,), T, dtype=jnp.int32).at[dest].set(tok_sorted.astype(jnp.int32))
    block_exp = jnp.minimum(jnp.searchsorted(pad_end, jnp.arange(n_blocks) * MOE_BLOCK, side='right'), N_EXP - 1)
    x_pad = jnp.concatenate([x, jnp.zeros((1, D_MODEL), x.dtype)], axis=0)

    def expert_block(args):
        rows, e = args
        xb = x_pad[rows]
        gu = xb @ w_gu[e] + b_gu[e]
        gt, up = gu[:, :D_FF], gu[:, D_FF:]
        gt = jnp.minimum(gt, SWIGLU_LIMIT)
        up = jnp.clip(up, -SWIGLU_LIMIT, SWIGLU_LIMIT)
        act = (up + 1.0) * (gt * jax.nn.sigmoid(SWIGLU_ALPHA * gt))
        return act @ w_down[e] + b_down[e]

    yb = lax.map(expert_block, (row_tok.reshape(n_blocks, MOE_BLOCK), block_exp))
    w_sorted = probs.reshape(-1)[order].astype(x.dtype)
    y_assign = yb.reshape(-1, D_MODEL)[dest] * w_sorted[:, None]
    y = jax.ops.segment_sum(y_assign, tok_sorted, num_segments=T)
    return y.reshape(shp)


def trunk_layer(x, mod, s0, lb, p):
    shift1, scale1, gate1, shift2, scale2, gate2 = jnp.split(mod, 6, axis=-1)
    h = layer_norm(x) * (1 + scale1) + shift1
    mix, s_fin = token_mixers(h, s0, lb, p)
    x = layer_norm(DEEPNORM_ALPHA * x + gate1 * mix, p["ln1_g"], p["ln1_b"])
    h = layer_norm(x) * (1 + scale2) + shift2
    ffn = moe_ffn(h, p["w_router"], p["b_router"], p["w_gu"], p["b_gu"], p["w_down"], p["b_down"])
    x = layer_norm(DEEPNORM_ALPHA * x + gate2 * ffn, p["ln2_g"], p["ln2_b"])
    return x, s_fin


def setup_inputs(seed: int = 0) -> dict:
    key = jax.random.key(seed)
    ks = jax.random.split(key, 40)
    cnt = [0]
    def nrm(shape, scale):
        k = ks[cnt[0]]
        cnt[0] += 1
        return jax.random.normal(k, shape, jnp.float32) * scale
    D = D_MODEL
    return {
        "x_prompt": nrm((BATCH, SEQ, D), 1.0),
        "x_sample": nrm((DEC_BATCH, DEC_SEQ, D), 1.0),
        "state_hgrn": nrm((DEC_BATCH, DEPTH, 2, H_A, DK, DV), 0.3),
        "c": nrm((DEC_BATCH, D), 1.0),
        "c_ctx": nrm((D,), 1.0),
        "w_ada": nrm((DEPTH, D, 6 * D), 0.5 * D ** -0.5),
        "b_ada": nrm((DEPTH, 6 * D), 0.02),
        "w_in": nrm((DEPTH, D, W_IN), D ** -0.5),
        "b_in": nrm((DEPTH, W_IN), 0.02),
        "lb_logits": nrm((DEPTH, 2, H_A * DK), 1.0),
        "g_norm_a": 1.0 + nrm((DEPTH, DV), 0.02),
        "conv_w": nrm((DEPTH, CONV_K, W_C), CONV_K ** -0.5),
        "conv_b": nrm((DEPTH, W_C), 0.02),
        "conv_ln_g": 1.0 + nrm((DEPTH, W_C), 0.02),
        "conv_ln_b": nrm((DEPTH, W_C), 0.02),
        "w_br_a": nrm((DEPTH, W_A, D), W_A ** -0.5),
        "w_br_b": nrm((DEPTH, W_B, D), W_B ** -0.5),
        "w_br_c": nrm((DEPTH, W_C, D), W_C ** -0.5),
        "b_br": nrm((DEPTH, N_BRANCH, D), 0.02),
        "w_gate": nrm((DEPTH, D, N_BRANCH * D), D ** -0.5),
        "b_gate": nrm((DEPTH, N_BRANCH * D), 0.02),
        "w_o": nrm((DEPTH, D, D), DEEPNORM_BETA * D ** -0.5),
        "b_o": nrm((DEPTH, D), 0.02),
        "ln1_g": 1.0 + nrm((DEPTH, D), 0.02),
        "ln1_b": nrm((DEPTH, D), 0.02),
        "w_router": nrm((DEPTH, D, N_EXP), D ** -0.5),
        "b_router": nrm((DEPTH, N_EXP), 0.01),
        "w_gu": nrm((DEPTH, N_EXP, D, 2 * D_FF), D ** -0.5),
        "b_gu": nrm((DEPTH, N_EXP, 2 * D_FF), 0.02),
        "w_down": nrm((DEPTH, N_EXP, D_FF, D), DEEPNORM_BETA * D_FF ** -0.5),
        "b_down": nrm((DEPTH, N_EXP, D), 0.02),
        "ln2_g": 1.0 + nrm((DEPTH, D), 0.02),
        "ln2_b": nrm((DEPTH, D), 0.02),
    }


def reference(x_prompt, x_sample, state_hgrn, c, c_ctx, w_ada, b_ada, w_in, b_in, lb_logits,
              g_norm_a, conv_w, conv_b, conv_ln_g, conv_ln_b, w_br_a, w_br_b, w_br_c, b_br,
              w_gate, b_gate, w_o, b_o, ln1_g, ln1_b, w_router, b_router, w_gu, b_gu,
              w_down, b_down, ln2_g, ln2_b):
    sm = jax.nn.softmax(lb_logits.astype(jnp.float32), axis=0)
    cs = jnp.cumsum(sm, axis=0)
    lower_bounds = cs - cs[0:1]

    xp = x_prompt
    xs = x_sample + grid_pos_embed(x_sample.shape[1]).astype(x_sample.dtype)[None]
    ctx_states = []
    for l in range(DEPTH):
        p = {"w_in": w_in[l], "b_in": b_in[l], "g_norm_a": g_norm_a[l],
             "conv_w": conv_w[l], "conv_b": conv_b[l], "conv_ln_g": conv_ln_g[l], "conv_ln_b": conv_ln_b[l],
             "w_br_a": w_br_a[l], "w_br_b": w_br_b[l], "w_br_c": w_br_c[l], "b_br": b_br[l],
             "w_gate": w_gate[l], "b_gate": b_gate[l], "w_o": w_o[l], "b_o": b_o[l],
             "ln1_g": ln1_g[l], "ln1_b": ln1_b[l], "w_router": w_router[l], "b_router": b_router[l],
             "w_gu": w_gu[l], "b_gu": b_gu[l], "w_down": w_down[l], "b_down": b_down[l],
             "ln2_g": ln2_g[l], "ln2_b": ln2_b[l]}
        mod_ctx = (jax.nn.silu(c_ctx) @ w_ada[l] + b_ada[l])[None, None, :]
        mod_lat = (jax.nn.silu(c) @ w_ada[l] + b_ada[l])[:, None, :]
        s0_ctx = jnp.zeros((xp.shape[0], 2, H_A, DK, DV), jnp.float32)
        xp, s_ctx = trunk_layer(xp, mod_ctx, s0_ctx, lower_bounds[l], p)
        xs, _ = trunk_layer(xs, mod_lat, state_hgrn[:, l], lower_bounds[l], p)
        ctx_states.append(s_ctx)
    new_state_hgrn = jnp.stack(ctx_states, axis=1)
    return (xp, xs, new_state_hgrn)
```

```python
import functools
import math

import jax
import jax.numpy as jnp
from jax import lax
from jax.experimental import pallas as pl
from jax.experimental.pallas import tpu as pltpu

F32 = jnp.float32
BF16 = jnp.bfloat16

GRID_W = 64
H_A = 8
DK = 128
DV = 128
W_A = H_A * DV
W_B = 512
W_C = 512
CONV_K = 31
N_BRANCH = 3
N_EXP = 32
TOP_K = 4
SWIGLU_LIMIT = 7.0
SWIGLU_ALPHA = 1.702
LN_EPS = 1e-5
RMS_EPS = 1e-6

LANES = 128
VMEM_LIMIT = 56 * 1024 * 1024
HGRN_CHUNK = 16
CONV_PAD = 16
CONV_ROWS = 32
LOGF_FLOOR = -1e4
TM = 1024
TN = 512
TM_OUT = 256
MOE_BM = 512


def _cparams(*sem):
    return pltpu.CompilerParams(dimension_semantics=sem, vmem_limit_bytes=VMEM_LIMIT)


def _ln_rows(x):
    mu = jnp.mean(x, axis=-1, keepdims=True)
    xc = x - mu
    var = jnp.mean(xc * xc, axis=-1, keepdims=True)
    return xc * lax.rsqrt(var + LN_EPS)


def _sigmoid(x):
    return 1.0 / (1.0 + jnp.exp(-x))


def _ada_kernel(c_ref, w_ref, b_ref, o_ref):
    c = c_ref[...]
    a = (c * _sigmoid(c)).astype(BF16)
    o_ref[...] = jnp.dot(a, w_ref[...].astype(BF16), preferred_element_type=F32) + b_ref[...]


def ada_mod(cond, w_ada, b_ada, *, tn=1024):
    depth, d, n = w_ada.shape
    r = cond.shape[0]
    return pl.pallas_call(
        _ada_kernel,
        grid=(depth, n // tn),
        in_specs=[
            pl.BlockSpec((r, d), lambda l, j: (0, 0)),
            pl.BlockSpec((None, d, tn), lambda l, j: (l, 0, j)),
            pl.BlockSpec((None, 1, tn), lambda l, j: (l, 0, j)),
        ],
        out_specs=pl.BlockSpec((None, r, tn), lambda l, j: (l, 0, j)),
        out_shape=jax.ShapeDtypeStruct((depth, r, n), F32),
        compiler_params=_cparams("parallel", "parallel"),
        name="ada_mod",
    )(cond, w_ada, b_ada.reshape(depth, 1, n))


def _lnmm_kernel(x_ref, sh_ref, sc_ref, w_ref, b_ref, o_ref, h_ref, *, n_plain):
    j = pl.program_id(1)

    @pl.when(j == 0)
    def _():
        y = _ln_rows(x_ref[...])
        h_ref[...] = (y * (1.0 + sc_ref[...]) + sh_ref[...]).astype(BF16)

    acc = jnp.dot(h_ref[...], w_ref[...], preferred_element_type=F32) + b_ref[...]

    @pl.when(j < n_plain)
    def _():
        o_ref[...] = acc.astype(BF16)

    @pl.when(j >= n_plain)
    def _():
        o_ref[...] = _sigmoid(acc).astype(BF16)


def ln_mod_matmul(x, mod3, row_of_block, w, b, *, n_plain_cols, tm, tn):
    t, d = x.shape
    n = w.shape[1]
    return pl.pallas_call(
        functools.partial(_lnmm_kernel, n_plain=n_plain_cols // tn),
        grid=(t // tm, n // tn),
        in_specs=[
            pl.BlockSpec((tm, d), lambda i, j: (i, 0)),
            pl.BlockSpec((None, 1, d), lambda i, j: (row_of_block(i), 0, 0)),
            pl.BlockSpec((None, 1, d), lambda i, j: (row_of_block(i), 0, 1)),
            pl.BlockSpec((d, tn), lambda i, j: (0, j)),
            pl.BlockSpec((1, tn), lambda i, j: (0, j)),
        ],
        out_specs=pl.BlockSpec((tm, tn), lambda i, j: (i, j)),
        out_shape=jax.ShapeDtypeStruct((t, n), BF16),
        scratch_shapes=[pltpu.VMEM((tm, d), BF16)],
        compiler_params=_cparams("parallel", "arbitrary"),
        name="ln_mod_matmul",
    )(x, mod3, mod3, w, b)


def _hgrn_chunk(q_ref, z_ref, v_ref, lb, st_ref, o_ref, c0, *, direction, chunk):
    rows = pl.ds(c0, chunk)
    q = q_ref[rows, :].astype(F32) * (DK ** -0.5)
    z = z_ref[rows, :].astype(F32)
    v = v_ref[rows, :]
    e = jnp.exp(-jnp.abs(z))
    r = 1.0 / (1.0 + e)
    er = e * r
    pos = z >= 0
    f = lb + (1.0 - lb) * jnp.where(pos, r, er)
    logf = jnp.maximum(jnp.log(f), LOGF_FLOOR)
    k = (1.0 - lb) * jnp.where(pos, er, r)

    ti = lax.broadcasted_iota(jnp.int32, (chunk, chunk), 0)
    si = lax.broadcasted_iota(jnp.int32, (chunk, chunk), 1)
    tri = (ti >= si) if direction == 0 else (ti <= si)
    b = jnp.dot(tri.astype(F32), logf, precision=lax.Precision.HIGHEST, preferred_element_type=F32)
    b_end = b[chunk - 1:chunk, :] if direction == 0 else b[0:1, :]

    st = st_ref[direction]
    qe = (q * jnp.exp(b)).astype(BF16)
    o_inter = lax.dot_general(qe, st.astype(BF16), (((1,), (1,)), ((), ())), preferred_element_type=F32)
    o_ref[rows, :] += o_inter

    vf = v.astype(F32)
    s_idx = lax.broadcasted_iota(jnp.int32, (chunk, 1), 0)
    for t in range(chunk):
        dec = jnp.exp(jnp.minimum(b[t:t + 1, :] - b, 0.0))
        w = jnp.sum((q[t:t + 1, :] * dec) * k, axis=-1, keepdims=True)
        keep = (s_idx <= t) if direction == 0 else (s_idx >= t)
        w = jnp.where(keep, w, 0.0)
        o_ref[pl.ds(c0 + t, 1), :] += jnp.sum(w * vf, axis=0, keepdims=True)

    kk = (k * jnp.exp(b_end - b)).astype(BF16)
    upd = lax.dot_general(v, kk, (((0,), (0,)), ((), ())), preferred_element_type=F32)
    st_ref[direction] = st * jnp.exp(b_end) + upd


def _hgrn_kernel(*refs, seq, chunk, has_s0, want_state):
    q_ref, zf_ref, zb_ref, v_ref, g_ref, lb_ref, gn_ref = refs[:7]
    pos = 7
    s0_ref = None
    if has_s0:
        s0_ref = refs[pos]
        pos += 1
    y_ref = refs[pos]
    pos += 1
    sfin_ref = None
    if want_state:
        sfin_ref = refs[pos]
        pos += 1
    o_ref, st_ref = refs[pos], refs[pos + 1]

    o_ref[...] = jnp.zeros_like(o_ref)
    for d in range(2):
        st_ref[d] = s0_ref[d].T if has_s0 else jnp.zeros((DV, DK), F32)

    n = seq // chunk
    lb_f = lb_ref[0:1, :]
    lb_b = lb_ref[1:2, :]

    def body(i, carry):
        cf = pl.multiple_of(i * chunk, chunk)
        cb = pl.multiple_of((n - 1 - i) * chunk, chunk)
        _hgrn_chunk(q_ref, zf_ref, v_ref, lb_f, st_ref, o_ref, cf, direction=0, chunk=chunk)
        _hgrn_chunk(q_ref, zb_ref, v_ref, lb_b, st_ref, o_ref, cb, direction=1, chunk=chunk)
        return carry

    lax.fori_loop(0, n, body, 0)

    o = o_ref[...]
    g = g_ref[...].astype(F32)
    o = o * lax.rsqrt(jnp.mean(o * o, axis=-1, keepdims=True) + RMS_EPS) * gn_ref[...]
    y_ref[...] = (o * (g * _sigmoid(g))).astype(BF16)
    if want_state:
        sfin_ref[0] = st_ref[0].T
        sfin_ref[1] = st_ref[1].T


def hgrn_mixer(u, lb, g_norm, s0, *, row0, batch, seq, want_state, chunk=HGRN_CHUNK):
    rb0 = row0 // seq
    has_s0 = s0 is not None

    def sec(k):
        return pl.BlockSpec((seq, LANES), lambda b, h: (rb0 + b, k * H_A + h))

    in_specs = [sec(0), sec(1), sec(2), sec(3), sec(4),
                pl.BlockSpec((2, DK), lambda b, h: (0, h)),
                pl.BlockSpec((1, DV), lambda b, h: (0, 0))]
    args = [u, u, u, u, u, lb, g_norm]
    state_spec = pl.BlockSpec((None, 2, None, DK, DV), lambda b, h: (b, 0, h, 0, 0))
    if has_s0:
        in_specs.append(state_spec)
        args.append(s0)
    out_specs = [pl.BlockSpec((seq, DV), lambda b, h: (b, h))]
    out_shape = [jax.ShapeDtypeStruct((batch * seq, W_A), BF16)]
    if want_state:
        out_specs.append(state_spec)
        out_shape.append(jax.ShapeDtypeStruct((batch, 2, H_A, DK, DV), F32))
    return pl.pallas_call(
        functools.partial(_hgrn_kernel, seq=seq, chunk=chunk, has_s0=has_s0, want_state=want_state),
        grid=(batch, H_A),
        in_specs=in_specs,
        out_specs=out_specs,
        out_shape=out_shape,
        scratch_shapes=[pltpu.VMEM((seq, DV), F32), pltpu.VMEM((2, DV, DK), F32)],
        compiler_params=_cparams("parallel", "parallel"),
        name="hgrn_mixer",
    )(*args)


def _fourier_kernel(u_ref, cm_ref, dl_ref, o_ref, ab_ref, *, seq, rows, scale):
    @pl.when(pl.program_id(1) == 0)
    def _():
        for r0 in range(0, seq, rows):
            ab = jnp.dot(u_ref[r0:r0 + rows, :], cm_ref[...], preferred_element_type=F32)
            ab_ref[r0:r0 + rows, :] = ab[:, :W_B].astype(BF16)
            ab_ref[seq + r0:seq + r0 + rows, :] = ab[:, W_B:].astype(BF16)

    o_ref[...] = (jnp.dot(dl_ref[...], ab_ref[...], preferred_element_type=F32) * scale).astype(BF16)


def dft_tables(seq):
    def cs(n):
        i = lax.iota(jnp.int32, n)
        ang = ((i[:, None] * i[None, :]) % n).astype(F32) * (2.0 * math.pi / n)
        return jnp.cos(ang), jnp.sin(ang)
    cl, sl = cs(seq)
    cc, sc = cs(W_B)
    return jnp.concatenate([cl, -sl], axis=1).astype(BF16), jnp.concatenate([cc, sc], axis=1).astype(BF16)


def fourier_mixer(u, dl, cm, *, row0, col0, batch, seq, rows=256):
    rows = min(rows, seq)
    rb0 = row0 // seq
    cb = col0 // W_B
    nt = seq // rows
    return pl.pallas_call(
        functools.partial(_fourier_kernel, seq=seq, rows=rows, scale=1.0 / math.sqrt(seq * W_B)),
        grid=(batch, nt),
        in_specs=[
            pl.BlockSpec((seq, W_B), lambda b, i: (rb0 + b, cb)),
            pl.BlockSpec((W_B, 2 * W_B), lambda b, i: (0, 0)),
            pl.BlockSpec((rows, 2 * seq), lambda b, i: (i, 0)),
        ],
        out_specs=pl.BlockSpec((rows, W_B), lambda b, i: (b * nt + i, 0)),
        out_shape=jax.ShapeDtypeStruct((batch * seq, W_B), BF16),
        scratch_shapes=[pltpu.VMEM((2 * seq, W_B), BF16)],
        compiler_params=_cparams("parallel", "arbitrary"),
        name="fourier_mixer",
    )(u, cm, dl)


def _conv_kernel(a_ref, gt_ref, w_ref, b_ref, g_ref, be_ref, o_ref, z_ref, *, seq):
    zeros = jnp.zeros((CONV_PAD, W_C), F32)
    z_ref[0:CONV_PAD, :] = zeros
    z_ref[CONV_PAD + seq:2 * CONV_PAD + seq, :] = zeros
    blk = min(seq, 256)
    for r0 in range(0, seq, blk):
        a = a_ref[r0:r0 + blk, :].astype(F32)
        gt = gt_ref[r0:r0 + blk, :].astype(F32)
        z_ref[CONV_PAD + r0:CONV_PAD + r0 + blk, :] = a * _sigmoid(gt)

    off = CONV_PAD - CONV_K // 2

    def tile(i, carry):
        r0 = pl.multiple_of(i * CONV_ROWS, CONV_ROWS)
        win = z_ref[pl.ds(r0, CONV_ROWS + 2 * CONV_PAD), :]
        acc = jnp.zeros((CONV_ROWS, W_C), F32)
        for k in range(CONV_K):
            acc = acc + win[off + k:off + k + CONV_ROWS, :] * w_ref[k:k + 1, :]
        y = _ln_rows(acc + b_ref[...]) * g_ref[...] + be_ref[...]
        o_ref[pl.ds(r0, CONV_ROWS), :] = (y * _sigmoid(y)).astype(BF16)
        return carry

    lax.fori_loop(0, seq // CONV_ROWS, tile, 0)


def conv_mixer(u, conv_w, conv_b, ln_g, ln_b, *, row0, col0, batch, seq):
    rb0 = row0 // seq
    cb = col0 // W_C
    vec = pl.BlockSpec((1, W_C), lambda b: (0, 0))
    return pl.pallas_call(
        functools.partial(_conv_kernel, seq=seq),
        grid=(batch,),
        in_specs=[
            pl.BlockSpec((seq, W_C), lambda b: (rb0 + b, cb)),
            pl.BlockSpec((seq, W_C), lambda b: (rb0 + b, cb + 1)),
            pl.BlockSpec((CONV_K, W_C), lambda b: (0, 0)),
            vec, vec, vec,
        ],
        out_specs=pl.BlockSpec((seq, W_C), lambda b: (b, 0)),
        out_shape=jax.ShapeDtypeStruct((batch * seq, W_C), BF16),
        scratch_shapes=[pltpu.VMEM((seq + 2 * CONV_PAD, W_C), F32)],
        compiler_params=_cparams("parallel"),
        name="conv_mixer",
    )(u, u, conv_w, conv_b, ln_g, ln_b)


def _branch_kernel(ya_ref, yb_ref, yc_ref, g0_ref, g1_ref, g2_ref, wa_ref, wb_ref, wc_ref, b_ref, o_ref):
    def proj(y_ref, w_ref, k):
        return jnp.dot(y_ref[...], w_ref[...], preferred_element_type=F32) + b_ref[k:k + 1, :]
    m = g0_ref[...].astype(F32) * proj(ya_ref, wa_ref, 0)
    m = m + g1_ref[...].astype(F32) * proj(yb_ref, wb_ref, 1)
    m = m + g2_ref[...].astype(F32) * proj(yc_ref, wc_ref, 2)
    o_ref[...] = m.astype(BF16)


def branch_merge(ya, yb, yc, u, wa, wb, wc, b_br, *, gate_col0, d, tm, tn):
    t = ya.shape[0]
    gb = gate_col0 // tn
    nd = d // tn

    def gate(k):
        return pl.BlockSpec((tm, tn), lambda i, j: (i, gb + k * nd + j))

    return pl.pallas_call(
        _branch_kernel,
        grid=(t // tm, nd),
        in_specs=[
            pl.BlockSpec((tm, ya.shape[1]), lambda i, j: (i, 0)),
            pl.BlockSpec((tm, yb.shape[1]), lambda i, j: (i, 0)),
            pl.BlockSpec((tm, yc.shape[1]), lambda i, j: (i, 0)),
            gate(0), gate(1), gate(2),
            pl.BlockSpec((wa.shape[0], tn), lambda i, j: (0, j)),
            pl.BlockSpec((wb.shape[0], tn), lambda i, j: (0, j)),
            pl.BlockSpec((wc.shape[0], tn), lambda i, j: (0, j)),
            pl.BlockSpec((N_BRANCH, tn), lambda i, j: (0, j)),
        ],
        out_specs=pl.BlockSpec((tm, tn), lambda i, j: (i, j)),
        out_shape=jax.ShapeDtypeStruct((t, d), BF16),
        compiler_params=_cparams("parallel", "arbitrary"),
        name="branch_merge",
    )(ya, yb, yc, u, u, u, wa, wb, wc, b_br)


def _out_router_kernel(x_ref, m_ref, g1_ref, sh2_ref, sc2_ref, wo_ref, bo_ref, lg_ref, lb_ref,
                       wr_ref, br_ref, x1_ref, h2_ref, idx_ref, p_ref, *, alpha):
    mix = jnp.dot(m_ref[...], wo_ref[...], preferred_element_type=F32) + bo_ref[...]
    x1 = _ln_rows(alpha * x_ref[...] + g1_ref[...] * mix) * lg_ref[...] + lb_ref[...]
    x1_ref[...] = x1
    h2 = (_ln_rows(x1) * (1.0 + sc2_ref[...]) + sh2_ref[...]).astype(BF16)
    h2_ref[...] = h2
    logits = jnp.dot(h2, wr_ref[...], preferred_element_type=F32) + br_ref[...]
    lane = lax.broadcasted_iota(jnp.int32, logits.shape, 1)
    idx_out = jnp.zeros(logits.shape, jnp.int32)
    p_out = jnp.zeros(logits.shape, F32)
    top = None
    denom = None
    for k in range(TOP_K):
        m = jnp.max(logits, axis=-1, keepdims=True)
        i = jnp.min(jnp.where(logits == m, lane, LANES), axis=-1, keepdims=True)
        if k == 0:
            top = m
        ek = jnp.exp(m - top)
        denom = ek if k == 0 else denom + ek
        idx_out = jnp.where(lane == k, i, idx_out)
        p_out = jnp.where(lane == k, ek, p_out)
        logits = jnp.where(lane == i, -jnp.inf, logits)
    idx_ref[...] = idx_out
    p_ref[...] = p_out / denom


def out_router(x, merged, mod3, row_of_block, wo, bo, ln_g, ln_b, wr, br, *, alpha, tm):
    t, d = x.shape

    def modspec(k):
        return pl.BlockSpec((None, 1, d), lambda i: (row_of_block(i), 0, k))

    vec = pl.BlockSpec((1, d), lambda i: (0, 0))
    row_f = pl.BlockSpec((tm, d), lambda i: (i, 0))
    row_s = pl.BlockSpec((tm, LANES), lambda i: (i, 0))
    return pl.pallas_call(
        functools.partial(_out_router_kernel, alpha=alpha),
        grid=(t // tm,),
        in_specs=[row_f, row_f, modspec(2), modspec(3), modspec(4),
                  pl.BlockSpec((d, d), lambda i: (0, 0)), vec, vec, vec,
                  pl.BlockSpec((d, LANES), lambda i: (0, 0)),
                  pl.BlockSpec((1, LANES), lambda i: (0, 0))],
        out_specs=[row_f, row_f, row_s, row_s],
        out_shape=[jax.ShapeDtypeStruct((t, d), F32), jax.ShapeDtypeStruct((t, d), BF16),
                   jax.ShapeDtypeStruct((t, LANES), jnp.int32), jax.ShapeDtypeStruct((t, LANES), F32)],
        compiler_params=_cparams("parallel"),
        name="out_router",
    )(x, merged, mod3, mod3, mod3, wo, bo, ln_g, ln_b, wr, br)


def _moe_up_kernel(be_ref, nv_ref, x_ref, wg_ref, wu_ref, bg_ref, bu_ref, o_ref, wgs_ref, wus_ref):
    b = pl.program_id(1)
    fresh = jnp.logical_or(b == 0, be_ref[b] != be_ref[jnp.maximum(b - 1, 0)])

    @pl.when(fresh)
    def _():
        wgs_ref[...] = wg_ref[...].astype(BF16)
        wus_ref[...] = wu_ref[...].astype(BF16)

    @pl.when(b < nv_ref[0])
    def _():
        x = x_ref[...]
        gt = jnp.dot(x, wgs_ref[...], preferred_element_type=F32) + bg_ref[...]
        up = jnp.dot(x, wus_ref[...], preferred_element_type=F32) + bu_ref[...]
        gt = jnp.minimum(gt, SWIGLU_LIMIT)
        up = jnp.clip(up, -SWIGLU_LIMIT, SWIGLU_LIMIT)
        o_ref[...] = ((up + 1.0) * (gt * _sigmoid(SWIGLU_ALPHA * gt))).astype(BF16)

    @pl.when(b >= nv_ref[0])
    def _():
        o_ref[...] = jnp.zeros_like(o_ref)


def moe_up(xs, block_exp, n_valid, w_gu, b_gu, *, bm, tn):
    r, d = xs.shape
    f = w_gu.shape[2] // 2
    nf = f // tn
    grid_spec = pltpu.PrefetchScalarGridSpec(
        num_scalar_prefetch=2,
        grid=(nf, r // bm),
        in_specs=[
            pl.BlockSpec((bm, d), lambda n, b, be, nv: (b, 0)),
            pl.BlockSpec((None, d, tn), lambda n, b, be, nv: (be[b], 0, n)),
            pl.BlockSpec((None, d, tn), lambda n, b, be, nv: (be[b], 0, nf + n)),
            pl.BlockSpec((None, 1, tn), lambda n, b, be, nv: (be[b], 0, n)),
            pl.BlockSpec((None, 1, tn), lambda n, b, be, nv: (be[b], 0, nf + n)),
        ],
        out_specs=pl.BlockSpec((bm, tn), lambda n, b, be, nv: (b, n)),
        scratch_shapes=[pltpu.VMEM((d, tn), BF16), pltpu.VMEM((d, tn), BF16)],
    )
    return pl.pallas_call(
        _moe_up_kernel,
        grid_spec=grid_spec,
        out_shape=jax.ShapeDtypeStruct((r, f), BF16),
        compiler_params=_cparams("arbitrary", "arbitrary"),
        name="moe_up",
    )(block_exp, n_valid, xs, w_gu, w_gu, b_gu, b_gu)


def _moe_down_kernel(be_ref, nv_ref, a_ref, w_ref, b_ref, p_ref, o_ref, ws_ref):
    b = pl.program_id(1)
    fresh = jnp.logical_or(b == 0, be_ref[b] != be_ref[jnp.maximum(b - 1, 0)])

    @pl.when(fresh)
    def _():
        ws_ref[...] = w_ref[...].astype(BF16)

    @pl.when(b < nv_ref[0])
    def _():
        y = jnp.dot(a_ref[...], ws_ref[...], preferred_element_type=F32) + b_ref[...]
        o_ref[...] = (y * p_ref[...]).astype(BF16)

    @pl.when(b >= nv_ref[0])
    def _():
        o_ref[...] = jnp.zeros_like(o_ref)


def moe_down(act, block_exp, n_valid, w_down, b_down, p_row, *, bm, tn):
    r, f = act.shape
    d = w_down.shape[2]
    grid_spec = pltpu.PrefetchScalarGridSpec(
        num_scalar_prefetch=2,
        grid=(d // tn, r // bm),
        in_specs=[
            pl.BlockSpec((bm, f), lambda n, b, be, nv: (b, 0)),
            pl.BlockSpec((None, f, tn), lambda n, b, be, nv: (be[b], 0, n)),
            pl.BlockSpec((None, 1, tn), lambda n, b, be, nv: (be[b], 0, n)),
            pl.BlockSpec((bm, 1), lambda n, b, be, nv: (b, 0)),
        ],
        out_specs=pl.BlockSpec((bm, tn), lambda n, b, be, nv: (b, n)),
        scratch_shapes=[pltpu.VMEM((f, tn), BF16)],
    )
    return pl.pallas_call(
        _moe_down_kernel,
        grid_spec=grid_spec,
        out_shape=jax.ShapeDtypeStruct((r, d), BF16),
        compiler_params=_cparams("arbitrary", "arbitrary"),
        name="moe_down",
    )(block_exp, n_valid, act, w_down, b_down, p_row)


def _combine_kernel(x_ref, y_ref, g2_ref, lg_ref, lb_ref, o_ref, *, alpha, d):
    ffn = y_ref[:, 0:d].astype(F32)
    for k in range(1, TOP_K):
        ffn = ffn + y_ref[:, k * d:(k + 1) * d].astype(F32)
    o_ref[...] = _ln_rows(alpha * x_ref[...] + g2_ref[...] * ffn) * lg_ref[...] + lb_ref[...]


def combine_ln(x1, yg, mod3, row_of_block, ln_g, ln_b, *, alpha, tm):
    t, d = x1.shape
    vec = pl.BlockSpec((1, d), lambda i: (0, 0))
    return pl.pallas_call(
        functools.partial(_combine_kernel, alpha=alpha, d=d),
        grid=(t // tm,),
        in_specs=[pl.BlockSpec((tm, d), lambda i: (i, 0)),
                  pl.BlockSpec((tm, TOP_K * d), lambda i: (i, 0)),
                  pl.BlockSpec((None, 1, d), lambda i: (row_of_block(i), 0, 5)),
                  vec, vec],
        out_specs=pl.BlockSpec((tm, d), lambda i: (i, 0)),
        out_shape=jax.ShapeDtypeStruct((t, d), F32),
        compiler_params=_cparams("parallel"),
        name="combine_ln",
    )(x1, yg, mod3, ln_g, ln_b)


def route_layout(top_idx, *, bm, n_rows):
    t = top_idx.shape[0]
    e_flat = top_idx.reshape(-1)
    onehot = (e_flat[:, None] == jnp.arange(N_EXP, dtype=jnp.int32)[None, :]).astype(jnp.int32)
    csum = jnp.cumsum(onehot, axis=0)
    counts = csum[-1]
    rank = jnp.sum((csum - 1) * onehot, axis=1)
    padded = (counts + bm - 1) // bm * bm
    pad_end = jnp.cumsum(padded)
    pad_start = pad_end - padded
    dest = pad_start[e_flat] + rank
    row_tok = jnp.zeros((n_rows,), jnp.int32).at[dest].set(jnp.arange(t * TOP_K, dtype=jnp.int32) // TOP_K)
    nb = n_rows // bm
    blk_start = jnp.arange(nb, dtype=jnp.int32) * bm
    block_exp = jnp.minimum(jnp.searchsorted(pad_end, blk_start, side="right"), N_EXP - 1).astype(jnp.int32)
    n_valid = (pad_end[-1] // bm).astype(jnp.int32).reshape(1)
    return dest, row_tok, block_exp, n_valid


def grid_pos_embed(n_tok, d):
    rows = n_tok // GRID_W
    r = jnp.repeat(jnp.arange(rows, dtype=F32), GRID_W)
    col = jnp.tile(jnp.arange(GRID_W, dtype=F32), rows)
    quarter = d // 4
    omega = 1.0 / (10000.0 ** (jnp.arange(quarter, dtype=F32) / quarter))

    def emb(p):
        a = p[:, None] * omega[None, :]
        return jnp.concatenate([jnp.sin(a), jnp.cos(a)], axis=-1)

    return jnp.concatenate([emb(r), emb(col)], axis=-1)


def kernel(x_prompt, x_sample, state_hgrn, c, c_ctx, w_ada, b_ada, w_in, b_in, lb_logits, g_norm_a, conv_w, conv_b, conv_ln_g, conv_ln_b, w_br_a, w_br_b, w_br_c, b_br, w_gate, b_gate, w_o, b_o, ln1_g, ln1_b, w_router, b_router, w_gu, b_gu, w_down, b_down, ln2_g, ln2_b):
    depth = w_in.shape[0]
    bp, lp, d = x_prompt.shape
    bs, ls, _ = x_sample.shape
    tp, ts = bp * lp, bs * ls
    t = tp + ts
    alpha = (2 * depth) ** 0.25
    w_in_cols = w_in.shape[2]
    col_b = 3 * H_A * DK + 2 * W_A
    col_c = col_b + W_B

    tm, tn, tm_o, bm = TM, TN, TM_OUT, MOE_BM
    n_rows = (t * TOP_K // bm + N_EXP) * bm

    def row_of_block_for(rows_per_block):
        npb = tp // rows_per_block
        per = ls // rows_per_block
        return lambda i: jnp.where(i < npb, 0, 1 + (i - npb) // per)

    sm = jax.nn.softmax(lb_logits.astype(F32), axis=0)
    cs = jnp.cumsum(sm, axis=0)
    lower = cs - cs[0:1]

    cond = jnp.zeros((16, d), F32).at[0].set(c_ctx).at[1:1 + bs].set(c)
    mod = ada_mod(cond, w_ada, b_ada)

    xs = x_sample + grid_pos_embed(ls, d)[None]
    x = jnp.concatenate([x_prompt.reshape(tp, d), xs.reshape(ts, d)], axis=0)

    dl_p, cm = dft_tables(lp)
    dl_s, _ = dft_tables(ls)

    states = []
    for l in range(depth):
        mod3 = mod[l].reshape(16, 1, 6 * d)
        w_cat = jnp.concatenate([w_in[l], w_gate[l]], axis=1).astype(BF16)
        b_cat = jnp.concatenate([b_in[l], b_gate[l]])[None, :]
        u = ln_mod_matmul(x, mod3, row_of_block_for(tm), w_cat, b_cat, n_plain_cols=w_in_cols, tm=tm, tn=tn)

        gn = g_norm_a[l][None, :]
        ya_p, s_ctx = hgrn_mixer(u, lower[l], gn, None, row0=0, batch=bp, seq=lp, want_state=True)
        (ya_s,) = hgrn_mixer(u, lower[l], gn, state_hgrn[:, l], row0=tp, batch=bs, seq=ls, want_state=False)
        states.append(s_ctx)
        yb_p = fourier_mixer(u, dl_p, cm, row0=0, col0=col_b, batch=bp, seq=lp)
        yb_s = fourier_mixer(u, dl_s, cm, row0=tp, col0=col_b, batch=bs, seq=ls)
        cw = (conv_w[l], conv_b[l][None, :], conv_ln_g[l][None, :], conv_ln_b[l][None, :])
        yc_p = conv_mixer(u, *cw, row0=0, col0=col_c, batch=bp, seq=lp)
        yc_s = conv_mixer(u, *cw, row0=tp, col0=col_c, batch=bs, seq=ls)
        ya = jnp.concatenate([ya_p, ya_s], axis=0)
        yb = jnp.concatenate([yb_p, yb_s], axis=0)
        yc = jnp.concatenate([yc_p, yc_s], axis=0)

        merged = branch_merge(ya, yb, yc, u, w_br_a[l].astype(BF16), w_br_b[l].astype(BF16),
                              w_br_c[l].astype(BF16), b_br[l], gate_col0=w_in_cols, d=d, tm=tm, tn=tn)

        wr = jnp.zeros((d, LANES), BF16).at[:, :N_EXP].set(w_router[l].astype(BF16))
        br = jnp.full((1, LANES), -1e30, F32).at[0, :N_EXP].set(b_router[l])
        x1, h2, idx, prob = out_router(x, merged, mod3, row_of_block_for(tm_o), w_o[l].astype(BF16),
                                       b_o[l][None, :], ln1_g[l][None, :], ln1_b[l][None, :], wr, br,
                                       alpha=alpha, tm=tm_o)

        top_idx = idx[:, :TOP_K]
        dest, row_tok, block_exp, n_valid = route_layout(top_idx, bm=bm, n_rows=n_rows)
        p_row = jnp.zeros((n_rows, 1), F32).at[dest, 0].set(prob[:, :TOP_K].reshape(-1))
        xsort = jnp.take(h2, row_tok, axis=0)
        act = moe_up(xsort, block_exp, n_valid, w_gu[l], b_gu[l][:, None, :], bm=bm, tn=tn)
        yrow = moe_down(act, block_exp, n_valid, w_down[l], b_down[l][:, None, :], p_row, bm=bm, tn=tn)
        yg = jnp.take(yrow, dest, axis=0).reshape(t, TOP_K * d)
        x = combine_ln(x1, yg, mod3, row_of_block_for(tm_o), ln2_g[l][None, :], ln2_b[l][None, :],
                       alpha=alpha, tm=tm_o)

    y_prompt = x[:tp].reshape(bp, lp, d)
    y_sample = x[tp:].reshape(bs, ls, d)
    return y_prompt, y_sample, jnp.stack(states, axis=1)
```

```python
import functools
import math

import jax
import jax.numpy as jnp
from jax import lax
from jax.experimental import pallas as pl
from jax.experimental.pallas import tpu as pltpu

F32 = jnp.float32
BF16 = jnp.bfloat16

GRID_W = 64
H_A = 8
DK = 128
DV = 128
W_A = H_A * DV
W_B = 512
W_C = 512
CONV_K = 31
N_BRANCH = 3
N_EXP = 32
TOP_K = 4
SWIGLU_LIMIT = 7.0
SWIGLU_ALPHA = 1.702
LN_EPS = 1e-5
RMS_EPS = 1e-6

LANES = 128
SUBLANES = 8
VMEM_LIMIT = 56 * 1024 * 1024
HGRN_CHUNK = 16
HGRN_UNROLL = 8
CONV_PAD = 16
CONV_ROWS = 32
LOGF_FLOOR = -1e4
LOG2E = 1.4426950408889634
TM = 1024
TN = 512
TM_OUT = 256
MOE_BM = 512
TN_DOWN = 1024
RANK_ROWS = 512


def _cparams(*sem):
    return pltpu.CompilerParams(dimension_semantics=sem, vmem_limit_bytes=VMEM_LIMIT)


def _ln_rows(x):
    mu = jnp.mean(x, axis=-1, keepdims=True)
    xc = x - mu
    var = jnp.mean(xc * xc, axis=-1, keepdims=True)
    return xc * lax.rsqrt(var + LN_EPS)


def _sigmoid(x):
    return 1.0 / (1.0 + jnp.exp(-x))


def _ada_kernel(c_ref, w_ref, b_ref, o_ref):
    c = c_ref[...]
    a = (c * _sigmoid(c)).astype(BF16)
    o_ref[...] = jnp.dot(a, w_ref[...].astype(BF16), preferred_element_type=F32) + b_ref[...]


def ada_mod(cond, w_ada, b_ada, *, tn=1024):
    depth, d, n = w_ada.shape
    r = cond.shape[0]
    return pl.pallas_call(
        _ada_kernel,
        grid=(depth, n // tn),
        in_specs=[
            pl.BlockSpec((r, d), lambda l, j: (0, 0)),
            pl.BlockSpec((None, d, tn), lambda l, j: (l, 0, j)),
            pl.BlockSpec((None, 1, tn), lambda l, j: (l, 0, j)),
        ],
        out_specs=pl.BlockSpec((None, r, tn), lambda l, j: (l, 0, j)),
        out_shape=jax.ShapeDtypeStruct((depth, r, n), F32),
        compiler_params=_cparams("parallel", "parallel"),
        name="ada_mod",
    )(cond, w_ada, b_ada.reshape(depth, 1, n))


def _lnmm_kernel(x_ref, sh_ref, sc_ref, w_ref, b_ref, o_ref, h_ref, *, n_plain):
    j = pl.program_id(1)

    @pl.when(j == 0)
    def _():
        y = _ln_rows(x_ref[...])
        h_ref[...] = (y * (1.0 + sc_ref[...]) + sh_ref[...]).astype(BF16)

    acc = jnp.dot(h_ref[...], w_ref[...], preferred_element_type=F32) + b_ref[...]

    @pl.when(j < n_plain)
    def _():
        o_ref[...] = acc.astype(BF16)

    @pl.when(j >= n_plain)
    def _():
        o_ref[...] = _sigmoid(acc).astype(BF16)


def ln_mod_matmul(x, mod3, row_of_block, w, b, *, n_plain_cols, tm, tn):
    t, d = x.shape
    n = w.shape[1]
    return pl.pallas_call(
        functools.partial(_lnmm_kernel, n_plain=n_plain_cols // tn),
        grid=(t // tm, n // tn),
        in_specs=[
            pl.BlockSpec((tm, d), lambda i, j: (i, 0)),
            pl.BlockSpec((None, 1, d), lambda i, j: (row_of_block(i), 0, 0)),
            pl.BlockSpec((None, 1, d), lambda i, j: (row_of_block(i), 0, 1)),
            pl.BlockSpec((d, tn), lambda i, j: (0, j)),
            pl.BlockSpec((1, tn), lambda i, j: (0, j)),
        ],
        out_specs=pl.BlockSpec((tm, tn), lambda i, j: (i, j)),
        out_shape=jax.ShapeDtypeStruct((t, n), BF16),
        scratch_shapes=[pltpu.VMEM((tm, d), BF16)],
        compiler_params=_cparams("parallel", "arbitrary"),
        name="ln_mod_matmul",
    )(x, mod3, mod3, w, b)


def _chunk_scan(x_h, direction):
    half = SUBLANES
    rid = lax.broadcasted_iota(jnp.int32, x_h[0].shape, 0)
    out = []
    for x in x_h:
        step = 1
        while step < half:
            if direction == 0:
                x = x + jnp.where(rid >= step, pltpu.roll(x, step, axis=0), 0.0)
            else:
                x = x + jnp.where(rid < half - step, pltpu.roll(x, half - step, axis=0), 0.0)
            step *= 2
        out.append(x)
    if direction == 0:
        out[1] = out[1] + out[0][half - 1:half, :]
    else:
        out[0] = out[0] + out[1][0:1, :]
    return out


def _hgrn_chunk(q_ref, z_ref, v_ref, lb, oml, st_ref, o_ref, rows_ref, c0, *, direction):
    half = SUBLANES
    rows = pl.ds(c0, HGRN_CHUNK)
    q = q_ref[rows, :].astype(F32) * (DK ** -0.5)
    z = z_ref[rows, :].astype(F32)
    v = v_ref[rows, :]
    e = jnp.exp2(jnp.abs(z) * (-LOG2E))
    r = 1.0 / (1.0 + e)
    er = e * r
    pos = z >= 0
    lf2 = jnp.maximum(jnp.log2(lb + oml * jnp.where(pos, r, er)), LOGF_FLOOR)
    lk2 = jnp.log2(oml * jnp.where(pos, er, r))

    b_h = _chunk_scan([lf2[0:half, :], lf2[half:HGRN_CHUNK, :]], direction)
    b2 = jnp.concatenate(b_h, axis=0)
    b_end = b_h[1][half - 1:half, :] if direction == 0 else b_h[0][0:1, :]

    st = st_ref[direction]
    qe = (q * jnp.exp2(b2)).astype(BF16)
    o_inter = lax.dot_general(qe, st.astype(BF16), (((1,), (1,)), ((), ())), preferred_element_type=F32)

    rows_ref[0] = b2 - lk2
    rows_ref[1] = v.astype(F32)
    q_h = (q[0:half, :], q[half:HGRN_CHUNK, :])
    acc = [jnp.zeros((half, DV), F32), jnp.zeros((half, DV), F32)]
    rid = lax.broadcasted_iota(jnp.int32, (half, DK), 0)
    for s in range(HGRN_CHUNK):
        hs, rs = divmod(s, half)
        bs = jnp.broadcast_to(rows_ref[0, s:s + 1, :], (half, DK))
        vs = jnp.broadcast_to(rows_ref[1, s:s + 1, :], (half, DV))
        for ht in range(2):
            if (ht < hs) if direction == 0 else (ht > hs):
                continue
            d = b_h[ht] - bs
            if ht == hs:
                keep = (rid >= rs) if direction == 0 else (rid <= rs)
                d = jnp.where(keep, d, -1e30)
            w = jnp.sum(q_h[ht] * jnp.exp2(d), axis=-1, keepdims=True)
            acc[ht] = acc[ht] + w * vs
    o_ref[pl.ds(c0, half), :] = acc[0] + o_inter[0:half, :]
    o_ref[pl.ds(c0 + half, half), :] = acc[1] + o_inter[half:HGRN_CHUNK, :]

    kk = jnp.exp2(lk2 + (b_end - b2)).astype(BF16)
    upd = lax.dot_general(v, kk, (((0,), (0,)), ((), ())), preferred_element_type=F32)
    st_ref[direction] = st * jnp.exp2(b_end) + upd


def _hgrn_kernel(*refs, seq, unroll, has_s0, want_state):
    q_ref, zf_ref, zb_ref, v_ref, g_ref, lb_ref, gn_ref = refs[:7]
    pos = 7
    s0_ref = None
    if has_s0:
        s0_ref = refs[pos]
        pos += 1
    y_ref = refs[pos]
    pos += 1
    sfin_ref = None
    if want_state:
        sfin_ref = refs[pos]
        pos += 1
    of_ref, ob_ref, st_ref, rows_ref = refs[pos:pos + 4]

    for d in range(2):
        st_ref[d] = s0_ref[d].T if has_s0 else jnp.zeros((DV, DK), F32)

    n = seq // HGRN_CHUNK
    lb_f = lb_ref[0:1, :]
    lb_b = lb_ref[1:2, :]
    oml_f = 1.0 - lb_f
    oml_b = 1.0 - lb_b

    def body(i, carry):
        for j in range(unroll):
            cf = pl.multiple_of((i * unroll + j) * HGRN_CHUNK, HGRN_CHUNK)
            cb = pl.multiple_of((n - 1 - (i * unroll + j)) * HGRN_CHUNK, HGRN_CHUNK)
            _hgrn_chunk(q_ref, zf_ref, v_ref, lb_f, oml_f, st_ref, of_ref, rows_ref.at[2 * j], cf, direction=0)
            _hgrn_chunk(q_ref, zb_ref, v_ref, lb_b, oml_b, st_ref, ob_ref, rows_ref.at[2 * j + 1], cb, direction=1)
        return carry

    lax.fori_loop(0, n // unroll, body, 0)

    blk = min(seq, 256)
    for r0 in range(0, seq, blk):
        o = of_ref[r0:r0 + blk, :] + ob_ref[r0:r0 + blk, :]
        g = g_ref[r0:r0 + blk, :].astype(F32)
        o = o * lax.rsqrt(jnp.mean(o * o, axis=-1, keepdims=True) + RMS_EPS) * gn_ref[...]
        y_ref[r0:r0 + blk, :] = (o * (g * _sigmoid(g))).astype(BF16)
    if want_state:
        sfin_ref[0] = st_ref[0].T
        sfin_ref[1] = st_ref[1].T


def hgrn_mixer(u, lb, g_norm, s0, *, row0, batch, seq, want_state, unroll=HGRN_UNROLL):
    rb0 = row0 // seq
    has_s0 = s0 is not None
    unroll = math.gcd(seq // HGRN_CHUNK, unroll)

    def sec(k):
        return pl.BlockSpec((seq, LANES), lambda b, h: (rb0 + b, k * H_A + h))

    in_specs = [sec(0), sec(1), sec(2), sec(3), sec(4),
                pl.BlockSpec((2, DK), lambda b, h: (0, h)),
                pl.BlockSpec((1, DV), lambda b, h: (0, 0))]
    args = [u, u, u, u, u, lb, g_norm]
    state_spec = pl.BlockSpec((None, 2, None, DK, DV), lambda b, h: (b, 0, h, 0, 0))
    if has_s0:
        in_specs.append(state_spec)
        args.append(s0)
    out_specs = [pl.BlockSpec((seq, DV), lambda b, h: (b, h))]
    out_shape = [jax.ShapeDtypeStruct((batch * seq, W_A), BF16)]
    if want_state:
        out_specs.append(state_spec)
        out_shape.append(jax.ShapeDtypeStruct((batch, 2, H_A, DK, DV), F32))
    return pl.pallas_call(
        functools.partial(_hgrn_kernel, seq=seq, unroll=unroll, has_s0=has_s0, want_state=want_state),
        grid=(batch, H_A),
        in_specs=in_specs,
        out_specs=out_specs,
        out_shape=out_shape,
        scratch_shapes=[pltpu.VMEM((seq, DV), F32), pltpu.VMEM((seq, DV), F32), pltpu.VMEM((2, DV, DK), F32),
                        pltpu.VMEM((2 * unroll, 2, HGRN_CHUNK, DK), F32)],
        compiler_params=_cparams("parallel", "parallel"),
        name="hgrn_mixer",
    )(*args)


def _fourier_kernel(u_ref, cm_ref, dl_ref, o_ref, ab_ref, *, seq, rows, scale):
    @pl.when(pl.program_id(1) == 0)
    def _():
        for r0 in range(0, seq, rows):
            ab = jnp.dot(u_ref[r0:r0 + rows, :], cm_ref[...], preferred_element_type=F32)
            ab_ref[r0:r0 + rows, :] = ab[:, :W_B].astype(BF16)
            ab_ref[seq + r0:seq + r0 + rows, :] = ab[:, W_B:].astype(BF16)

    o_ref[...] = (jnp.dot(dl_ref[...], ab_ref[...], preferred_element_type=F32) * scale).astype(BF16)


def dft_tables(seq):
    def cs(n):
        i = lax.iota(jnp.int32, n)
        ang = ((i[:, None] * i[None, :]) % n).astype(F32) * (2.0 * math.pi / n)
        return jnp.cos(ang), jnp.sin(ang)
    cl, sl = cs(seq)
    cc, sc = cs(W_B)
    return jnp.concatenate([cl, -sl], axis=1).astype(BF16), jnp.concatenate([cc, sc], axis=1).astype(BF16)


def fourier_mixer(u, dl, cm, *, row0, col0, batch, seq, rows=256):
    rows = min(rows, seq)
    rb0 = row0 // seq
    cb = col0 // W_B
    nt = seq // rows
    return pl.pallas_call(
        functools.partial(_fourier_kernel, seq=seq, rows=rows, scale=1.0 / math.sqrt(seq * W_B)),
        grid=(batch, nt),
        in_specs=[
            pl.BlockSpec((seq, W_B), lambda b, i: (rb0 + b, cb)),
            pl.BlockSpec((W_B, 2 * W_B), lambda b, i: (0, 0)),
            pl.BlockSpec((rows, 2 * seq), lambda b, i: (i, 0)),
        ],
        out_specs=pl.BlockSpec((rows, W_B), lambda b, i: (b * nt + i, 0)),
        out_shape=jax.ShapeDtypeStruct((batch * seq, W_B), BF16),
        scratch_shapes=[pltpu.VMEM((2 * seq, W_B), BF16)],
        compiler_params=_cparams("parallel", "arbitrary"),
        name="fourier_mixer",
    )(u, cm, dl)


def _conv_kernel(a_ref, gt_ref, w_ref, b_ref, g_ref, be_ref, o_ref, z_ref, *, seq):
    zeros = jnp.zeros((CONV_PAD, W_C), F32)
    z_ref[0:CONV_PAD, :] = zeros
    z_ref[CONV_PAD + seq:2 * CONV_PAD + seq, :] = zeros
    blk = min(seq, 256)
    for r0 in range(0, seq, blk):
        a = a_ref[r0:r0 + blk, :].astype(F32)
        gt = gt_ref[r0:r0 + blk, :].astype(F32)
        z_ref[CONV_PAD + r0:CONV_PAD + r0 + blk, :] = a * _sigmoid(gt)

    off = CONV_PAD - CONV_K // 2

    def tile(i, carry):
        r0 = pl.multiple_of(i * CONV_ROWS, CONV_ROWS)
        win = z_ref[pl.ds(r0, CONV_ROWS + 2 * CONV_PAD), :]
        acc = jnp.zeros((CONV_ROWS, W_C), F32)
        for k in range(CONV_K):
            acc = acc + win[off + k:off + k + CONV_ROWS, :] * w_ref[k:k + 1, :]
        y = _ln_rows(acc + b_ref[...]) * g_ref[...] + be_ref[...]
        o_ref[pl.ds(r0, CONV_ROWS), :] = (y * _sigmoid(y)).astype(BF16)
        return carry

    lax.fori_loop(0, seq // CONV_ROWS, tile, 0)


def conv_mixer(u, conv_w, conv_b, ln_g, ln_b, *, row0, col0, batch, seq):
    rb0 = row0 // seq
    cb = col0 // W_C
    vec = pl.BlockSpec((1, W_C), lambda b: (0, 0))
    return pl.pallas_call(
        functools.partial(_conv_kernel, seq=seq),
        grid=(batch,),
        in_specs=[
            pl.BlockSpec((seq, W_C), lambda b: (rb0 + b, cb)),
            pl.BlockSpec((seq, W_C), lambda b: (rb0 + b, cb + 1)),
            pl.BlockSpec((CONV_K, W_C), lambda b: (0, 0)),
            vec, vec, vec,
        ],
        out_specs=pl.BlockSpec((seq, W_C), lambda b: (b, 0)),
        out_shape=jax.ShapeDtypeStruct((batch * seq, W_C), BF16),
        scratch_shapes=[pltpu.VMEM((seq + 2 * CONV_PAD, W_C), F32)],
        compiler_params=_cparams("parallel"),
        name="conv_mixer",
    )(u, u, conv_w, conv_b, ln_g, ln_b)


def _branch_kernel(ya_ref, yb_ref, yc_ref, g0_ref, g1_ref, g2_ref, wa_ref, wb_ref, wc_ref, b_ref, o_ref):
    def proj(y_ref, w_ref, k):
        return jnp.dot(y_ref[...], w_ref[...], preferred_element_type=F32) + b_ref[k:k + 1, :]
    m = g0_ref[...].astype(F32) * proj(ya_ref, wa_ref, 0)
    m = m + g1_ref[...].astype(F32) * proj(yb_ref, wb_ref, 1)
    m = m + g2_ref[...].astype(F32) * proj(yc_ref, wc_ref, 2)
    o_ref[...] = m.astype(BF16)


def branch_merge(ya, yb, yc, u, wa, wb, wc, b_br, *, gate_col0, d, tm, tn):
    t = ya.shape[0]
    gb = gate_col0 // tn
    nd = d // tn

    def gate(k):
        return pl.BlockSpec((tm, tn), lambda i, j: (i, gb + k * nd + j))

    return pl.pallas_call(
        _branch_kernel,
        grid=(t // tm, nd),
        in_specs=[
            pl.BlockSpec((tm, ya.shape[1]), lambda i, j: (i, 0)),
            pl.BlockSpec((tm, yb.shape[1]), lambda i, j: (i, 0)),
            pl.BlockSpec((tm, yc.shape[1]), lambda i, j: (i, 0)),
            gate(0), gate(1), gate(2),
            pl.BlockSpec((wa.shape[0], tn), lambda i, j: (0, j)),
            pl.BlockSpec((wb.shape[0], tn), lambda i, j: (0, j)),
            pl.BlockSpec((wc.shape[0], tn), lambda i, j: (0, j)),
            pl.BlockSpec((N_BRANCH, tn), lambda i, j: (0, j)),
        ],
        out_specs=pl.BlockSpec((tm, tn), lambda i, j: (i, j)),
        out_shape=jax.ShapeDtypeStruct((t, d), BF16),
        compiler_params=_cparams("parallel", "arbitrary"),
        name="branch_merge",
    )(ya, yb, yc, u, u, u, wa, wb, wc, b_br)


def _out_router_kernel(x_ref, m_ref, g1_ref, sh2_ref, sc2_ref, wo_ref, bo_ref, lg_ref, lb_ref,
                       wr_ref, br_ref, x1_ref, h2_ref, idx_ref, p_ref, *, alpha):
    mix = jnp.dot(m_ref[...], wo_ref[...], preferred_element_type=F32) + bo_ref[...]
    x1 = _ln_rows(alpha * x_ref[...] + g1_ref[...] * mix) * lg_ref[...] + lb_ref[...]
    x1_ref[...] = x1
    h2 = (_ln_rows(x1) * (1.0 + sc2_ref[...]) + sh2_ref[...]).astype(BF16)
    h2_ref[...] = h2
    logits = jnp.dot(h2, wr_ref[...], preferred_element_type=F32) + br_ref[...]
    lane = lax.broadcasted_iota(jnp.int32, logits.shape, 1)
    idx_out = jnp.zeros(logits.shape, jnp.int32)
    p_out = jnp.zeros(logits.shape, F32)
    top = None
    denom = None
    for k in range(TOP_K):
        m = jnp.max(logits, axis=-1, keepdims=True)
        i = jnp.min(jnp.where(logits == m, lane, LANES), axis=-1, keepdims=True)
        if k == 0:
            top = m
        ek = jnp.exp(m - top)
        denom = ek if k == 0 else denom + ek
        idx_out = jnp.where(lane == k, i, idx_out)
        p_out = jnp.where(lane == k, ek, p_out)
        logits = jnp.where(lane == i, -jnp.inf, logits)
    idx_ref[...] = idx_out
    p_ref[...] = p_out / denom


def out_router(x, merged, mod3, row_of_block, wo, bo, ln_g, ln_b, wr, br, *, alpha, tm):
    t, d = x.shape

    def modspec(k):
        return pl.BlockSpec((None, 1, d), lambda i: (row_of_block(i), 0, k))

    vec = pl.BlockSpec((1, d), lambda i: (0, 0))
    row_f = pl.BlockSpec((tm, d), lambda i: (i, 0))
    row_s = pl.BlockSpec((tm, LANES), lambda i: (i, 0))
    return pl.pallas_call(
        functools.partial(_out_router_kernel, alpha=alpha),
        grid=(t // tm,),
        in_specs=[row_f, row_f, modspec(2), modspec(3), modspec(4),
                  pl.BlockSpec((d, d), lambda i: (0, 0)), vec, vec, vec,
                  pl.BlockSpec((d, LANES), lambda i: (0, 0)),
                  pl.BlockSpec((1, LANES), lambda i: (0, 0))],
        out_specs=[row_f, row_f, row_s, row_s],
        out_shape=[jax.ShapeDtypeStruct((t, d), F32), jax.ShapeDtypeStruct((t, d), BF16),
                   jax.ShapeDtypeStruct((t, LANES), jnp.int32), jax.ShapeDtypeStruct((t, LANES), F32)],
        compiler_params=_cparams("parallel"),
        name="out_router",
    )(x, merged, mod3, mod3, mod3, wo, bo, ln_g, ln_b, wr, br)


def _moe_up_kernel(be_ref, nv_ref, x_ref, wg_ref, wu_ref, bg_ref, bu_ref, o_ref, wgs_ref, wus_ref):
    b = pl.program_id(1)
    fresh = jnp.logical_or(b == 0, be_ref[b] != be_ref[jnp.maximum(b - 1, 0)])

    @pl.when(fresh)
    def _():
        wgs_ref[...] = wg_ref[...].astype(BF16)
        wus_ref[...] = wu_ref[...].astype(BF16)

    @pl.when(b < nv_ref[0])
    def _():
        x = x_ref[...]
        gt = jnp.dot(x, wgs_ref[...], preferred_element_type=F32) + bg_ref[...]
        up = jnp.dot(x, wus_ref[...], preferred_element_type=F32) + bu_ref[...]
        gt = jnp.minimum(gt, SWIGLU_LIMIT)
        up = jnp.clip(up, -SWIGLU_LIMIT, SWIGLU_LIMIT)
        o_ref[...] = ((up + 1.0) * (gt * _sigmoid(SWIGLU_ALPHA * gt))).astype(BF16)

    @pl.when(b >= nv_ref[0])
    def _():
        o_ref[...] = jnp.zeros_like(o_ref)


def moe_up(xs, block_exp, n_valid, w_gu, b_gu, layer, *, bm, tn):
    r, d = xs.shape
    f = w_gu.shape[3] // 2
    nf = f // tn
    grid_spec = pltpu.PrefetchScalarGridSpec(
        num_scalar_prefetch=2,
        grid=(nf, r // bm),
        in_specs=[
            pl.BlockSpec((bm, d), lambda n, b, be, nv: (b, 0)),
            pl.BlockSpec((None, None, d, tn), lambda n, b, be, nv: (layer, be[b], 0, n)),
            pl.BlockSpec((None, None, d, tn), lambda n, b, be, nv: (layer, be[b], 0, nf + n)),
            pl.BlockSpec((None, None, 1, tn), lambda n, b, be, nv: (layer, be[b], 0, n)),
            pl.BlockSpec((None, None, 1, tn), lambda n, b, be, nv: (layer, be[b], 0, nf + n)),
        ],
        out_specs=pl.BlockSpec((bm, tn), lambda n, b, be, nv: (b, n)),
        scratch_shapes=[pltpu.VMEM((d, tn), BF16), pltpu.VMEM((d, tn), BF16)],
    )
    return pl.pallas_call(
        _moe_up_kernel,
        grid_spec=grid_spec,
        out_shape=jax.ShapeDtypeStruct((r, f), BF16),
        compiler_params=_cparams("arbitrary", "arbitrary"),
        name="moe_up",
    )(block_exp, n_valid, xs, w_gu, w_gu, b_gu, b_gu)


def _moe_down_kernel(be_ref, nv_ref, a_ref, w_ref, b_ref, o_ref, ws_ref):
    b = pl.program_id(1)
    fresh = jnp.logical_or(b == 0, be_ref[b] != be_ref[jnp.maximum(b - 1, 0)])

    @pl.when(fresh)
    def _():
        ws_ref[...] = w_ref[...].astype(BF16)

    @pl.when(b < nv_ref[0])
    def _():
        y = jnp.dot(a_ref[...], ws_ref[...], preferred_element_type=F32) + b_ref[...]
        o_ref[...] = y.astype(BF16)

    @pl.when(b >= nv_ref[0])
    def _():
        o_ref[...] = jnp.zeros_like(o_ref)


def moe_down(act, block_exp, n_valid, w_down, b_down, layer, *, bm, tn):
    r, f = act.shape
    d = w_down.shape[3]
    grid_spec = pltpu.PrefetchScalarGridSpec(
        num_scalar_prefetch=2,
        grid=(d // tn, r // bm),
        in_specs=[
            pl.BlockSpec((bm, f), lambda n, b, be, nv: (b, 0)),
            pl.BlockSpec((None, None, f, tn), lambda n, b, be, nv: (layer, be[b], 0, n)),
            pl.BlockSpec((None, None, 1, tn), lambda n, b, be, nv: (layer, be[b], 0, n)),
        ],
        out_specs=pl.BlockSpec((bm, tn), lambda n, b, be, nv: (b, n)),
        scratch_shapes=[pltpu.VMEM((f, tn), BF16)],
    )
    return pl.pallas_call(
        _moe_down_kernel,
        grid_spec=grid_spec,
        out_shape=jax.ShapeDtypeStruct((r, d), BF16),
        compiler_params=_cparams("arbitrary", "arbitrary"),
        name="moe_down",
    )(block_exp, n_valid, act, w_down, b_down)


def _combine_kernel(x_ref, y0_ref, y1_ref, y2_ref, y3_ref, p_ref, g2_ref, lg_ref, lb_ref, o_ref, *, alpha):
    p = p_ref[...]
    ffn = None
    for k, y_ref in enumerate((y0_ref, y1_ref, y2_ref, y3_ref)):
        term = p[:, k:k + 1] * y_ref[...].astype(F32)
        ffn = term if ffn is None else ffn + term
    o_ref[...] = _ln_rows(alpha * x_ref[...] + g2_ref[...] * ffn) * lg_ref[...] + lb_ref[...]


def combine_ln(x1, yg, prob, mod3, row_of_block, ln_g, ln_b, *, alpha, tm):
    t, d = x1.shape
    nb = t // tm
    vec = pl.BlockSpec((1, d), lambda i: (0, 0))

    def sel(k):
        return pl.BlockSpec((tm, d), lambda i: (k * nb + i, 0))

    return pl.pallas_call(
        functools.partial(_combine_kernel, alpha=alpha),
        grid=(nb,),
        in_specs=[pl.BlockSpec((tm, d), lambda i: (i, 0)), sel(0), sel(1), sel(2), sel(3),
                  pl.BlockSpec((tm, LANES), lambda i: (i, 0)),
                  pl.BlockSpec((None, 1, d), lambda i: (row_of_block(i), 0, 5)),
                  vec, vec],
        out_specs=pl.BlockSpec((tm, d), lambda i: (i, 0)),
        out_shape=jax.ShapeDtypeStruct((t, d), F32),
        compiler_params=_cparams("parallel"),
        name="combine_ln",
    )(x1, yg, yg, yg, yg, prob, mod3, ln_g, ln_b)


def _rank_kernel(idx_ref, rank_ref, cnt_ref, carry_ref):
    @pl.when(pl.program_id(0) == 0)
    def _():
        carry_ref[...] = jnp.zeros_like(carry_ref)

    idx = idx_ref[...]
    tb = idx.shape[0]
    lane = lax.broadcasted_iota(jnp.int32, idx.shape, 1)
    hot = [(lane == idx[:, k:k + 1]) for k in range(TOP_K)]
    tot = hot[0]
    for k in range(1, TOP_K):
        tot = jnp.logical_or(tot, hot[k])
    tot = jnp.where(tot, 1.0, 0.0)
    ti = lax.broadcasted_iota(jnp.int32, (tb, tb), 0)
    si = lax.broadcasted_iota(jnp.int32, (tb, tb), 1)
    before = jnp.where(ti > si, 1.0, 0.0).astype(BF16)
    prefix = jnp.dot(before, tot.astype(BF16), preferred_element_type=F32) + carry_ref[...]
    out = jnp.zeros(idx.shape, F32)
    for k in range(TOP_K):
        rk = jnp.sum(jnp.where(hot[k], prefix, 0.0), axis=-1, keepdims=True)
        out = jnp.where(lane == k, rk, out)
    rank_ref[...] = out.astype(jnp.int32)
    carry_ref[...] += jnp.sum(tot, axis=0, keepdims=True)
    cnt_ref[...] = carry_ref[...].astype(jnp.int32)


def route_rank(idx, *, tb):
    t = idx.shape[0]
    return pl.pallas_call(
        _rank_kernel,
        grid=(t // tb,),
        in_specs=[pl.BlockSpec((tb, LANES), lambda i: (i, 0))],
        out_specs=[pl.BlockSpec((tb, LANES), lambda i: (i, 0)), pl.BlockSpec((1, LANES), lambda i: (0, 0))],
        out_shape=[jax.ShapeDtypeStruct((t, LANES), jnp.int32), jax.ShapeDtypeStruct((1, LANES), jnp.int32)],
        scratch_shapes=[pltpu.VMEM((1, LANES), F32)],
        compiler_params=_cparams("arbitrary"),
        name="route_rank",
    )(idx)


def route_layout(top_idx, rank, counts, *, bm, n_rows):
    t = top_idx.shape[0]
    padded = (counts + bm - 1) // bm * bm
    pad_end = jnp.cumsum(padded)
    pad_start = pad_end - padded
    dest = (pad_start[top_idx] + rank).T.reshape(-1)
    row_tok = jnp.zeros((n_rows,), jnp.int32).at[dest].set(jnp.tile(jnp.arange(t, dtype=jnp.int32), TOP_K))
    nb = n_rows // bm
    blk_start = jnp.arange(nb, dtype=jnp.int32) * bm
    block_exp = jnp.minimum(jnp.searchsorted(pad_end, blk_start, side="right"), N_EXP - 1).astype(jnp.int32)
    n_valid = (pad_end[-1] // bm).astype(jnp.int32).reshape(1)
    return dest, row_tok, block_exp, n_valid


def grid_pos_embed(n_tok, d):
    rows = n_tok // GRID_W
    r = jnp.repeat(jnp.arange(rows, dtype=F32), GRID_W)
    col = jnp.tile(jnp.arange(GRID_W, dtype=F32), rows)
    quarter = d // 4
    omega = 1.0 / (10000.0 ** (jnp.arange(quarter, dtype=F32) / quarter))

    def emb(p):
        a = p[:, None] * omega[None, :]
        return jnp.concatenate([jnp.sin(a), jnp.cos(a)], axis=-1)

    return jnp.concatenate([emb(r), emb(col)], axis=-1)


def kernel(x_prompt, x_sample, state_hgrn, c, c_ctx, w_ada, b_ada, w_in, b_in, lb_logits, g_norm_a, conv_w, conv_b, conv_ln_g, conv_ln_b, w_br_a, w_br_b, w_br_c, b_br, w_gate, b_gate, w_o, b_o, ln1_g, ln1_b, w_router, b_router, w_gu, b_gu, w_down, b_down, ln2_g, ln2_b):
    depth = w_in.shape[0]
    bp, lp, d = x_prompt.shape
    bs, ls, _ = x_sample.shape
    tp, ts = bp * lp, bs * ls
    t = tp + ts
    alpha = (2 * depth) ** 0.25
    w_in_cols = w_in.shape[2]
    col_b = 3 * H_A * DK + 2 * W_A
    col_c = col_b + W_B

    tm, tn, tm_o, bm = TM, TN, TM_OUT, MOE_BM
    n_rows = (t * TOP_K // bm + N_EXP) * bm

    def row_of_block_for(rows_per_block):
        npb = tp // rows_per_block
        per = ls // rows_per_block
        return lambda i: jnp.where(i < npb, 0, 1 + (i - npb) // per)

    sm = jax.nn.softmax(lb_logits.astype(F32), axis=0)
    cs = jnp.cumsum(sm, axis=0)
    lower = cs - cs[0:1]

    cond = jnp.zeros((16, d), F32).at[0].set(c_ctx).at[1:1 + bs].set(c)
    mod = ada_mod(cond, w_ada, b_ada)

    xs = x_sample + grid_pos_embed(ls, d)[None]
    x = jnp.concatenate([x_prompt.reshape(tp, d), xs.reshape(ts, d)], axis=0)

    b_gu4 = b_gu[:, :, None, :]
    b_down4 = b_down[:, :, None, :]
    dl_p, cm = dft_tables(lp)
    dl_s, _ = dft_tables(ls)

    states = []
    for l in range(depth):
        mod3 = mod[l].reshape(16, 1, 6 * d)
        w_cat = jnp.concatenate([w_in[l], w_gate[l]], axis=1).astype(BF16)
        b_cat = jnp.concatenate([b_in[l], b_gate[l]])[None, :]
        u = ln_mod_matmul(x, mod3, row_of_block_for(tm), w_cat, b_cat, n_plain_cols=w_in_cols, tm=tm, tn=tn)

        gn = g_norm_a[l][None, :]
        ya_p, s_ctx = hgrn_mixer(u, lower[l], gn, None, row0=0, batch=bp, seq=lp, want_state=True)
        (ya_s,) = hgrn_mixer(u, lower[l], gn, state_hgrn[:, l], row0=tp, batch=bs, seq=ls, want_state=False)
        states.append(s_ctx)
        yb_p = fourier_mixer(u, dl_p, cm, row0=0, col0=col_b, batch=bp, seq=lp)
        yb_s = fourier_mixer(u, dl_s, cm, row0=tp, col0=col_b, batch=bs, seq=ls)
        cw = (conv_w[l], conv_b[l][None, :], conv_ln_g[l][None, :], conv_ln_b[l][None, :])
        yc_p = conv_mixer(u, *cw, row0=0, col0=col_c, batch=bp, seq=lp)
        yc_s = conv_mixer(u, *cw, row0=tp, col0=col_c, batch=bs, seq=ls)
        ya = jnp.concatenate([ya_p, ya_s], axis=0)
        yb = jnp.concatenate([yb_p, yb_s], axis=0)
        yc = jnp.concatenate([yc_p, yc_s], axis=0)

        merged = branch_merge(ya, yb, yc, u, w_br_a[l].astype(BF16), w_br_b[l].astype(BF16),
                              w_br_c[l].astype(BF16), b_br[l], gate_col0=w_in_cols, d=d, tm=tm, tn=tn)

        wr = jnp.zeros((d, LANES), BF16).at[:, :N_EXP].set(w_router[l].astype(BF16))
        br = jnp.full((1, LANES), -1e30, F32).at[0, :N_EXP].set(b_router[l])
        x1, h2, idx, prob = out_router(x, merged, mod3, row_of_block_for(tm_o), w_o[l].astype(BF16),
                                       b_o[l][None, :], ln1_g[l][None, :], ln1_b[l][None, :], wr, br,
                                       alpha=alpha, tm=tm_o)

        rank, counts = route_rank(idx, tb=RANK_ROWS)
        dest, row_tok, block_exp, n_valid = route_layout(idx[:, :TOP_K], rank[:, :TOP_K], counts[0, :N_EXP],
                                                         bm=bm, n_rows=n_rows)
        xsort = jnp.take(h2, row_tok, axis=0)
        act = moe_up(xsort, block_exp, n_valid, w_gu, b_gu4, l, bm=bm, tn=tn)
        yrow = moe_down(act, block_exp, n_valid, w_down, b_down4, l, bm=bm, tn=TN_DOWN)
        yg = jnp.take(yrow, dest, axis=0)
        x = combine_ln(x1, yg, prob, mod3, row_of_block_for(tm_o), ln2_g[l][None, :], ln2_b[l][None, :],
                       alpha=alpha, tm=tm_o)

    y_prompt = x[:tp].reshape(bp, lp, d)
    y_sample = x[tp:].reshape(bs, ls, d)
    return y_prompt, y_sample, jnp.stack(states, axis=1)
```

```python
import functools
import math

import jax
import jax.numpy as jnp
from jax import lax
from jax.experimental import pallas as pl
from jax.experimental.pallas import tpu as pltpu
from jax.experimental.pallas import tpu_sc as plsc

F32 = jnp.float32
BF16 = jnp.bfloat16

GRID_W = 64
H_A = 8
DK = 128
DV = 128
W_A = H_A * DV
W_B = 512
W_C = 512
CONV_K = 31
N_BRANCH = 3
N_EXP = 32
TOP_K = 4
SWIGLU_LIMIT = 7.0
SWIGLU_ALPHA = 1.702
LN_EPS = 1e-5
RMS_EPS = 1e-6

LANES = 128
SUBLANES = 8
VMEM_LIMIT = 56 * 1024 * 1024
HGRN_CHUNK = 16
HGRN_UNROLL = 8
CONV_PAD = 16
CONV_ROWS = 32
LOGF_FLOOR = -1e4
LOG2E = 1.4426950408889634
TM = 1024
TN = 512
TM_OUT = 256
MOE_BM = 512
TN_DOWN = 1024
RANK_ROWS = 512
SC_GATHER_ROWS = 32


def _cparams(*sem):
    return pltpu.CompilerParams(dimension_semantics=sem, vmem_limit_bytes=VMEM_LIMIT)


def _ln_rows(x):
    mu = jnp.mean(x, axis=-1, keepdims=True)
    xc = x - mu
    var = jnp.mean(xc * xc, axis=-1, keepdims=True)
    return xc * lax.rsqrt(var + LN_EPS)


def _sigmoid(x):
    return 1.0 / (1.0 + jnp.exp(-x))


def _bf16_bits(x):
    u = lax.bitcast_convert_type(x, jnp.uint32)
    r = u + jnp.uint32(0x7FFF) + ((u >> 16) & jnp.uint32(1))
    return r & jnp.uint32(0xFFFF0000)


def _pack_halves(x):
    n = x.shape[1] // 2
    return (_bf16_bits(x[:, :n]) >> 16) | _bf16_bits(x[:, n:])


def _unpack_halves(w):
    lo = lax.bitcast_convert_type(w << 16, F32)
    hi = lax.bitcast_convert_type(w & jnp.uint32(0xFFFF0000), F32)
    return lo, hi


def _ada_kernel(c_ref, w_ref, b_ref, o_ref):
    c = c_ref[...]
    a = (c * _sigmoid(c)).astype(BF16)
    o_ref[...] = jnp.dot(a, w_ref[...].astype(BF16), preferred_element_type=F32) + b_ref[...]


def ada_mod(cond, w_ada, b_ada, *, tn=1024):
    depth, d, n = w_ada.shape
    r = cond.shape[0]
    return pl.pallas_call(
        _ada_kernel,
        grid=(depth, n // tn),
        in_specs=[
            pl.BlockSpec((r, d), lambda l, j: (0, 0)),
            pl.BlockSpec((None, d, tn), lambda l, j: (l, 0, j)),
            pl.BlockSpec((None, 1, tn), lambda l, j: (l, 0, j)),
        ],
        out_specs=pl.BlockSpec((None, r, tn), lambda l, j: (l, 0, j)),
        out_shape=jax.ShapeDtypeStruct((depth, r, n), F32),
        compiler_params=_cparams("parallel", "parallel"),
        name="ada_mod",
    )(cond, w_ada, b_ada.reshape(depth, 1, n))


def _lnmm_kernel(x_ref, sh_ref, sc_ref, w_ref, b_ref, o_ref, h_ref, *, n_plain):
    j = pl.program_id(1)

    @pl.when(j == 0)
    def _():
        y = _ln_rows(x_ref[...])
        h_ref[...] = (y * (1.0 + sc_ref[...]) + sh_ref[...]).astype(BF16)

    acc = jnp.dot(h_ref[...], w_ref[...], preferred_element_type=F32) + b_ref[...]

    @pl.when(j < n_plain)
    def _():
        o_ref[...] = acc.astype(BF16)

    @pl.when(j >= n_plain)
    def _():
        o_ref[...] = _sigmoid(acc).astype(BF16)


def ln_mod_matmul(x, mod3, row_of_block, w, b, *, n_plain_cols, tm, tn):
    t, d = x.shape
    n = w.shape[1]
    return pl.pallas_call(
        functools.partial(_lnmm_kernel, n_plain=n_plain_cols // tn),
        grid=(t // tm, n // tn),
        in_specs=[
            pl.BlockSpec((tm, d), lambda i, j: (i, 0)),
            pl.BlockSpec((None, 1, d), lambda i, j: (row_of_block(i), 0, 0)),
            pl.BlockSpec((None, 1, d), lambda i, j: (row_of_block(i), 0, 1)),
            pl.BlockSpec((d, tn), lambda i, j: (0, j)),
            pl.BlockSpec((1, tn), lambda i, j: (0, j)),
        ],
        out_specs=pl.BlockSpec((tm, tn), lambda i, j: (i, j)),
        out_shape=jax.ShapeDtypeStruct((t, n), BF16),
        scratch_shapes=[pltpu.VMEM((tm, d), BF16)],
        compiler_params=_cparams("parallel", "arbitrary"),
        name="ln_mod_matmul",
    )(x, mod3, mod3, w, b)


def _chunk_scan(x_h, direction):
    half = SUBLANES
    rid = lax.broadcasted_iota(jnp.int32, x_h[0].shape, 0)
    out = []
    for x in x_h:
        step = 1
        while step < half:
            if direction == 0:
                x = x + jnp.where(rid >= step, pltpu.roll(x, step, axis=0), 0.0)
            else:
                x = x + jnp.where(rid < half - step, pltpu.roll(x, half - step, axis=0), 0.0)
            step *= 2
        out.append(x)
    if direction == 0:
        out[1] = out[1] + out[0][half - 1:half, :]
    else:
        out[0] = out[0] + out[1][0:1, :]
    return out


def _hgrn_chunk(q_ref, z_ref, v_ref, lb, oml, st_ref, o_ref, rows_ref, c0, *, direction):
    half = SUBLANES
    rows = pl.ds(c0, HGRN_CHUNK)
    q = q_ref[rows, :].astype(F32) * (DK ** -0.5)
    z = z_ref[rows, :].astype(F32)
    v = v_ref[rows, :]
    e = jnp.exp2(jnp.abs(z) * (-LOG2E))
    r = 1.0 / (1.0 + e)
    er = e * r
    pos = z >= 0
    lf2 = jnp.maximum(jnp.log2(lb + oml * jnp.where(pos, r, er)), LOGF_FLOOR)
    lk2 = jnp.log2(oml * jnp.where(pos, er, r))

    b_h = _chunk_scan([lf2[0:half, :], lf2[half:HGRN_CHUNK, :]], direction)
    b2 = jnp.concatenate(b_h, axis=0)
    b_end = b_h[1][half - 1:half, :] if direction == 0 else b_h[0][0:1, :]

    st = st_ref[direction]
    qe = (q * jnp.exp2(b2)).astype(BF16)
    o_inter = lax.dot_general(qe, st.astype(BF16), (((1,), (1,)), ((), ())), preferred_element_type=F32)

    rows_ref[0] = b2 - lk2
    rows_ref[1] = v.astype(F32)
    q_h = (q[0:half, :], q[half:HGRN_CHUNK, :])
    acc = [jnp.zeros((half, DV), F32), jnp.zeros((half, DV), F32)]
    rid = lax.broadcasted_iota(jnp.int32, (half, DK), 0)
    for s in range(HGRN_CHUNK):
        hs, rs = divmod(s, half)
        bs = jnp.broadcast_to(rows_ref[0, s:s + 1, :], (half, DK))
        vs = jnp.broadcast_to(rows_ref[1, s:s + 1, :], (half, DV))
        for ht in range(2):
            if (ht < hs) if direction == 0 else (ht > hs):
                continue
            d = b_h[ht] - bs
            if ht == hs:
                keep = (rid >= rs) if direction == 0 else (rid <= rs)
                d = jnp.where(keep, d, -1e30)
            w = jnp.sum(q_h[ht] * jnp.exp2(d), axis=-1, keepdims=True)
            acc[ht] = acc[ht] + w * vs
    o_ref[pl.ds(c0, half), :] = acc[0] + o_inter[0:half, :]
    o_ref[pl.ds(c0 + half, half), :] = acc[1] + o_inter[half:HGRN_CHUNK, :]

    kk = jnp.exp2(lk2 + (b_end - b2)).astype(BF16)
    upd = lax.dot_general(v, kk, (((0,), (0,)), ((), ())), preferred_element_type=F32)
    st_ref[direction] = st * jnp.exp2(b_end) + upd


def _hgrn_kernel(*refs, seq, unroll, has_s0, want_state):
    q_ref, zf_ref, zb_ref, v_ref, g_ref, lb_ref, gn_ref = refs[:7]
    pos = 7
    s0_ref = None
    if has_s0:
        s0_ref = refs[pos]
        pos += 1
    y_ref = refs[pos]
    pos += 1
    sfin_ref = None
    if want_state:
        sfin_ref = refs[pos]
        pos += 1
    of_ref, ob_ref, st_ref, rows_ref = refs[pos:pos + 4]

    for d in range(2):
        st_ref[d] = s0_ref[d].T if has_s0 else jnp.zeros((DV, DK), F32)

    n = seq // HGRN_CHUNK
    lb_f = lb_ref[0:1, :]
    lb_b = lb_ref[1:2, :]
    oml_f = 1.0 - lb_f
    oml_b = 1.0 - lb_b

    def body(i, carry):
        for j in range(unroll):
            cf = pl.multiple_of((i * unroll + j) * HGRN_CHUNK, HGRN_CHUNK)
            cb = pl.multiple_of((n - 1 - (i * unroll + j)) * HGRN_CHUNK, HGRN_CHUNK)
            _hgrn_chunk(q_ref, zf_ref, v_ref, lb_f, oml_f, st_ref, of_ref, rows_ref.at[2 * j], cf, direction=0)
            _hgrn_chunk(q_ref, zb_ref, v_ref, lb_b, oml_b, st_ref, ob_ref, rows_ref.at[2 * j + 1], cb, direction=1)
        return carry

    lax.fori_loop(0, n // unroll, body, 0)

    blk = min(seq, 256)
    for r0 in range(0, seq, blk):
        o = of_ref[r0:r0 + blk, :] + ob_ref[r0:r0 + blk, :]
        g = g_ref[r0:r0 + blk, :].astype(F32)
        o = o * lax.rsqrt(jnp.mean(o * o, axis=-1, keepdims=True) + RMS_EPS) * gn_ref[...]
        y_ref[r0:r0 + blk, :] = (o * (g * _sigmoid(g))).astype(BF16)
    if want_state:
        sfin_ref[0] = st_ref[0].T
        sfin_ref[1] = st_ref[1].T


def hgrn_mixer(u, lb, g_norm, s0, *, row0, batch, seq, want_state, unroll=HGRN_UNROLL):
    rb0 = row0 // seq
    has_s0 = s0 is not None
    unroll = math.gcd(seq // HGRN_CHUNK, unroll)

    def sec(k):
        return pl.BlockSpec((seq, LANES), lambda b, h: (rb0 + b, k * H_A + h))

    in_specs = [sec(0), sec(1), sec(2), sec(3), sec(4),
                pl.BlockSpec((2, DK), lambda b, h: (0, h)),
                pl.BlockSpec((1, DV), lambda b, h: (0, 0))]
    args = [u, u, u, u, u, lb, g_norm]
    state_spec = pl.BlockSpec((None, 2, None, DK, DV), lambda b, h: (b, 0, h, 0, 0))
    if has_s0:
        in_specs.append(state_spec)
        args.append(s0)
    out_specs = [pl.BlockSpec((seq, DV), lambda b, h: (b, h))]
    out_shape = [jax.ShapeDtypeStruct((batch * seq, W_A), BF16)]
    if want_state:
        out_specs.append(state_spec)
        out_shape.append(jax.ShapeDtypeStruct((batch, 2, H_A, DK, DV), F32))
    return pl.pallas_call(
        functools.partial(_hgrn_kernel, seq=seq, unroll=unroll, has_s0=has_s0, want_state=want_state),
        grid=(batch, H_A),
        in_specs=in_specs,
        out_specs=out_specs,
        out_shape=out_shape,
        scratch_shapes=[pltpu.VMEM((seq, DV), F32), pltpu.VMEM((seq, DV), F32), pltpu.VMEM((2, DV, DK), F32),
                        pltpu.VMEM((2 * unroll, 2, HGRN_CHUNK, DK), F32)],
        compiler_params=_cparams("parallel", "parallel"),
        name="hgrn_mixer",
    )(*args)


def _fourier_kernel(u_ref, cm_ref, dl_ref, o_ref, ab_ref, *, seq, rows, scale):
    @pl.when(pl.program_id(1) == 0)
    def _():
        for r0 in range(0, seq, rows):
            ab = jnp.dot(u_ref[r0:r0 + rows, :], cm_ref[...], preferred_element_type=F32)
            ab_ref[r0:r0 + rows, :] = ab[:, :W_B].astype(BF16)
            ab_ref[seq + r0:seq + r0 + rows, :] = ab[:, W_B:].astype(BF16)

    o_ref[...] = (jnp.dot(dl_ref[...], ab_ref[...], preferred_element_type=F32) * scale).astype(BF16)


def dft_tables(seq):
    def cs(n):
        i = lax.iota(jnp.int32, n)
        ang = ((i[:, None] * i[None, :]) % n).astype(F32) * (2.0 * math.pi / n)
        return jnp.cos(ang), jnp.sin(ang)
    cl, sl = cs(seq)
    cc, sc = cs(W_B)
    return jnp.concatenate([cl, -sl], axis=1).astype(BF16), jnp.concatenate([cc, sc], axis=1).astype(BF16)


def fourier_mixer(u, dl, cm, *, row0, col0, batch, seq, rows=256):
    rows = min(rows, seq)
    rb0 = row0 // seq
    cb = col0 // W_B
    nt = seq // rows
    return pl.pallas_call(
        functools.partial(_fourier_kernel, seq=seq, rows=rows, scale=1.0 / math.sqrt(seq * W_B)),
        grid=(batch, nt),
        in_specs=[
            pl.BlockSpec((seq, W_B), lambda b, i: (rb0 + b, cb)),
            pl.BlockSpec((W_B, 2 * W_B), lambda b, i: (0, 0)),
            pl.BlockSpec((rows, 2 * seq), lambda b, i: (i, 0)),
        ],
        out_specs=pl.BlockSpec((rows, W_B), lambda b, i: (b * nt + i, 0)),
        out_shape=jax.ShapeDtypeStruct((batch * seq, W_B), BF16),
        scratch_shapes=[pltpu.VMEM((2 * seq, W_B), BF16)],
        compiler_params=_cparams("parallel", "arbitrary"),
        name="fourier_mixer",
    )(u, cm, dl)


def _conv_kernel(a_ref, gt_ref, w_ref, b_ref, g_ref, be_ref, o_ref, z_ref, *, seq):
    zeros = jnp.zeros((CONV_PAD, W_C), F32)
    z_ref[0:CONV_PAD, :] = zeros
    z_ref[CONV_PAD + seq:2 * CONV_PAD + seq, :] = zeros
    blk = min(seq, 256)
    for r0 in range(0, seq, blk):
        a = a_ref[r0:r0 + blk, :].astype(F32)
        gt = gt_ref[r0:r0 + blk, :].astype(F32)
        z_ref[CONV_PAD + r0:CONV_PAD + r0 + blk, :] = a * _sigmoid(gt)

    off = CONV_PAD - CONV_K // 2

    def tile(i, carry):
        r0 = pl.multiple_of(i * CONV_ROWS, CONV_ROWS)
        win = z_ref[pl.ds(r0, CONV_ROWS + 2 * CONV_PAD), :]
        acc = jnp.zeros((CONV_ROWS, W_C), F32)
        for k in range(CONV_K):
            acc = acc + win[off + k:off + k + CONV_ROWS, :] * w_ref[k:k + 1, :]
        y = _ln_rows(acc + b_ref[...]) * g_ref[...] + be_ref[...]
        o_ref[pl.ds(r0, CONV_ROWS), :] = (y * _sigmoid(y)).astype(BF16)
        return carry

    lax.fori_loop(0, seq // CONV_ROWS, tile, 0)


def conv_mixer(u, conv_w, conv_b, ln_g, ln_b, *, row0, col0, batch, seq):
    rb0 = row0 // seq
    cb = col0 // W_C
    vec = pl.BlockSpec((1, W_C), lambda b: (0, 0))
    return pl.pallas_call(
        functools.partial(_conv_kernel, seq=seq),
        grid=(batch,),
        in_specs=[
            pl.BlockSpec((seq, W_C), lambda b: (rb0 + b, cb)),
            pl.BlockSpec((seq, W_C), lambda b: (rb0 + b, cb + 1)),
            pl.BlockSpec((CONV_K, W_C), lambda b: (0, 0)),
            vec, vec, vec,
        ],
        out_specs=pl.BlockSpec((seq, W_C), lambda b: (b, 0)),
        out_shape=jax.ShapeDtypeStruct((batch * seq, W_C), BF16),
        scratch_shapes=[pltpu.VMEM((seq + 2 * CONV_PAD, W_C), F32)],
        compiler_params=_cparams("parallel"),
        name="conv_mixer",
    )(u, u, conv_w, conv_b, ln_g, ln_b)


def _branch_kernel(ya_ref, yb_ref, yc_ref, g0_ref, g1_ref, g2_ref, wa_ref, wb_ref, wc_ref, b_ref, o_ref):
    def proj(y_ref, w_ref, k):
        return jnp.dot(y_ref[...], w_ref[...], preferred_element_type=F32) + b_ref[k:k + 1, :]
    m = g0_ref[...].astype(F32) * proj(ya_ref, wa_ref, 0)
    m = m + g1_ref[...].astype(F32) * proj(yb_ref, wb_ref, 1)
    m = m + g2_ref[...].astype(F32) * proj(yc_ref, wc_ref, 2)
    o_ref[...] = m.astype(BF16)


def branch_merge(ya, yb, yc, u, wa, wb, wc, b_br, *, gate_col0, d, tm, tn):
    t = ya.shape[0]
    gb = gate_col0 // tn
    nd = d // tn

    def gate(k):
        return pl.BlockSpec((tm, tn), lambda i, j: (i, gb + k * nd + j))

    return pl.pallas_call(
        _branch_kernel,
        grid=(t // tm, nd),
        in_specs=[
            pl.BlockSpec((tm, ya.shape[1]), lambda i, j: (i, 0)),
            pl.BlockSpec((tm, yb.shape[1]), lambda i, j: (i, 0)),
            pl.BlockSpec((tm, yc.shape[1]), lambda i, j: (i, 0)),
            gate(0), gate(1), gate(2),
            pl.BlockSpec((wa.shape[0], tn), lambda i, j: (0, j)),
            pl.BlockSpec((wb.shape[0], tn), lambda i, j: (0, j)),
            pl.BlockSpec((wc.shape[0], tn), lambda i, j: (0, j)),
            pl.BlockSpec((N_BRANCH, tn), lambda i, j: (0, j)),
        ],
        out_specs=pl.BlockSpec((tm, tn), lambda i, j: (i, j)),
        out_shape=jax.ShapeDtypeStruct((t, d), BF16),
        compiler_params=_cparams("parallel", "arbitrary"),
        name="branch_merge",
    )(ya, yb, yc, u, u, u, wa, wb, wc, b_br)


def _out_router_kernel(x_ref, m_ref, g1_ref, sh2_ref, sc2_ref, wo_ref, bo_ref, lg_ref, lb_ref,
                       wr_ref, br_ref, x1_ref, h2_ref, idx_ref, p_ref, *, alpha):
    mix = jnp.dot(m_ref[...], wo_ref[...], preferred_element_type=F32) + bo_ref[...]
    x1 = _ln_rows(alpha * x_ref[...] + g1_ref[...] * mix) * lg_ref[...] + lb_ref[...]
    x1_ref[...] = x1
    h2 = _ln_rows(x1) * (1.0 + sc2_ref[...]) + sh2_ref[...]
    h2_ref[...] = _pack_halves(h2)
    logits = jnp.dot(h2.astype(BF16), wr_ref[...], preferred_element_type=F32) + br_ref[...]
    lane = lax.broadcasted_iota(jnp.int32, logits.shape, 1)
    idx_out = jnp.zeros(logits.shape, jnp.int32)
    p_out = jnp.zeros(logits.shape, F32)
    top = None
    denom = None
    for k in range(TOP_K):
        m = jnp.max(logits, axis=-1, keepdims=True)
        i = jnp.min(jnp.where(logits == m, lane, LANES), axis=-1, keepdims=True)
        if k == 0:
            top = m
        ek = jnp.exp(m - top)
        denom = ek if k == 0 else denom + ek
        idx_out = jnp.where(lane == k, i, idx_out)
        p_out = jnp.where(lane == k, ek, p_out)
        logits = jnp.where(lane == i, -jnp.inf, logits)
    idx_ref[...] = idx_out
    p_ref[...] = p_out / denom


def out_router(x, merged, mod3, row_of_block, wo, bo, ln_g, ln_b, wr, br, *, alpha, tm):
    t, d = x.shape

    def modspec(k):
        return pl.BlockSpec((None, 1, d), lambda i: (row_of_block(i), 0, k))

    vec = pl.BlockSpec((1, d), lambda i: (0, 0))
    row_f = pl.BlockSpec((tm, d), lambda i: (i, 0))
    row_s = pl.BlockSpec((tm, LANES), lambda i: (i, 0))
    return pl.pallas_call(
        functools.partial(_out_router_kernel, alpha=alpha),
        grid=(t // tm,),
        in_specs=[row_f, row_f, modspec(2), modspec(3), modspec(4),
                  pl.BlockSpec((d, d), lambda i: (0, 0)), vec, vec, vec,
                  pl.BlockSpec((d, LANES), lambda i: (0, 0)),
                  pl.BlockSpec((1, LANES), lambda i: (0, 0))],
        out_specs=[row_f, pl.BlockSpec((tm, d // 2), lambda i: (i, 0)), row_s, row_s],
        out_shape=[jax.ShapeDtypeStruct((t, d), F32), jax.ShapeDtypeStruct((t, d // 2), jnp.uint32),
                   jax.ShapeDtypeStruct((t, LANES), jnp.int32), jax.ShapeDtypeStruct((t, LANES), F32)],
        compiler_params=_cparams("parallel"),
        name="out_router",
    )(x, merged, mod3, mod3, mod3, wo, bo, ln_g, ln_b, wr, br)


def _moe_up_kernel(be_ref, nv_ref, x_ref, wg_ref, wu_ref, bg_ref, bu_ref, o_ref, wgs_ref, wus_ref):
    b = pl.program_id(1)
    fresh = jnp.logical_or(b == 0, be_ref[b] != be_ref[jnp.maximum(b - 1, 0)])

    @pl.when(fresh)
    def _():
        wgs_ref[...] = wg_ref[...].astype(BF16)
        wus_ref[...] = wu_ref[...].astype(BF16)

    @pl.when(b < nv_ref[0])
    def _():
        lo, hi = _unpack_halves(x_ref[...])
        lo = lo.astype(BF16)
        hi = hi.astype(BF16)
        kh = lo.shape[1]

        def proj(w_ref, b_ref):
            return (jnp.dot(lo, w_ref[0:kh, :], preferred_element_type=F32)
                    + jnp.dot(hi, w_ref[kh:2 * kh, :], preferred_element_type=F32) + b_ref[...])

        gt = proj(wgs_ref, bg_ref)
        up = proj(wus_ref, bu_ref)
        gt = jnp.minimum(gt, SWIGLU_LIMIT)
        up = jnp.clip(up, -SWIGLU_LIMIT, SWIGLU_LIMIT)
        o_ref[...] = ((up + 1.0) * (gt * _sigmoid(SWIGLU_ALPHA * gt))).astype(BF16)

    @pl.when(b >= nv_ref[0])
    def _():
        o_ref[...] = jnp.zeros_like(o_ref)


def moe_up(xs, block_exp, n_valid, w_gu, b_gu, layer, *, bm, tn):
    r = xs.shape[0]
    d = w_gu.shape[2]
    f = w_gu.shape[3] // 2
    nf = f // tn
    grid_spec = pltpu.PrefetchScalarGridSpec(
        num_scalar_prefetch=2,
        grid=(nf, r // bm),
        in_specs=[
            pl.BlockSpec((bm, d // 2), lambda n, b, be, nv: (b, 0)),
            pl.BlockSpec((None, None, d, tn), lambda n, b, be, nv: (layer, be[b], 0, n)),
            pl.BlockSpec((None, None, d, tn), lambda n, b, be, nv: (layer, be[b], 0, nf + n)),
            pl.BlockSpec((None, None, 1, tn), lambda n, b, be, nv: (layer, be[b], 0, n)),
            pl.BlockSpec((None, None, 1, tn), lambda n, b, be, nv: (layer, be[b], 0, nf + n)),
        ],
        out_specs=pl.BlockSpec((bm, tn), lambda n, b, be, nv: (b, n)),
        scratch_shapes=[pltpu.VMEM((d, tn), BF16), pltpu.VMEM((d, tn), BF16)],
    )
    return pl.pallas_call(
        _moe_up_kernel,
        grid_spec=grid_spec,
        out_shape=jax.ShapeDtypeStruct((r, f), BF16),
        compiler_params=_cparams("arbitrary", "arbitrary"),
        name="moe_up",
    )(block_exp, n_valid, xs, w_gu, w_gu, b_gu, b_gu)


def _moe_down_kernel(be_ref, nv_ref, a_ref, w_ref, b_ref, o_ref, ws_ref):
    b = pl.program_id(1)
    fresh = jnp.logical_or(b == 0, be_ref[b] != be_ref[jnp.maximum(b - 1, 0)])

    @pl.when(fresh)
    def _():
        ws_ref[...] = w_ref[...].astype(BF16)

    @pl.when(b < nv_ref[0])
    def _():
        y = jnp.dot(a_ref[...], ws_ref[...], preferred_element_type=F32) + b_ref[...]
        o_ref[...] = _pack_halves(y)

    @pl.when(b >= nv_ref[0])
    def _():
        o_ref[...] = jnp.zeros_like(o_ref)


def moe_down(act, block_exp, n_valid, w_down, b_down, layer, *, bm, tn):
    r, f = act.shape
    d = w_down.shape[3]
    grid_spec = pltpu.PrefetchScalarGridSpec(
        num_scalar_prefetch=2,
        grid=(d // tn, r // bm),
        in_specs=[
            pl.BlockSpec((bm, f), lambda n, b, be, nv: (b, 0)),
            pl.BlockSpec((None, None, f, tn), lambda n, b, be, nv: (layer, be[b], 0, n)),
            pl.BlockSpec((None, None, 1, tn), lambda n, b, be, nv: (layer, be[b], 0, n)),
        ],
        out_specs=pl.BlockSpec((bm, tn // 2), lambda n, b, be, nv: (b, n)),
        scratch_shapes=[pltpu.VMEM((f, tn), BF16)],
    )
    return pl.pallas_call(
        _moe_down_kernel,
        grid_spec=grid_spec,
        out_shape=jax.ShapeDtypeStruct((r, d // 2), jnp.uint32),
        compiler_params=_cparams("arbitrary", "arbitrary"),
        name="moe_down",
    )(block_exp, n_valid, act, w_down, b_down)


def sc_gather(x, idx):
    n = idx.shape[0]
    d = x.shape[1]
    win = SC_GATHER_ROWS
    mesh = plsc.VectorSubcoreMesh(core_axis_name="core", subcore_axis_name="subcore")

    @pl.kernel(out_type=jax.ShapeDtypeStruct((n, d), x.dtype), mesh=mesh)
    def gather_kernel(x_hbm, i_hbm, o_hbm):
        def body(i_vmem, o_vmem):
            pltpu.sync_copy(x_hbm.at[i_vmem.at[0]], o_vmem)

        pltpu.emit_pipeline(
            body,
            grid=(n // win,),
            in_specs=[pl.BlockSpec((1, win), index_map=lambda i: (i, 0))],
            out_specs=[pl.BlockSpec((win, d), index_map=lambda i: (i, 0))],
            core_axis_name=("core", "subcore"),
            dimension_semantics=(pltpu.PARALLEL,),
        )(i_hbm, o_hbm)

    return gather_kernel(x, idx.reshape(n // win, win))


def _combine_kernel(x_ref, y0_ref, y1_ref, y2_ref, y3_ref, p_ref, g2_ref, lg_ref, lb_ref, o_ref, *, alpha, group):
    p = p_ref[...]
    ffn = None
    for k, y_ref in enumerate((y0_ref, y1_ref, y2_ref, y3_ref)):
        w = y_ref[...]
        cols = []
        for c0 in range(0, w.shape[1], group):
            cols.extend(_unpack_halves(w[:, c0:c0 + group]))
        term = p[:, k:k + 1] * jnp.concatenate(cols, axis=1)
        ffn = term if ffn is None else ffn + term
    o_ref[...] = _ln_rows(alpha * x_ref[...] + g2_ref[...] * ffn) * lg_ref[...] + lb_ref[...]


def combine_ln(x1, yg, prob, mod3, row_of_block, ln_g, ln_b, *, alpha, tm, group):
    t, d = x1.shape
    nb = t // tm
    vec = pl.BlockSpec((1, d), lambda i: (0, 0))

    def sel(k):
        return pl.BlockSpec((tm, d // 2), lambda i: (k * nb + i, 0))

    return pl.pallas_call(
        functools.partial(_combine_kernel, alpha=alpha, group=group),
        grid=(nb,),
        in_specs=[pl.BlockSpec((tm, d), lambda i: (i, 0)), sel(0), sel(1), sel(2), sel(3),
                  pl.BlockSpec((tm, LANES), lambda i: (i, 0)),
                  pl.BlockSpec((None, 1, d), lambda i: (row_of_block(i), 0, 5)),
                  vec, vec],
        out_specs=pl.BlockSpec((tm, d), lambda i: (i, 0)),
        out_shape=jax.ShapeDtypeStruct((t, d), F32),
        compiler_params=_cparams("parallel"),
        name="combine_ln",
    )(x1, yg, yg, yg, yg, prob, mod3, ln_g, ln_b)


def _rank_kernel(idx_ref, rank_ref, cnt_ref, carry_ref):
    @pl.when(pl.program_id(0) == 0)
    def _():
        carry_ref[...] = jnp.zeros_like(carry_ref)

    idx = idx_ref[...]
    tb = idx.shape[0]
    lane = lax.broadcasted_iota(jnp.int32, idx.shape, 1)
    hot = [(lane == idx[:, k:k + 1]) for k in range(TOP_K)]
    tot = hot[0]
    for k in range(1, TOP_K):
        tot = jnp.logical_or(tot, hot[k])
    tot = jnp.where(tot, 1.0, 0.0)
    ti = lax.broadcasted_iota(jnp.int32, (tb, tb), 0)
    si = lax.broadcasted_iota(jnp.int32, (tb, tb), 1)
    before = jnp.where(ti > si, 1.0, 0.0).astype(BF16)
    prefix = jnp.dot(before, tot.astype(BF16), preferred_element_type=F32) + carry_ref[...]
    out = jnp.zeros(idx.shape, F32)
    for k in range(TOP_K):
        rk = jnp.sum(jnp.where(hot[k], prefix, 0.0), axis=-1, keepdims=True)
        out = jnp.where(lane == k, rk, out)
    rank_ref[...] = out.astype(jnp.int32)
    carry_ref[...] += jnp.sum(tot, axis=0, keepdims=True)
    cnt_ref[...] = carry_ref[...].astype(jnp.int32)


def route_rank(idx, *, tb):
    t = idx.shape[0]
    return pl.pallas_call(
        _rank_kernel,
        grid=(t // tb,),
        in_specs=[pl.BlockSpec((tb, LANES), lambda i: (i, 0))],
        out_specs=[pl.BlockSpec((tb, LANES), lambda i: (i, 0)), pl.BlockSpec((1, LANES), lambda i: (0, 0))],
        out_shape=[jax.ShapeDtypeStruct((t, LANES), jnp.int32), jax.ShapeDtypeStruct((1, LANES), jnp.int32)],
        scratch_shapes=[pltpu.VMEM((1, LANES), F32)],
        compiler_params=_cparams("arbitrary"),
        name="route_rank",
    )(idx)


def route_layout(top_idx, rank, counts, *, bm, n_rows):
    t = top_idx.shape[0]
    padded = (counts + bm - 1) // bm * bm
    pad_end = jnp.cumsum(padded)
    pad_start = pad_end - padded
    dest = (pad_start[top_idx] + rank).T.reshape(-1)
    row_tok = jnp.zeros((n_rows,), jnp.int32).at[dest].set(jnp.tile(jnp.arange(t, dtype=jnp.int32), TOP_K))
    nb = n_rows // bm
    blk_start = jnp.arange(nb, dtype=jnp.int32) * bm
    block_exp = jnp.minimum(jnp.searchsorted(pad_end, blk_start, side="right"), N_EXP - 1).astype(jnp.int32)
    n_valid = (pad_end[-1] // bm).astype(jnp.int32).reshape(1)
    return dest, row_tok, block_exp, n_valid


def grid_pos_embed(n_tok, d):
    rows = n_tok // GRID_W
    r = jnp.repeat(jnp.arange(rows, dtype=F32), GRID_W)
    col = jnp.tile(jnp.arange(GRID_W, dtype=F32), rows)
    quarter = d // 4
    omega = 1.0 / (10000.0 ** (jnp.arange(quarter, dtype=F32) / quarter))

    def emb(p):
        a = p[:, None] * omega[None, :]
        return jnp.concatenate([jnp.sin(a), jnp.cos(a)], axis=-1)

    return jnp.concatenate([emb(r), emb(col)], axis=-1)


def kernel(x_prompt, x_sample, state_hgrn, c, c_ctx, w_ada, b_ada, w_in, b_in, lb_logits, g_norm_a, conv_w, conv_b, conv_ln_g, conv_ln_b, w_br_a, w_br_b, w_br_c, b_br, w_gate, b_gate, w_o, b_o, ln1_g, ln1_b, w_router, b_router, w_gu, b_gu, w_down, b_down, ln2_g, ln2_b):
    depth = w_in.shape[0]
    bp, lp, d = x_prompt.shape
    bs, ls, _ = x_sample.shape
    tp, ts = bp * lp, bs * ls
    t = tp + ts
    alpha = (2 * depth) ** 0.25
    w_in_cols = w_in.shape[2]
    col_b = 3 * H_A * DK + 2 * W_A
    col_c = col_b + W_B

    tm, tn, tm_o, bm = TM, TN, TM_OUT, MOE_BM
    n_rows = (t * TOP_K // bm + N_EXP) * bm

    def row_of_block_for(rows_per_block):
        npb = tp // rows_per_block
        per = ls // rows_per_block
        return lambda i: jnp.where(i < npb, 0, 1 + (i - npb) // per)

    sm = jax.nn.softmax(lb_logits.astype(F32), axis=0)
    cs = jnp.cumsum(sm, axis=0)
    lower = cs - cs[0:1]

    cond = jnp.zeros((16, d), F32).at[0].set(c_ctx).at[1:1 + bs].set(c)
    mod = ada_mod(cond, w_ada, b_ada)

    xs = x_sample + grid_pos_embed(ls, d)[None]
    x = jnp.concatenate([x_prompt.reshape(tp, d), xs.reshape(ts, d)], axis=0)

    b_gu4 = b_gu[:, :, None, :]
    b_down4 = b_down[:, :, None, :]
    dl_p, cm = dft_tables(lp)
    dl_s, _ = dft_tables(ls)

    states = []
    for l in range(depth):
        mod3 = mod[l].reshape(16, 1, 6 * d)
        w_cat = jnp.concatenate([w_in[l], w_gate[l]], axis=1).astype(BF16)
        b_cat = jnp.concatenate([b_in[l], b_gate[l]])[None, :]
        u = ln_mod_matmul(x, mod3, row_of_block_for(tm), w_cat, b_cat, n_plain_cols=w_in_cols, tm=tm, tn=tn)

        gn = g_norm_a[l][None, :]
        ya_p, s_ctx = hgrn_mixer(u, lower[l], gn, None, row0=0, batch=bp, seq=lp, want_state=True)
        (ya_s,) = hgrn_mixer(u, lower[l], gn, state_hgrn[:, l], row0=tp, batch=bs, seq=ls, want_state=False)
        states.append(s_ctx)
        yb_p = fourier_mixer(u, dl_p, cm, row0=0, col0=col_b, batch=bp, seq=lp)
        yb_s = fourier_mixer(u, dl_s, cm, row0=tp, col0=col_b, batch=bs, seq=ls)
        cw = (conv_w[l], conv_b[l][None, :], conv_ln_g[l][None, :], conv_ln_b[l][None, :])
        yc_p = conv_mixer(u, *cw, row0=0, col0=col_c, batch=bp, seq=lp)
        yc_s = conv_mixer(u, *cw, row0=tp, col0=col_c, batch=bs, seq=ls)
        ya = jnp.concatenate([ya_p, ya_s], axis=0)
        yb = jnp.concatenate([yb_p, yb_s], axis=0)
        yc = jnp.concatenate([yc_p, yc_s], axis=0)

        merged = branch_merge(ya, yb, yc, u, w_br_a[l].astype(BF16), w_br_b[l].astype(BF16),
                              w_br_c[l].astype(BF16), b_br[l], gate_col0=w_in_cols, d=d, tm=tm, tn=tn)

        wr = jnp.zeros((d, LANES), BF16).at[:, :N_EXP].set(w_router[l].astype(BF16))
        br = jnp.full((1, LANES), -1e30, F32).at[0, :N_EXP].set(b_router[l])
        x1, h2, idx, prob = out_router(x, merged, mod3, row_of_block_for(tm_o), w_o[l].astype(BF16),
                                       b_o[l][None, :], ln1_g[l][None, :], ln1_b[l][None, :], wr, br,
                                       alpha=alpha, tm=tm_o)

        rank, counts = route_rank(idx, tb=RANK_ROWS)
        dest, row_tok, block_exp, n_valid = route_layout(idx[:, :TOP_K], rank[:, :TOP_K], counts[0, :N_EXP],
                                                         bm=bm, n_rows=n_rows)
        xsort = sc_gather(h2, row_tok)
        act = moe_up(xsort, block_exp, n_valid, w_gu, b_gu4, l, bm=bm, tn=tn)
        yrow = moe_down(act, block_exp, n_valid, w_down, b_down4, l, bm=bm, tn=TN_DOWN)
        yg = sc_gather(yrow, dest)
        x = combine_ln(x1, yg, prob, mod3, row_of_block_for(tm_o), ln2_g[l][None, :], ln2_b[l][None, :],
                       alpha=alpha, tm=tm_o, group=TN_DOWN // 2)

    y_prompt = x[:tp].reshape(bp, lp, d)
    y_sample = x[tp:].reshape(bs, ls, d)
    return y_prompt, y_sample, jnp.stack(states, axis=1)
```

```python
import functools
import math

import jax
import jax.numpy as jnp
from jax import lax
from jax.experimental import pallas as pl
from jax.experimental.pallas import tpu as pltpu
from jax.experimental.pallas import tpu_sc as plsc

F32 = jnp.float32
BF16 = jnp.bfloat16

GRID_W = 64
H_A = 8
DK = 128
DV = 128
W_A = H_A * DV
W_B = 512
W_C = 512
CONV_K = 31
N_BRANCH = 3
N_EXP = 32
TOP_K = 4
SWIGLU_LIMIT = 7.0
SWIGLU_ALPHA = 1.702
LN_EPS = 1e-5
RMS_EPS = 1e-6

LANES = 128
SUBLANES = 8
VMEM_LIMIT = 56 * 1024 * 1024
HGRN_CHUNK = 16
HGRN_UNROLL = 8
CONV_PAD = 16
CONV_ROWS = 32
LOGF_FLOOR = -1e4
LOG2E = 1.4426950408889634
TM = 1024
TN = 512
TM_OUT = 256
MOE_BM = 512
TN_DOWN = 1024
RANK_ROWS = 512
SC_GATHER_ROWS = 32


def _cparams(*sem):
    return pltpu.CompilerParams(dimension_semantics=sem, vmem_limit_bytes=VMEM_LIMIT)


def _ln_rows(x):
    mu = jnp.mean(x, axis=-1, keepdims=True)
    xc = x - mu
    var = jnp.mean(xc * xc, axis=-1, keepdims=True)
    return xc * lax.rsqrt(var + LN_EPS)


def _sigmoid(x):
    return 1.0 / (1.0 + jnp.exp(-x))


def _bf16_bits(x):
    u = lax.bitcast_convert_type(x, jnp.uint32)
    r = u + jnp.uint32(0x7FFF) + ((u >> 16) & jnp.uint32(1))
    return r & jnp.uint32(0xFFFF0000)


def _pack_halves(x):
    n = x.shape[1] // 2
    return (_bf16_bits(x[:, :n]) >> 16) | _bf16_bits(x[:, n:])


def _unpack_halves(w):
    lo = lax.bitcast_convert_type(w << 16, F32)
    hi = lax.bitcast_convert_type(w & jnp.uint32(0xFFFF0000), F32)
    return lo, hi


def _ada_kernel(c_ref, w_ref, b_ref, o_ref):
    c = c_ref[...]
    a = (c * _sigmoid(c)).astype(BF16)
    o_ref[...] = jnp.dot(a, w_ref[...].astype(BF16), preferred_element_type=F32) + b_ref[...]


def ada_mod(cond, w_ada, b_ada, *, tn=1024):
    depth, d, n = w_ada.shape
    r = cond.shape[0]
    return pl.pallas_call(
        _ada_kernel,
        grid=(depth, n // tn),
        in_specs=[
            pl.BlockSpec((r, d), lambda l, j: (0, 0)),
            pl.BlockSpec((None, d, tn), lambda l, j: (l, 0, j)),
            pl.BlockSpec((None, 1, tn), lambda l, j: (l, 0, j)),
        ],
        out_specs=pl.BlockSpec((None, r, tn), lambda l, j: (l, 0, j)),
        out_shape=jax.ShapeDtypeStruct((depth, r, n), F32),
        compiler_params=_cparams("parallel", "parallel"),
        name="ada_mod",
    )(cond, w_ada, b_ada.reshape(depth, 1, n))


def _lnmm_kernel(x_ref, sh_ref, sc_ref, w_ref, b_ref, o_ref, h_ref, *, n_plain):
    j = pl.program_id(1)

    @pl.when(j == 0)
    def _():
        y = _ln_rows(x_ref[...])
        h_ref[...] = (y * (1.0 + sc_ref[...]) + sh_ref[...]).astype(BF16)

    acc = jnp.dot(h_ref[...], w_ref[...], preferred_element_type=F32) + b_ref[...]

    @pl.when(j < n_plain)
    def _():
        o_ref[...] = acc.astype(BF16)

    @pl.when(j >= n_plain)
    def _():
        o_ref[...] = _sigmoid(acc).astype(BF16)


def ln_mod_matmul(x, mod3, row_of_block, w, b, *, n_plain_cols, tm, tn):
    t, d = x.shape
    n = w.shape[1]
    return pl.pallas_call(
        functools.partial(_lnmm_kernel, n_plain=n_plain_cols // tn),
        grid=(t // tm, n // tn),
        in_specs=[
            pl.BlockSpec((tm, d), lambda i, j: (i, 0)),
            pl.BlockSpec((None, 1, d), lambda i, j: (row_of_block(i), 0, 0)),
            pl.BlockSpec((None, 1, d), lambda i, j: (row_of_block(i), 0, 1)),
            pl.BlockSpec((d, tn), lambda i, j: (0, j)),
            pl.BlockSpec((1, tn), lambda i, j: (0, j)),
        ],
        out_specs=pl.BlockSpec((tm, tn), lambda i, j: (i, j)),
        out_shape=jax.ShapeDtypeStruct((t, n), BF16),
        scratch_shapes=[pltpu.VMEM((tm, d), BF16)],
        compiler_params=_cparams("parallel", "arbitrary"),
        name="ln_mod_matmul",
    )(x, mod3, mod3, w, b)


def _chunk_scan(x_h, direction):
    half = SUBLANES
    rid = lax.broadcasted_iota(jnp.int32, x_h[0].shape, 0)
    out = []
    for x in x_h:
        step = 1
        while step < half:
            if direction == 0:
                x = x + jnp.where(rid >= step, pltpu.roll(x, step, axis=0), 0.0)
            else:
                x = x + jnp.where(rid < half - step, pltpu.roll(x, half - step, axis=0), 0.0)
            step *= 2
        out.append(x)
    if direction == 0:
        out[1] = out[1] + out[0][half - 1:half, :]
    else:
        out[0] = out[0] + out[1][0:1, :]
    return out


def _hgrn_chunk(q_ref, z_ref, v_ref, lb, oml, st_ref, o_ref, rows_ref, c0, *, direction):
    half = SUBLANES
    rows = pl.ds(c0, HGRN_CHUNK)
    q = q_ref[rows, :].astype(F32) * (DK ** -0.5)
    z = z_ref[rows, :].astype(F32)
    v = v_ref[rows, :]
    e = jnp.exp2(jnp.abs(z) * (-LOG2E))
    r = 1.0 / (1.0 + e)
    er = e * r
    pos = z >= 0
    lf2 = jnp.maximum(jnp.log2(lb + oml * jnp.where(pos, r, er)), LOGF_FLOOR)
    lk2 = jnp.log2(oml * jnp.where(pos, er, r))

    b_h = _chunk_scan([lf2[0:half, :], lf2[half:HGRN_CHUNK, :]], direction)
    b2 = jnp.concatenate(b_h, axis=0)
    b_end = b_h[1][half - 1:half, :] if direction == 0 else b_h[0][0:1, :]

    st = st_ref[direction]
    qe = (q * jnp.exp2(b2)).astype(BF16)
    o_inter = lax.dot_general(qe, st.astype(BF16), (((1,), (1,)), ((), ())), preferred_element_type=F32)

    rows_ref[0] = b2 - lk2
    rows_ref[1] = v.astype(F32)
    q_h = (q[0:half, :], q[half:HGRN_CHUNK, :])
    acc = [jnp.zeros((half, DV), F32), jnp.zeros((half, DV), F32)]
    rid = lax.broadcasted_iota(jnp.int32, (half, DK), 0)
    for s in range(HGRN_CHUNK):
        hs, rs = divmod(s, half)
        bs = jnp.broadcast_to(rows_ref[0, s:s + 1, :], (half, DK))
        vs = jnp.broadcast_to(rows_ref[1, s:s + 1, :], (half, DV))
        for ht in range(2):
            if (ht < hs) if direction == 0 else (ht > hs):
                continue
            d = b_h[ht] - bs
            if ht == hs:
                keep = (rid >= rs) if direction == 0 else (rid <= rs)
                d = jnp.where(keep, d, -1e30)
            w = jnp.sum(q_h[ht] * jnp.exp2(d), axis=-1, keepdims=True)
            acc[ht] = acc[ht] + w * vs
    o_ref[pl.ds(c0, half), :] = acc[0] + o_inter[0:half, :]
    o_ref[pl.ds(c0 + half, half), :] = acc[1] + o_inter[half:HGRN_CHUNK, :]

    kk = jnp.exp2(lk2 + (b_end - b2)).astype(BF16)
    upd = lax.dot_general(v, kk, (((0,), (0,)), ((), ())), preferred_element_type=F32)
    st_ref[direction] = st * jnp.exp2(b_end) + upd


def _hgrn_kernel(*refs, seq, unroll, has_s0, want_state):
    q_ref, zf_ref, zb_ref, v_ref, g_ref, lb_ref, gn_ref = refs[:7]
    pos = 7
    s0_ref = None
    if has_s0:
        s0_ref = refs[pos]
        pos += 1
    y_ref = refs[pos]
    pos += 1
    sfin_ref = None
    if want_state:
        sfin_ref = refs[pos]
        pos += 1
    of_ref, ob_ref, st_ref, rows_ref = refs[pos:pos + 4]

    for d in range(2):
        st_ref[d] = s0_ref[d].T if has_s0 else jnp.zeros((DV, DK), F32)

    n = seq // HGRN_CHUNK
    lb_f = lb_ref[0:1, :]
    lb_b = lb_ref[1:2, :]
    oml_f = 1.0 - lb_f
    oml_b = 1.0 - lb_b

    def body(i, carry):
        for j in range(unroll):
            cf = pl.multiple_of((i * unroll + j) * HGRN_CHUNK, HGRN_CHUNK)
            cb = pl.multiple_of((n - 1 - (i * unroll + j)) * HGRN_CHUNK, HGRN_CHUNK)
            _hgrn_chunk(q_ref, zf_ref, v_ref, lb_f, oml_f, st_ref, of_ref, rows_ref.at[2 * j], cf, direction=0)
            _hgrn_chunk(q_ref, zb_ref, v_ref, lb_b, oml_b, st_ref, ob_ref, rows_ref.at[2 * j + 1], cb, direction=1)
        return carry

    lax.fori_loop(0, n // unroll, body, 0)

    blk = min(seq, 256)
    for r0 in range(0, seq, blk):
        o = of_ref[r0:r0 + blk, :] + ob_ref[r0:r0 + blk, :]
        g = g_ref[r0:r0 + blk, :].astype(F32)
        o = o * lax.rsqrt(jnp.mean(o * o, axis=-1, keepdims=True) + RMS_EPS) * gn_ref[...]
        y_ref[r0:r0 + blk, :] = (o * (g * _sigmoid(g))).astype(BF16)
    if want_state:
        sfin_ref[0] = st_ref[0].T
        sfin_ref[1] = st_ref[1].T


def hgrn_mixer(u, lb, g_norm, s0, *, row0, batch, seq, want_state, unroll=HGRN_UNROLL):
    rb0 = row0 // seq
    has_s0 = s0 is not None
    unroll = math.gcd(seq // HGRN_CHUNK, unroll)

    def sec(k):
        return pl.BlockSpec((seq, LANES), lambda b, h: (rb0 + b, k * H_A + h))

    in_specs = [sec(0), sec(1), sec(2), sec(3), sec(4),
                pl.BlockSpec((2, DK), lambda b, h: (0, h)),
                pl.BlockSpec((1, DV), lambda b, h: (0, 0))]
    args = [u, u, u, u, u, lb, g_norm]
    state_spec = pl.BlockSpec((None, 2, None, DK, DV), lambda b, h: (b, 0, h, 0, 0))
    if has_s0:
        in_specs.append(state_spec)
        args.append(s0)
    out_specs = [pl.BlockSpec((seq, DV), lambda b, h: (b, h))]
    out_shape = [jax.ShapeDtypeStruct((batch * seq, W_A), BF16)]
    if want_state:
        out_specs.append(state_spec)
        out_shape.append(jax.ShapeDtypeStruct((batch, 2, H_A, DK, DV), F32))
    return pl.pallas_call(
        functools.partial(_hgrn_kernel, seq=seq, unroll=unroll, has_s0=has_s0, want_state=want_state),
        grid=(batch, H_A),
        in_specs=in_specs,
        out_specs=out_specs,
        out_shape=out_shape,
        scratch_shapes=[pltpu.VMEM((seq, DV), F32), pltpu.VMEM((seq, DV), F32), pltpu.VMEM((2, DV, DK), F32),
                        pltpu.VMEM((2 * unroll, 2, HGRN_CHUNK, DK), F32)],
        compiler_params=_cparams("parallel", "parallel"),
        name="hgrn_mixer",
    )(*args)


def _fourier_kernel(u_ref, cm_ref, dl_ref, o_ref, ab_ref, *, seq, rows, scale):
    @pl.when(pl.program_id(1) == 0)
    def _():
        for r0 in range(0, seq, rows):
            ab = jnp.dot(u_ref[r0:r0 + rows, :], cm_ref[...], preferred_element_type=F32)
            ab_ref[r0:r0 + rows, :] = ab[:, :W_B].astype(BF16)
            ab_ref[seq + r0:seq + r0 + rows, :] = ab[:, W_B:].astype(BF16)

    o_ref[...] = (jnp.dot(dl_ref[...], ab_ref[...], preferred_element_type=F32) * scale).astype(BF16)


def dft_tables(seq):
    def cs(n):
        i = lax.iota(jnp.int32, n)
        ang = ((i[:, None] * i[None, :]) % n).astype(F32) * (2.0 * math.pi / n)
        return jnp.cos(ang), jnp.sin(ang)
    cl, sl = cs(seq)
    cc, sc = cs(W_B)
    return jnp.concatenate([cl, -sl], axis=1).astype(BF16), jnp.concatenate([cc, sc], axis=1).astype(BF16)


def fourier_mixer(u, dl, cm, *, row0, col0, batch, seq, rows=256):
    rows = min(rows, seq)
    rb0 = row0 // seq
    cb = col0 // W_B
    nt = seq // rows
    return pl.pallas_call(
        functools.partial(_fourier_kernel, seq=seq, rows=rows, scale=1.0 / math.sqrt(seq * W_B)),
        grid=(batch, nt),
        in_specs=[
            pl.BlockSpec((seq, W_B), lambda b, i: (rb0 + b, cb)),
            pl.BlockSpec((W_B, 2 * W_B), lambda b, i: (0, 0)),
            pl.BlockSpec((rows, 2 * seq), lambda b, i: (i, 0)),
        ],
        out_specs=pl.BlockSpec((rows, W_B), lambda b, i: (b * nt + i, 0)),
        out_shape=jax.ShapeDtypeStruct((batch * seq, W_B), BF16),
        scratch_shapes=[pltpu.VMEM((2 * seq, W_B), BF16)],
        compiler_params=_cparams("parallel", "arbitrary"),
        name="fourier_mixer",
    )(u, cm, dl)


def _conv_kernel(a_ref, gt_ref, w_ref, b_ref, g_ref, be_ref, o_ref, z_ref, sh_ref, *, seq):
    zeros = jnp.zeros((CONV_PAD, W_C), F32)
    z_ref[0:CONV_PAD, :] = zeros
    z_ref[CONV_PAD + seq:2 * CONV_PAD + seq, :] = zeros
    blk = min(seq, 256)
    for r0 in range(0, seq, blk):
        a = a_ref[r0:r0 + blk, :].astype(F32)
        gt = gt_ref[r0:r0 + blk, :].astype(F32)
        z_ref[CONV_PAD + r0:CONV_PAD + r0 + blk, :] = a * _sigmoid(gt)

    off = CONV_PAD - CONV_K // 2

    def tile(i, carry):
        r0 = pl.multiple_of(i * CONV_ROWS, CONV_ROWS)
        win = z_ref[pl.ds(r0, CONV_ROWS + 2 * CONV_PAD), :]
        span = CONV_ROWS + 2 * CONV_PAD - SUBLANES
        for r in range(SUBLANES):
            sh_ref[r] = win[r:r + span, :]
        acc = jnp.zeros((CONV_ROWS, W_C), F32)
        for k in range(CONV_K):
            m, r = divmod(off + k, SUBLANES)
            acc = acc + sh_ref[r, m * SUBLANES:m * SUBLANES + CONV_ROWS, :] * w_ref[k:k + 1, :]
        y = _ln_rows(acc + b_ref[...]) * g_ref[...] + be_ref[...]
        o_ref[pl.ds(r0, CONV_ROWS), :] = (y * _sigmoid(y)).astype(BF16)
        return carry

    lax.fori_loop(0, seq // CONV_ROWS, tile, 0)


def conv_mixer(u, conv_w, conv_b, ln_g, ln_b, *, row0, col0, batch, seq):
    rb0 = row0 // seq
    cb = col0 // W_C
    vec = pl.BlockSpec((1, W_C), lambda b: (0, 0))
    return pl.pallas_call(
        functools.partial(_conv_kernel, seq=seq),
        grid=(batch,),
        in_specs=[
            pl.BlockSpec((seq, W_C), lambda b: (rb0 + b, cb)),
            pl.BlockSpec((seq, W_C), lambda b: (rb0 + b, cb + 1)),
            pl.BlockSpec((CONV_K, W_C), lambda b: (0, 0)),
            vec, vec, vec,
        ],
        out_specs=pl.BlockSpec((seq, W_C), lambda b: (b, 0)),
        out_shape=jax.ShapeDtypeStruct((batch * seq, W_C), BF16),
        scratch_shapes=[pltpu.VMEM((seq + 2 * CONV_PAD, W_C), F32),
                        pltpu.VMEM((SUBLANES, CONV_ROWS + 2 * CONV_PAD - SUBLANES, W_C), F32)],
        compiler_params=_cparams("parallel"),
        name="conv_mixer",
    )(u, u, conv_w, conv_b, ln_g, ln_b)


def _branch_kernel(yap_ref, ybp_ref, ycp_ref, yas_ref, ybs_ref, ycs_ref, g0_ref, g1_ref, g2_ref,
                   wa_ref, wb_ref, wc_ref, b_ref, o_ref, *, n_first):
    def merge(ya_ref, yb_ref, yc_ref):
        def proj(y_ref, w_ref, k):
            return jnp.dot(y_ref[...], w_ref[...], preferred_element_type=F32) + b_ref[k:k + 1, :]
        m = g0_ref[...].astype(F32) * proj(ya_ref, wa_ref, 0)
        m = m + g1_ref[...].astype(F32) * proj(yb_ref, wb_ref, 1)
        m = m + g2_ref[...].astype(F32) * proj(yc_ref, wc_ref, 2)
        o_ref[...] = m.astype(BF16)

    first = pl.program_id(0) < n_first

    @pl.when(first)
    def _():
        merge(yap_ref, ybp_ref, ycp_ref)

    @pl.when(jnp.logical_not(first))
    def _():
        merge(yas_ref, ybs_ref, ycs_ref)


def branch_merge(y_first, y_second, u, wa, wb, wc, b_br, *, gate_col0, d, tm, tn):
    t1 = y_first[0].shape[0]
    t2 = y_second[0].shape[0]
    n1 = t1 // tm
    n2 = t2 // tm
    gb = gate_col0 // tn
    nd = d // tn

    def gate(k):
        return pl.BlockSpec((tm, tn), lambda i, j: (i, gb + k * nd + j))

    def first(y):
        return pl.BlockSpec((tm, y.shape[1]), lambda i, j: (jnp.minimum(i, n1 - 1), 0))

    def second(y):
        return pl.BlockSpec((tm, y.shape[1]), lambda i, j: (jnp.maximum(i - n1, 0), 0))

    return pl.pallas_call(
        functools.partial(_branch_kernel, n_first=n1),
        grid=(n1 + n2, nd),
        in_specs=[
            first(y_first[0]), first(y_first[1]), first(y_first[2]),
            second(y_second[0]), second(y_second[1]), second(y_second[2]),
            gate(0), gate(1), gate(2),
            pl.BlockSpec((wa.shape[0], tn), lambda i, j: (0, j)),
            pl.BlockSpec((wb.shape[0], tn), lambda i, j: (0, j)),
            pl.BlockSpec((wc.shape[0], tn), lambda i, j: (0, j)),
            pl.BlockSpec((N_BRANCH, tn), lambda i, j: (0, j)),
        ],
        out_specs=pl.BlockSpec((tm, tn), lambda i, j: (i, j)),
        out_shape=jax.ShapeDtypeStruct((t1 + t2, d), BF16),
        compiler_params=_cparams("parallel", "arbitrary"),
        name="branch_merge",
    )(*y_first, *y_second, u, u, u, wa, wb, wc, b_br)


def _out_router_kernel(x_ref, m_ref, g1_ref, sh2_ref, sc2_ref, wo_ref, bo_ref, lg_ref, lb_ref,
                       wr_ref, br_ref, x1_ref, h2_ref, idx_ref, p_ref, *, alpha):
    mix = jnp.dot(m_ref[...], wo_ref[...], preferred_element_type=F32) + bo_ref[...]
    x1 = _ln_rows(alpha * x_ref[...] + g1_ref[...] * mix) * lg_ref[...] + lb_ref[...]
    x1_ref[...] = x1
    h2 = _ln_rows(x1) * (1.0 + sc2_ref[...]) + sh2_ref[...]
    h2_ref[...] = _pack_halves(h2)
    logits = jnp.dot(h2.astype(BF16), wr_ref[...], preferred_element_type=F32) + br_ref[...]
    lane = lax.broadcasted_iota(jnp.int32, logits.shape, 1)
    idx_out = jnp.zeros(logits.shape, jnp.int32)
    p_out = jnp.zeros(logits.shape, F32)
    top = None
    denom = None
    for k in range(TOP_K):
        m = jnp.max(logits, axis=-1, keepdims=True)
        i = jnp.min(jnp.where(logits == m, lane, LANES), axis=-1, keepdims=True)
        if k == 0:
            top = m
        ek = jnp.exp(m - top)
        denom = ek if k == 0 else denom + ek
        idx_out = jnp.where(lane == k, i, idx_out)
        p_out = jnp.where(lane == k, ek, p_out)
        logits = jnp.where(lane == i, -jnp.inf, logits)
    idx_ref[...] = idx_out
    p_ref[...] = p_out / denom


def out_router(x, merged, mod3, row_of_block, wo, bo, ln_g, ln_b, wr, br, *, alpha, tm):
    t, d = x.shape

    def modspec(k):
        return pl.BlockSpec((None, 1, d), lambda i: (row_of_block(i), 0, k))

    vec = pl.BlockSpec((1, d), lambda i: (0, 0))
    row_f = pl.BlockSpec((tm, d), lambda i: (i, 0))
    row_s = pl.BlockSpec((tm, LANES), lambda i: (i, 0))
    return pl.pallas_call(
        functools.partial(_out_router_kernel, alpha=alpha),
        grid=(t // tm,),
        in_specs=[row_f, row_f, modspec(2), modspec(3), modspec(4),
                  pl.BlockSpec((d, d), lambda i: (0, 0)), vec, vec, vec,
                  pl.BlockSpec((d, LANES), lambda i: (0, 0)),
                  pl.BlockSpec((1, LANES), lambda i: (0, 0))],
        out_specs=[row_f, pl.BlockSpec((tm, d // 2), lambda i: (i, 0)), row_s, row_s],
        out_shape=[jax.ShapeDtypeStruct((t, d), F32), jax.ShapeDtypeStruct((t, d // 2), jnp.uint32),
                   jax.ShapeDtypeStruct((t, LANES), jnp.int32), jax.ShapeDtypeStruct((t, LANES), F32)],
        compiler_params=_cparams("parallel"),
        name="out_router",
    )(x, merged, mod3, mod3, mod3, wo, bo, ln_g, ln_b, wr, br)


def _moe_up_kernel(be_ref, nv_ref, x_ref, wg_ref, wu_ref, bg_ref, bu_ref, o_ref, wgs_ref, wus_ref):
    b = pl.program_id(1)
    fresh = jnp.logical_or(b == 0, be_ref[b] != be_ref[jnp.maximum(b - 1, 0)])

    @pl.when(fresh)
    def _():
        wgs_ref[...] = wg_ref[...].astype(BF16)
        wus_ref[...] = wu_ref[...].astype(BF16)

    @pl.when(b < nv_ref[0])
    def _():
        lo, hi = _unpack_halves(x_ref[...])
        lo = lo.astype(BF16)
        hi = hi.astype(BF16)
        kh = lo.shape[1]

        def proj(w_ref, b_ref):
            return (jnp.dot(lo, w_ref[0:kh, :], preferred_element_type=F32)
                    + jnp.dot(hi, w_ref[kh:2 * kh, :], preferred_element_type=F32) + b_ref[...])

        gt = proj(wgs_ref, bg_ref)
        up = proj(wus_ref, bu_ref)
        gt = jnp.minimum(gt, SWIGLU_LIMIT)
        up = jnp.clip(up, -SWIGLU_LIMIT, SWIGLU_LIMIT)
        o_ref[...] = ((up + 1.0) * (gt * _sigmoid(SWIGLU_ALPHA * gt))).astype(BF16)

    @pl.when(b >= nv_ref[0])
    def _():
        o_ref[...] = jnp.zeros_like(o_ref)


def moe_up(xs, block_exp, n_valid, w_gu, b_gu, layer, *, bm, tn):
    r = xs.shape[0]
    d = w_gu.shape[2]
    f = w_gu.shape[3] // 2
    nf = f // tn
    grid_spec = pltpu.PrefetchScalarGridSpec(
        num_scalar_prefetch=2,
        grid=(nf, r // bm),
        in_specs=[
            pl.BlockSpec((bm, d // 2), lambda n, b, be, nv: (b, 0)),
            pl.BlockSpec((None, None, d, tn), lambda n, b, be, nv: (layer, be[b], 0, n)),
            pl.BlockSpec((None, None, d, tn), lambda n, b, be, nv: (layer, be[b], 0, nf + n)),
            pl.BlockSpec((None, None, 1, tn), lambda n, b, be, nv: (layer, be[b], 0, n)),
            pl.BlockSpec((None, None, 1, tn), lambda n, b, be, nv: (layer, be[b], 0, nf + n)),
        ],
        out_specs=pl.BlockSpec((bm, tn), lambda n, b, be, nv: (b, n)),
        scratch_shapes=[pltpu.VMEM((d, tn), BF16), pltpu.VMEM((d, tn), BF16)],
    )
    return pl.pallas_call(
        _moe_up_kernel,
        grid_spec=grid_spec,
        out_shape=jax.ShapeDtypeStruct((r, f), BF16),
        compiler_params=_cparams("arbitrary", "arbitrary"),
        name="moe_up",
    )(block_exp, n_valid, xs, w_gu, w_gu, b_gu, b_gu)


def _moe_down_kernel(be_ref, nv_ref, a_ref, w_ref, b_ref, o_ref, ws_ref):
    b = pl.program_id(1)
    fresh = jnp.logical_or(b == 0, be_ref[b] != be_ref[jnp.maximum(b - 1, 0)])

    @pl.when(fresh)
    def _():
        ws_ref[...] = w_ref[...].astype(BF16)

    @pl.when(b < nv_ref[0])
    def _():
        y = jnp.dot(a_ref[...], ws_ref[...], preferred_element_type=F32) + b_ref[...]
        o_ref[...] = _pack_halves(y)

    @pl.when(b >= nv_ref[0])
    def _():
        o_ref[...] = jnp.zeros_like(o_ref)


def moe_down(act, block_exp, n_valid, w_down, b_down, layer, *, bm, tn):
    r, f = act.shape
    d = w_down.shape[3]
    grid_spec = pltpu.PrefetchScalarGridSpec(
        num_scalar_prefetch=2,
        grid=(d // tn, r // bm),
        in_specs=[
            pl.BlockSpec((bm, f), lambda n, b, be, nv: (b, 0)),
            pl.BlockSpec((None, None, f, tn), lambda n, b, be, nv: (layer, be[b], 0, n)),
            pl.BlockSpec((None, None, 1, tn), lambda n, b, be, nv: (layer, be[b], 0, n)),
        ],
        out_specs=pl.BlockSpec((bm, tn // 2), lambda n, b, be, nv: (b, n)),
        scratch_shapes=[pltpu.VMEM((f, tn), BF16)],
    )
    return pl.pallas_call(
        _moe_down_kernel,
        grid_spec=grid_spec,
        out_shape=jax.ShapeDtypeStruct((r, d // 2), jnp.uint32),
        compiler_params=_cparams("arbitrary", "arbitrary"),
        name="moe_down",
    )(block_exp, n_valid, act, w_down, b_down)


def sc_gather(x, idx):
    n = idx.shape[0]
    d = x.shape[1]
    win = SC_GATHER_ROWS
    mesh = plsc.VectorSubcoreMesh(core_axis_name="core", subcore_axis_name="subcore")

    @pl.kernel(out_type=jax.ShapeDtypeStruct((n, d), x.dtype), mesh=mesh)
    def gather_kernel(x_hbm, i_hbm, o_hbm):
        def body(i_vmem, o_vmem):
            pltpu.sync_copy(x_hbm.at[i_vmem.at[0]], o_vmem)

        pltpu.emit_pipeline(
            body,
            grid=(n // win,),
            in_specs=[pl.BlockSpec((1, win), index_map=lambda i: (i, 0))],
            out_specs=[pl.BlockSpec((win, d), index_map=lambda i: (i, 0))],
            core_axis_name=("core", "subcore"),
            dimension_semantics=(pltpu.PARALLEL,),
        )(i_hbm, o_hbm)

    return gather_kernel(x, idx.reshape(n // win, win))


def _combine_kernel(x_ref, y0_ref, y1_ref, y2_ref, y3_ref, p_ref, g2_ref, lg_ref, lb_ref, o_ref, *, alpha, group):
    p = p_ref[...]
    ffn = None
    for k, y_ref in enumerate((y0_ref, y1_ref, y2_ref, y3_ref)):
        w = y_ref[...]
        cols = []
        for c0 in range(0, w.shape[1], group):
            cols.extend(_unpack_halves(w[:, c0:c0 + group]))
        term = p[:, k:k + 1] * jnp.concatenate(cols, axis=1)
        ffn = term if ffn is None else ffn + term
    o_ref[...] = _ln_rows(alpha * x_ref[...] + g2_ref[...] * ffn) * lg_ref[...] + lb_ref[...]


def combine_ln(x1, yg, prob, mod3, row_of_block, ln_g, ln_b, *, alpha, tm, group):
    t, d = x1.shape
    nb = t // tm
    vec = pl.BlockSpec((1, d), lambda i: (0, 0))

    def sel(k):
        return pl.BlockSpec((tm, d // 2), lambda i: (k * nb + i, 0))

    return pl.pallas_call(
        functools.partial(_combine_kernel, alpha=alpha, group=group),
        grid=(nb,),
        in_specs=[pl.BlockSpec((tm, d), lambda i: (i, 0)), sel(0), sel(1), sel(2), sel(3),
                  pl.BlockSpec((tm, LANES), lambda i: (i, 0)),
                  pl.BlockSpec((None, 1, d), lambda i: (row_of_block(i), 0, 5)),
                  vec, vec],
        out_specs=pl.BlockSpec((tm, d), lambda i: (i, 0)),
        out_shape=jax.ShapeDtypeStruct((t, d), F32),
        compiler_params=_cparams("parallel"),
        name="combine_ln",
    )(x1, yg, yg, yg, yg, prob, mod3, ln_g, ln_b)


def _rank_kernel(idx_ref, rank_ref, cnt_ref, carry_ref):
    @pl.when(pl.program_id(0) == 0)
    def _():
        carry_ref[...] = jnp.zeros_like(carry_ref)

    idx = idx_ref[...]
    tb = idx.shape[0]
    lane = lax.broadcasted_iota(jnp.int32, idx.shape, 1)
    hot = [(lane == idx[:, k:k + 1]) for k in range(TOP_K)]
    tot = hot[0]
    for k in range(1, TOP_K):
        tot = jnp.logical_or(tot, hot[k])
    tot = jnp.where(tot, 1.0, 0.0)
    ti = lax.broadcasted_iota(jnp.int32, (tb, tb), 0)
    si = lax.broadcasted_iota(jnp.int32, (tb, tb), 1)
    before = jnp.where(ti > si, 1.0, 0.0).astype(BF16)
    prefix = jnp.dot(before, tot.astype(BF16), preferred_element_type=F32) + carry_ref[...]
    out = jnp.zeros(idx.shape, F32)
    for k in range(TOP_K):
        rk = jnp.sum(jnp.where(hot[k], prefix, 0.0), axis=-1, keepdims=True)
        out = jnp.where(lane == k, rk, out)
    rank_ref[...] = out.astype(jnp.int32)
    carry_ref[...] += jnp.sum(tot, axis=0, keepdims=True)
    cnt_ref[...] = carry_ref[...].astype(jnp.int32)


def route_rank(idx, *, tb):
    t = idx.shape[0]
    return pl.pallas_call(
        _rank_kernel,
        grid=(t // tb,),
        in_specs=[pl.BlockSpec((tb, LANES), lambda i: (i, 0))],
        out_specs=[pl.BlockSpec((tb, LANES), lambda i: (i, 0)), pl.BlockSpec((1, LANES), lambda i: (0, 0))],
        out_shape=[jax.ShapeDtypeStruct((t, LANES), jnp.int32), jax.ShapeDtypeStruct((1, LANES), jnp.int32)],
        scratch_shapes=[pltpu.VMEM((1, LANES), F32)],
        compiler_params=_cparams("arbitrary"),
        name="route_rank",
    )(idx)


def route_layout(top_idx, rank, counts, *, bm, n_rows):
    t = top_idx.shape[0]
    padded = (counts + bm - 1) // bm * bm
    pad_end = jnp.cumsum(padded)
    pad_start = pad_end - padded
    dest = (pad_start[top_idx] + rank).T.reshape(-1)
    row_tok = (jnp.arange(n_rows, dtype=jnp.int32) % t).at[dest].set(jnp.tile(jnp.arange(t, dtype=jnp.int32), TOP_K))
    nb = n_rows // bm
    blk_start = jnp.arange(nb, dtype=jnp.int32) * bm
    block_exp = jnp.sum((blk_start[:, None] >= pad_end[None, :]).astype(jnp.int32), axis=1)
    block_exp = jnp.minimum(block_exp, N_EXP - 1)
    n_valid = (pad_end[-1] // bm).astype(jnp.int32).reshape(1)
    return dest, row_tok, block_exp, n_valid


def grid_pos_embed(n_tok, d):
    rows = n_tok // GRID_W
    r = jnp.repeat(jnp.arange(rows, dtype=F32), GRID_W)
    col = jnp.tile(jnp.arange(GRID_W, dtype=F32), rows)
    quarter = d // 4
    omega = 1.0 / (10000.0 ** (jnp.arange(quarter, dtype=F32) / quarter))

    def emb(p):
        a = p[:, None] * omega[None, :]
        return jnp.concatenate([jnp.sin(a), jnp.cos(a)], axis=-1)

    return jnp.concatenate([emb(r), emb(col)], axis=-1)


def kernel(x_prompt, x_sample, state_hgrn, c, c_ctx, w_ada, b_ada, w_in, b_in, lb_logits, g_norm_a, conv_w, conv_b, conv_ln_g, conv_ln_b, w_br_a, w_br_b, w_br_c, b_br, w_gate, b_gate, w_o, b_o, ln1_g, ln1_b, w_router, b_router, w_gu, b_gu, w_down, b_down, ln2_g, ln2_b):
    depth = w_in.shape[0]
    bp, lp, d = x_prompt.shape
    bs, ls, _ = x_sample.shape
    tp, ts = bp * lp, bs * ls
    t = tp + ts
    alpha = (2 * depth) ** 0.25
    w_in_cols = w_in.shape[2]
    col_b = 3 * H_A * DK + 2 * W_A
    col_c = col_b + W_B

    tm, tn, tm_o, bm = TM, TN, TM_OUT, MOE_BM
    n_rows = (t * TOP_K // bm + N_EXP) * bm

    def row_of_block_for(rows_per_block):
        npb = tp // rows_per_block
        per = ls // rows_per_block
        return lambda i: jnp.where(i < npb, 0, 1 + (i - npb) // per)

    sm = jax.nn.softmax(lb_logits.astype(F32), axis=0)
    cs = jnp.cumsum(sm, axis=0)
    lower = cs - cs[0:1]

    cond = jnp.zeros((16, d), F32).at[0].set(c_ctx).at[1:1 + bs].set(c)
    mod = ada_mod(cond, w_ada, b_ada)

    xs = x_sample + grid_pos_embed(ls, d)[None]
    x = jnp.concatenate([x_prompt.reshape(tp, d), xs.reshape(ts, d)], axis=0)

    b_gu4 = b_gu[:, :, None, :]
    b_down4 = b_down[:, :, None, :]
    dl_p, cm = dft_tables(lp)
    dl_s, _ = dft_tables(ls)

    states = []
    for l in range(depth):
        mod3 = mod[l].reshape(16, 1, 6 * d)
        w_cat = jnp.concatenate([w_in[l], w_gate[l]], axis=1).astype(BF16)
        b_cat = jnp.concatenate([b_in[l], b_gate[l]])[None, :]
        u = ln_mod_matmul(x, mod3, row_of_block_for(tm), w_cat, b_cat, n_plain_cols=w_in_cols, tm=tm, tn=tn)

        gn = g_norm_a[l][None, :]
        ya_p, s_ctx = hgrn_mixer(u, lower[l], gn, None, row0=0, batch=bp, seq=lp, want_state=True)
        (ya_s,) = hgrn_mixer(u, lower[l], gn, state_hgrn[:, l], row0=tp, batch=bs, seq=ls, want_state=False)
        states.append(s_ctx)
        yb_p = fourier_mixer(u, dl_p, cm, row0=0, col0=col_b, batch=bp, seq=lp)
        yb_s = fourier_mixer(u, dl_s, cm, row0=tp, col0=col_b, batch=bs, seq=ls)
        cw = (conv_w[l], conv_b[l][None, :], conv_ln_g[l][None, :], conv_ln_b[l][None, :])
        yc_p = conv_mixer(u, *cw, row0=0, col0=col_c, batch=bp, seq=lp)
        yc_s = conv_mixer(u, *cw, row0=tp, col0=col_c, batch=bs, seq=ls)
        merged = branch_merge((ya_p, yb_p, yc_p), (ya_s, yb_s, yc_s), u, w_br_a[l].astype(BF16),
                              w_br_b[l].astype(BF16), w_br_c[l].astype(BF16), b_br[l],
                              gate_col0=w_in_cols, d=d, tm=tm, tn=tn)

        wr = jnp.zeros((d, LANES), BF16).at[:, :N_EXP].set(w_router[l].astype(BF16))
        br = jnp.full((1, LANES), -1e30, F32).at[0, :N_EXP].set(b_router[l])
        x1, h2, idx, prob = out_router(x, merged, mod3, row_of_block_for(tm_o), w_o[l].astype(BF16),
                                       b_o[l][None, :], ln1_g[l][None, :], ln1_b[l][None, :], wr, br,
                                       alpha=alpha, tm=tm_o)

        rank, counts = route_rank(idx, tb=RANK_ROWS)
        dest, row_tok, block_exp, n_valid = route_layout(idx[:, :TOP_K], rank[:, :TOP_K], counts[0, :N_EXP],
                                                         bm=bm, n_rows=n_rows)
        xsort = sc_gather(h2, row_tok)
        act = moe_up(xsort, block_exp, n_valid, w_gu, b_gu4, l, bm=bm, tn=tn)
        yrow = moe_down(act, block_exp, n_valid, w_down, b_down4, l, bm=bm, tn=TN_DOWN)
        yg = sc_gather(yrow, dest)
        x = combine_ln(x1, yg, prob, mod3, row_of_block_for(tm_o), ln2_g[l][None, :], ln2_b[l][None, :],
                       alpha=alpha, tm=tm_o, group=TN_DOWN // 2)

    y_prompt = x[:tp].reshape(bp, lp, d)
    y_sample = x[tp:].reshape(bs, ls, d)
    return y_prompt, y_sample, jnp.stack(states, axis=1)
```

```python
import functools
import math

import jax
import jax.numpy as jnp
from jax import lax
from jax.experimental import pallas as pl
from jax.experimental.pallas import tpu as pltpu
from jax.experimental.pallas import tpu_sc as plsc

F32 = jnp.float32
BF16 = jnp.bfloat16

GRID_W = 64
H_A = 8
DK = 128
DV = 128
W_A = H_A * DV
W_B = 512
W_C = 512
CONV_K = 31
N_BRANCH = 3
N_EXP = 32
TOP_K = 4
SWIGLU_LIMIT = 7.0
SWIGLU_ALPHA = 1.702
LN_EPS = 1e-5
RMS_EPS = 1e-6

LANES = 128
SUBLANES = 8
VMEM_LIMIT = 56 * 1024 * 1024
HGRN_CHUNK = 16
HGRN_UNROLL = 16
CONV_PAD = 16
CONV_ROWS = 32
LOGF_FLOOR = -1e4
LOG2E = 1.4426950408889634
TM = 1024
TM_IN = 2048
TN = 512
TM_OUT = 256
MOE_BM = 512
TN_UP = 1024
TN_DOWN = 2048
RANK_ROWS = 512
SC_GATHER_ROWS = 32


def _cparams(*sem):
    return pltpu.CompilerParams(dimension_semantics=sem, vmem_limit_bytes=VMEM_LIMIT)


def _ln_rows(x):
    mu = jnp.mean(x, axis=-1, keepdims=True)
    xc = x - mu
    var = jnp.mean(xc * xc, axis=-1, keepdims=True)
    return xc * lax.rsqrt(var + LN_EPS)


def _sigmoid(x):
    return 1.0 / (1.0 + jnp.exp(-x))


def _bf16_bits(x):
    u = lax.bitcast_convert_type(x, jnp.uint32)
    r = u + jnp.uint32(0x7FFF) + ((u >> 16) & jnp.uint32(1))
    return r & jnp.uint32(0xFFFF0000)


def _pack_halves(x):
    n = x.shape[1] // 2
    return (_bf16_bits(x[:, :n]) >> 16) | _bf16_bits(x[:, n:])


def _unpack_halves(w):
    lo = lax.bitcast_convert_type(w << 16, F32)
    hi = lax.bitcast_convert_type(w & jnp.uint32(0xFFFF0000), F32)
    return lo, hi


def _ada_kernel(c_ref, w_ref, b_ref, o_ref):
    c = c_ref[...]
    a = (c * _sigmoid(c)).astype(BF16)
    o_ref[...] = jnp.dot(a, w_ref[...].astype(BF16), preferred_element_type=F32) + b_ref[...]


def ada_mod(cond, w_ada, b_ada, *, tn=1024):
    depth, d, n = w_ada.shape
    r = cond.shape[0]
    return pl.pallas_call(
        _ada_kernel,
        grid=(depth, n // tn),
        in_specs=[
            pl.BlockSpec((r, d), lambda l, j: (0, 0)),
            pl.BlockSpec((None, d, tn), lambda l, j: (l, 0, j)),
            pl.BlockSpec((None, 1, tn), lambda l, j: (l, 0, j)),
        ],
        out_specs=pl.BlockSpec((None, r, tn), lambda l, j: (l, 0, j)),
        out_shape=jax.ShapeDtypeStruct((depth, r, n), F32),
        compiler_params=_cparams("parallel", "parallel"),
        name="ada_mod",
    )(cond, w_ada, b_ada.reshape(depth, 1, n))


def _lnmm_kernel(x_ref, sh_ref, sc_ref, w_ref, b_ref, o_ref, h_ref, *, n_plain):
    j = pl.program_id(1)

    @pl.when(j == 0)
    def _():
        y = _ln_rows(x_ref[...])
        h_ref[...] = (y * (1.0 + sc_ref[...]) + sh_ref[...]).astype(BF16)

    acc = jnp.dot(h_ref[...], w_ref[...], preferred_element_type=F32) + b_ref[...]

    @pl.when(j < n_plain)
    def _():
        o_ref[...] = acc.astype(BF16)

    @pl.when(j >= n_plain)
    def _():
        o_ref[...] = _sigmoid(acc).astype(BF16)


def ln_mod_matmul(x, mod3, row_of_block, w, b, *, n_plain_cols, tm, tn):
    t, d = x.shape
    n = w.shape[1]
    return pl.pallas_call(
        functools.partial(_lnmm_kernel, n_plain=n_plain_cols // tn),
        grid=(t // tm, n // tn),
        in_specs=[
            pl.BlockSpec((tm, d), lambda i, j: (i, 0), pipeline_mode=pl.Buffered(1)),
            pl.BlockSpec((None, 1, d), lambda i, j: (row_of_block(i), 0, 0)),
            pl.BlockSpec((None, 1, d), lambda i, j: (row_of_block(i), 0, 1)),
            pl.BlockSpec((d, tn), lambda i, j: (0, j)),
            pl.BlockSpec((1, tn), lambda i, j: (0, j)),
        ],
        out_specs=pl.BlockSpec((tm, tn), lambda i, j: (i, j)),
        out_shape=jax.ShapeDtypeStruct((t, n), BF16),
        scratch_shapes=[pltpu.VMEM((tm, d), BF16)],
        compiler_params=_cparams("parallel", "arbitrary"),
        name="ln_mod_matmul",
    )(x, mod3, mod3, w, b)


def _chunk_scan(x_h, direction):
    half = SUBLANES
    rid = lax.broadcasted_iota(jnp.int32, x_h[0].shape, 0)
    out = []
    for x in x_h:
        step = 1
        while step < half:
            if direction == 0:
                x = x + jnp.where(rid >= step, pltpu.roll(x, step, axis=0), 0.0)
            else:
                x = x + jnp.where(rid < half - step, pltpu.roll(x, half - step, axis=0), 0.0)
            step *= 2
        out.append(x)
    if direction == 0:
        out[1] = out[1] + out[0][half - 1:half, :]
    else:
        out[0] = out[0] + out[1][0:1, :]
    return out


def _hgrn_chunk(q_ref, z_ref, v_ref, lb, oml, st_ref, o_ref, rows_ref, c0, *, direction):
    half = SUBLANES
    rows = pl.ds(c0, HGRN_CHUNK)
    q = q_ref[rows, :].astype(F32) * (DK ** -0.5)
    z = z_ref[rows, :].astype(F32)
    v = v_ref[rows, :]
    e = jnp.exp2(jnp.abs(z) * (-LOG2E))
    r = 1.0 / (1.0 + e)
    er = e * r
    pos = z >= 0
    lf2 = jnp.maximum(jnp.log2(lb + oml * jnp.where(pos, r, er)), LOGF_FLOOR)
    lk2 = jnp.log2(oml * jnp.where(pos, er, r))

    b_h = _chunk_scan([lf2[0:half, :], lf2[half:HGRN_CHUNK, :]], direction)
    b2 = jnp.concatenate(b_h, axis=0)
    b_end = b_h[1][half - 1:half, :] if direction == 0 else b_h[0][0:1, :]

    st = st_ref[direction]
    qe = (q * jnp.exp2(b2)).astype(BF16)
    o_inter = lax.dot_general(qe, st.astype(BF16), (((1,), (1,)), ((), ())), preferred_element_type=F32)

    rows_ref[0] = b2 - lk2
    rows_ref[1] = v.astype(F32)
    q_h = (q[0:half, :], q[half:HGRN_CHUNK, :])
    acc = [jnp.zeros((half, DV), F32), jnp.zeros((half, DV), F32)]
    rid = lax.broadcasted_iota(jnp.int32, (half, DK), 0)
    for s in range(HGRN_CHUNK):
        hs, rs = divmod(s, half)
        bs = jnp.broadcast_to(rows_ref[0, s:s + 1, :], (half, DK))
        vs = jnp.broadcast_to(rows_ref[1, s:s + 1, :], (half, DV))
        for ht in range(2):
            if (ht < hs) if direction == 0 else (ht > hs):
                continue
            d = b_h[ht] - bs
            if ht == hs:
                keep = (rid >= rs) if direction == 0 else (rid <= rs)
                d = jnp.where(keep, d, -1e30)
            w = jnp.sum(q_h[ht] * jnp.exp2(d), axis=-1, keepdims=True)
            acc[ht] = acc[ht] + w * vs
    o_ref[pl.ds(c0, half), :] = acc[0] + o_inter[0:half, :]
    o_ref[pl.ds(c0 + half, half), :] = acc[1] + o_inter[half:HGRN_CHUNK, :]

    kk = jnp.exp2(lk2 + (b_end - b2)).astype(BF16)
    upd = lax.dot_general(v, kk, (((0,), (0,)), ((), ())), preferred_element_type=F32)
    st_ref[direction] = st * jnp.exp2(b_end) + upd


def _hgrn_kernel(*refs, seq, unroll, has_s0, want_state):
    q_ref, zf_ref, zb_ref, v_ref, g_ref, lb_ref, gn_ref = refs[:7]
    pos = 7
    s0_ref = None
    if has_s0:
        s0_ref = refs[pos]
        pos += 1
    y_ref = refs[pos]
    pos += 1
    sfin_ref = None
    if want_state:
        sfin_ref = refs[pos]
        pos += 1
    of_ref, ob_ref, st_ref, rows_ref = refs[pos:pos + 4]

    for d in range(2):
        st_ref[d] = s0_ref[d].T if has_s0 else jnp.zeros((DV, DK), F32)

    n = seq // HGRN_CHUNK
    lb_f = lb_ref[0:1, :]
    lb_b = lb_ref[1:2, :]
    oml_f = 1.0 - lb_f
    oml_b = 1.0 - lb_b

    def body(i, carry):
        for j in range(unroll):
            cf = pl.multiple_of((i * unroll + j) * HGRN_CHUNK, HGRN_CHUNK)
            cb = pl.multiple_of((n - 1 - (i * unroll + j)) * HGRN_CHUNK, HGRN_CHUNK)
            _hgrn_chunk(q_ref, zf_ref, v_ref, lb_f, oml_f, st_ref, of_ref, rows_ref.at[2 * j], cf, direction=0)
            _hgrn_chunk(q_ref, zb_ref, v_ref, lb_b, oml_b, st_ref, ob_ref, rows_ref.at[2 * j + 1], cb, direction=1)
        return carry

    lax.fori_loop(0, n // unroll, body, 0)

    blk = min(seq, 256)
    for r0 in range(0, seq, blk):
        o = of_ref[r0:r0 + blk, :] + ob_ref[r0:r0 + blk, :]
        g = g_ref[r0:r0 + blk, :].astype(F32)
        o = o * lax.rsqrt(jnp.mean(o * o, axis=-1, keepdims=True) + RMS_EPS) * gn_ref[...]
        y_ref[r0:r0 + blk, :] = (o * (g * _sigmoid(g))).astype(BF16)
    if want_state:
        sfin_ref[0] = st_ref[0].T
        sfin_ref[1] = st_ref[1].T


def hgrn_mixer(u, lb, g_norm, s0, *, row0, batch, seq, want_state, unroll=HGRN_UNROLL):
    rb0 = row0 // seq
    has_s0 = s0 is not None
    unroll = math.gcd(seq // HGRN_CHUNK, unroll)

    def sec(k):
        return pl.BlockSpec((seq, LANES), lambda b, h: (rb0 + b, k * H_A + h))

    in_specs = [sec(0), sec(1), sec(2), sec(3), sec(4),
                pl.BlockSpec((2, DK), lambda b, h: (0, h)),
                pl.BlockSpec((1, DV), lambda b, h: (0, 0))]
    args = [u, u, u, u, u, lb, g_norm]
    state_spec = pl.BlockSpec((None, 2, None, DK, DV), lambda b, h: (b, 0, h, 0, 0))
    if has_s0:
        in_specs.append(state_spec)
        args.append(s0)
    out_specs = [pl.BlockSpec((seq, DV), lambda b, h: (b, h))]
    out_shape = [jax.ShapeDtypeStruct((batch * seq, W_A), BF16)]
    if want_state:
        out_specs.append(state_spec)
        out_shape.append(jax.ShapeDtypeStruct((batch, 2, H_A, DK, DV), F32))
    return pl.pallas_call(
        functools.partial(_hgrn_kernel, seq=seq, unroll=unroll, has_s0=has_s0, want_state=want_state),
        grid=(batch, H_A),
        in_specs=in_specs,
        out_specs=out_specs,
        out_shape=out_shape,
        scratch_shapes=[pltpu.VMEM((seq, DV), F32), pltpu.VMEM((seq, DV), F32), pltpu.VMEM((2, DV, DK), F32),
                        pltpu.VMEM((2 * unroll, 2, HGRN_CHUNK, DK), F32)],
        compiler_params=_cparams("parallel", "parallel"),
        name="hgrn_mixer",
    )(*args)


def _fourier_kernel(u_ref, cm_ref, dl_ref, o_ref, ab_ref, *, seq, rows, scale):
    @pl.when(pl.program_id(1) == 0)
    def _():
        for r0 in range(0, seq, rows):
            ab = jnp.dot(u_ref[r0:r0 + rows, :], cm_ref[...], preferred_element_type=F32)
            ab_ref[r0:r0 + rows, :] = ab[:, :W_B].astype(BF16)
            ab_ref[seq + r0:seq + r0 + rows, :] = ab[:, W_B:].astype(BF16)

    o_ref[...] = (jnp.dot(dl_ref[...], ab_ref[...], preferred_element_type=F32) * scale).astype(BF16)


def dft_tables(seq):
    def cs(n):
        i = lax.iota(jnp.int32, n)
        ang = ((i[:, None] * i[None, :]) % n).astype(F32) * (2.0 * math.pi / n)
        return jnp.cos(ang), jnp.sin(ang)
    cl, sl = cs(seq)
    cc, sc = cs(W_B)
    return jnp.concatenate([cl, -sl], axis=1).astype(BF16), jnp.concatenate([cc, sc], axis=1).astype(BF16)


def fourier_mixer(u, dl, cm, *, row0, col0, batch, seq, rows=256):
    rows = min(rows, seq)
    rb0 = row0 // seq
    cb = col0 // W_B
    nt = seq // rows
    return pl.pallas_call(
        functools.partial(_fourier_kernel, seq=seq, rows=rows, scale=1.0 / math.sqrt(seq * W_B)),
        grid=(batch, nt),
        in_specs=[
            pl.BlockSpec((seq, W_B), lambda b, i: (rb0 + b, cb)),
            pl.BlockSpec((W_B, 2 * W_B), lambda b, i: (0, 0)),
            pl.BlockSpec((rows, 2 * seq), lambda b, i: (i, 0)),
        ],
        out_specs=pl.BlockSpec((rows, W_B), lambda b, i: (b * nt + i, 0)),
        out_shape=jax.ShapeDtypeStruct((batch * seq, W_B), BF16),
        scratch_shapes=[pltpu.VMEM((2 * seq, W_B), BF16)],
        compiler_params=_cparams("parallel", "arbitrary"),
        name="fourier_mixer",
    )(u, cm, dl)


def _conv_kernel(a_ref, gt_ref, w_ref, b_ref, g_ref, be_ref, o_ref, z_ref, sh_ref, *, seq):
    zeros = jnp.zeros((CONV_PAD, W_C), F32)
    z_ref[0:CONV_PAD, :] = zeros
    z_ref[CONV_PAD + seq:2 * CONV_PAD + seq, :] = zeros
    blk = min(seq, 256)
    for r0 in range(0, seq, blk):
        a = a_ref[r0:r0 + blk, :].astype(F32)
        gt = gt_ref[r0:r0 + blk, :].astype(F32)
        z_ref[CONV_PAD + r0:CONV_PAD + r0 + blk, :] = a * _sigmoid(gt)

    off = CONV_PAD - CONV_K // 2

    def tile(i, carry):
        r0 = pl.multiple_of(i * CONV_ROWS, CONV_ROWS)
        win = z_ref[pl.ds(r0, CONV_ROWS + 2 * CONV_PAD), :]
        span = CONV_ROWS + 2 * CONV_PAD - SUBLANES
        for r in range(SUBLANES):
            sh_ref[r] = win[r:r + span, :]
        acc = jnp.zeros((CONV_ROWS, W_C), F32)
        for k in range(CONV_K):
            m, r = divmod(off + k, SUBLANES)
            acc = acc + sh_ref[r, m * SUBLANES:m * SUBLANES + CONV_ROWS, :] * w_ref[k:k + 1, :]
        y = _ln_rows(acc + b_ref[...]) * g_ref[...] + be_ref[...]
        o_ref[pl.ds(r0, CONV_ROWS), :] = (y * _sigmoid(y)).astype(BF16)
        return carry

    lax.fori_loop(0, seq // CONV_ROWS, tile, 0)


def conv_mixer(u, conv_w, conv_b, ln_g, ln_b, *, row0, col0, batch, seq):
    rb0 = row0 // seq
    cb = col0 // W_C
    vec = pl.BlockSpec((1, W_C), lambda b: (0, 0))
    return pl.pallas_call(
        functools.partial(_conv_kernel, seq=seq),
        grid=(batch,),
        in_specs=[
            pl.BlockSpec((seq, W_C), lambda b: (rb0 + b, cb)),
            pl.BlockSpec((seq, W_C), lambda b: (rb0 + b, cb + 1)),
            pl.BlockSpec((CONV_K, W_C), lambda b: (0, 0)),
            vec, vec, vec,
        ],
        out_specs=pl.BlockSpec((seq, W_C), lambda b: (b, 0)),
        out_shape=jax.ShapeDtypeStruct((batch * seq, W_C), BF16),
        scratch_shapes=[pltpu.VMEM((seq + 2 * CONV_PAD, W_C), F32),
                        pltpu.VMEM((SUBLANES, CONV_ROWS + 2 * CONV_PAD - SUBLANES, W_C), F32)],
        compiler_params=_cparams("parallel"),
        name="conv_mixer",
    )(u, u, conv_w, conv_b, ln_g, ln_b)


def _branch_kernel(yap_ref, ybp_ref, ycp_ref, yas_ref, ybs_ref, ycs_ref, g0_ref, g1_ref, g2_ref,
                   wa_ref, wb_ref, wc_ref, b_ref, o_ref, *, n_first):
    def merge(ya_ref, yb_ref, yc_ref):
        def proj(y_ref, w_ref, k):
            return jnp.dot(y_ref[...], w_ref[...], preferred_element_type=F32) + b_ref[k:k + 1, :]
        m = g0_ref[...].astype(F32) * proj(ya_ref, wa_ref, 0)
        m = m + g1_ref[...].astype(F32) * proj(yb_ref, wb_ref, 1)
        m = m + g2_ref[...].astype(F32) * proj(yc_ref, wc_ref, 2)
        o_ref[...] = m.astype(BF16)

    first = pl.program_id(0) < n_first

    @pl.when(first)
    def _():
        merge(yap_ref, ybp_ref, ycp_ref)

    @pl.when(jnp.logical_not(first))
    def _():
        merge(yas_ref, ybs_ref, ycs_ref)


def branch_merge(y_first, y_second, u, wa, wb, wc, b_br, *, gate_col0, d, tm, tn):
    t1 = y_first[0].shape[0]
    t2 = y_second[0].shape[0]
    n1 = t1 // tm
    n2 = t2 // tm
    gb = gate_col0 // tn
    nd = d // tn

    def gate(k):
        return pl.BlockSpec((tm, tn), lambda i, j: (i, gb + k * nd + j))

    def first(y):
        return pl.BlockSpec((tm, y.shape[1]), lambda i, j: (jnp.minimum(i, n1 - 1), 0))

    def second(y):
        return pl.BlockSpec((tm, y.shape[1]), lambda i, j: (jnp.maximum(i - n1, 0), 0))

    return pl.pallas_call(
        functools.partial(_branch_kernel, n_first=n1),
        grid=(n1 + n2, nd),
        in_specs=[
            first(y_first[0]), first(y_first[1]), first(y_first[2]),
            second(y_second[0]), second(y_second[1]), second(y_second[2]),
            gate(0), gate(1), gate(2),
            pl.BlockSpec((wa.shape[0], tn), lambda i, j: (0, j)),
            pl.BlockSpec((wb.shape[0], tn), lambda i, j: (0, j)),
            pl.BlockSpec((wc.shape[0], tn), lambda i, j: (0, j)),
            pl.BlockSpec((N_BRANCH, tn), lambda i, j: (0, j)),
        ],
        out_specs=pl.BlockSpec((tm, tn), lambda i, j: (i, j)),
        out_shape=jax.ShapeDtypeStruct((t1 + t2, d), BF16),
        compiler_params=_cparams("parallel", "arbitrary"),
        name="branch_merge",
    )(*y_first, *y_second, u, u, u, wa, wb, wc, b_br)


def _out_router_kernel(x_ref, m_ref, g1_ref, sh2_ref, sc2_ref, wo_ref, bo_ref, lg_ref, lb_ref,
                       wr_ref, br_ref, x1_ref, h2_ref, idx_ref, p_ref, *, alpha):
    mix = jnp.dot(m_ref[...], wo_ref[...], preferred_element_type=F32) + bo_ref[...]
    x1 = _ln_rows(alpha * x_ref[...] + g1_ref[...] * mix) * lg_ref[...] + lb_ref[...]
    x1_ref[...] = x1
    h2 = _ln_rows(x1) * (1.0 + sc2_ref[...]) + sh2_ref[...]
    h2_ref[...] = _pack_halves(h2)
    logits = jnp.dot(h2.astype(BF16), wr_ref[...], preferred_element_type=F32) + br_ref[...]
    lane = lax.broadcasted_iota(jnp.int32, logits.shape, 1)
    idx_out = jnp.zeros(logits.shape, jnp.int32)
    p_out = jnp.zeros(logits.shape, F32)
    top = None
    denom = None
    for k in range(TOP_K):
        m = jnp.max(logits, axis=-1, keepdims=True)
        i = jnp.min(jnp.where(logits == m, lane, LANES), axis=-1, keepdims=True)
        if k == 0:
            top = m
        ek = jnp.exp(m - top)
        denom = ek if k == 0 else denom + ek
        idx_out = jnp.where(lane == k, i, idx_out)
        p_out = jnp.where(lane == k, ek, p_out)
        logits = jnp.where(lane == i, -jnp.inf, logits)
    idx_ref[...] = idx_out
    p_ref[...] = p_out / denom


def out_router(x, merged, mod3, row_of_block, wo, bo, ln_g, ln_b, wr, br, *, alpha, tm):
    t, d = x.shape

    def modspec(k):
        return pl.BlockSpec((None, 1, d), lambda i: (row_of_block(i), 0, k))

    vec = pl.BlockSpec((1, d), lambda i: (0, 0))
    row_f = pl.BlockSpec((tm, d), lambda i: (i, 0))
    row_s = pl.BlockSpec((tm, LANES), lambda i: (i, 0))
    return pl.pallas_call(
        functools.partial(_out_router_kernel, alpha=alpha),
        grid=(t // tm,),
        in_specs=[row_f, row_f, modspec(2), modspec(3), modspec(4),
                  pl.BlockSpec((d, d), lambda i: (0, 0)), vec, vec, vec,
                  pl.BlockSpec((d, LANES), lambda i: (0, 0)),
                  pl.BlockSpec((1, LANES), lambda i: (0, 0))],
        out_specs=[row_f, pl.BlockSpec((tm, d // 2), lambda i: (i, 0)), row_s, row_s],
        out_shape=[jax.ShapeDtypeStruct((t, d), F32), jax.ShapeDtypeStruct((t, d // 2), jnp.uint32),
                   jax.ShapeDtypeStruct((t, LANES), jnp.int32), jax.ShapeDtypeStruct((t, LANES), F32)],
        compiler_params=_cparams("parallel"),
        name="out_router",
    )(x, merged, mod3, mod3, mod3, wo, bo, ln_g, ln_b, wr, br)


def _moe_up_kernel(be_ref, nv_ref, x_ref, wg_ref, wu_ref, bg_ref, bu_ref, o_ref, wgs_ref, wus_ref):
    b = pl.program_id(1)
    fresh = jnp.logical_or(b == 0, be_ref[b] != be_ref[jnp.maximum(b - 1, 0)])

    @pl.when(fresh)
    def _():
        wgs_ref[...] = wg_ref[...].astype(BF16)
        wus_ref[...] = wu_ref[...].astype(BF16)

    @pl.when(b < nv_ref[0])
    def _():
        lo, hi = _unpack_halves(x_ref[...])
        lo = lo.astype(BF16)
        hi = hi.astype(BF16)
        kh = lo.shape[1]

        def proj(w_ref, b_ref):
            return (jnp.dot(lo, w_ref[0:kh, :], preferred_element_type=F32)
                    + jnp.dot(hi, w_ref[kh:2 * kh, :], preferred_element_type=F32) + b_ref[...])

        gt = proj(wgs_ref, bg_ref)
        up = proj(wus_ref, bu_ref)
        gt = jnp.minimum(gt, SWIGLU_LIMIT)
        up = jnp.clip(up, -SWIGLU_LIMIT, SWIGLU_LIMIT)
        o_ref[...] = ((up + 1.0) * (gt * _sigmoid(SWIGLU_ALPHA * gt))).astype(BF16)

    @pl.when(b >= nv_ref[0])
    def _():
        o_ref[...] = jnp.zeros_like(o_ref)


def moe_up(xs, block_exp, n_valid, w_gu, b_gu, layer, *, bm, tn):
    r = xs.shape[0]
    d = w_gu.shape[2]
    f = w_gu.shape[3] // 2
    nf = f // tn
    grid_spec = pltpu.PrefetchScalarGridSpec(
        num_scalar_prefetch=2,
        grid=(nf, r // bm),
        in_specs=[
            pl.BlockSpec((bm, d // 2), lambda n, b, be, nv: (b, 0)),
            pl.BlockSpec((None, None, d, tn), lambda n, b, be, nv: (layer, be[b], 0, n)),
            pl.BlockSpec((None, None, d, tn), lambda n, b, be, nv: (layer, be[b], 0, nf + n)),
            pl.BlockSpec((None, None, 1, tn), lambda n, b, be, nv: (layer, be[b], 0, n)),
            pl.BlockSpec((None, None, 1, tn), lambda n, b, be, nv: (layer, be[b], 0, nf + n)),
        ],
        out_specs=pl.BlockSpec((bm, tn), lambda n, b, be, nv: (b, n)),
        scratch_shapes=[pltpu.VMEM((d, tn), BF16), pltpu.VMEM((d, tn), BF16)],
    )
    return pl.pallas_call(
        _moe_up_kernel,
        grid_spec=grid_spec,
        out_shape=jax.ShapeDtypeStruct((r, f), BF16),
        compiler_params=_cparams("arbitrary", "arbitrary"),
        name="moe_up",
    )(block_exp, n_valid, xs, w_gu, w_gu, b_gu, b_gu)


def _moe_down_kernel(be_ref, nv_ref, a_ref, w_ref, b_ref, o_ref, ws_ref):
    b = pl.program_id(1)
    fresh = jnp.logical_or(b == 0, be_ref[b] != be_ref[jnp.maximum(b - 1, 0)])

    @pl.when(fresh)
    def _():
        ws_ref[...] = w_ref[...].astype(BF16)

    @pl.when(b < nv_ref[0])
    def _():
        y = jnp.dot(a_ref[...], ws_ref[...], preferred_element_type=F32) + b_ref[...]
        o_ref[...] = _pack_halves(y)

    @pl.when(b >= nv_ref[0])
    def _():
        o_ref[...] = jnp.zeros_like(o_ref)


def moe_down(act, block_exp, n_valid, w_down, b_down, layer, *, bm, tn):
    r, f = act.shape
    d = w_down.shape[3]
    grid_spec = pltpu.PrefetchScalarGridSpec(
        num_scalar_prefetch=2,
        grid=(d // tn, r // bm),
        in_specs=[
            pl.BlockSpec((bm, f), lambda n, b, be, nv: (b, 0)),
            pl.BlockSpec((None, None, f, tn), lambda n, b, be, nv: (layer, be[b], 0, n)),
            pl.BlockSpec((None, None, 1, tn), lambda n, b, be, nv: (layer, be[b], 0, n)),
        ],
        out_specs=pl.BlockSpec((bm, tn // 2), lambda n, b, be, nv: (b, n)),
        scratch_shapes=[pltpu.VMEM((f, tn), BF16)],
    )
    return pl.pallas_call(
        _moe_down_kernel,
        grid_spec=grid_spec,
        out_shape=jax.ShapeDtypeStruct((r, d // 2), jnp.uint32),
        compiler_params=_cparams("arbitrary", "arbitrary"),
        name="moe_down",
    )(block_exp, n_valid, act, w_down, b_down)


def sc_gather(x, idx):
    n = idx.shape[0]
    d = x.shape[1]
    win = SC_GATHER_ROWS
    mesh = plsc.VectorSubcoreMesh(core_axis_name="core", subcore_axis_name="subcore")

    @pl.kernel(out_type=jax.ShapeDtypeStruct((n, d), x.dtype), mesh=mesh)
    def gather_kernel(x_hbm, i_hbm, o_hbm):
        def body(i_vmem, o_vmem):
            pltpu.sync_copy(x_hbm.at[i_vmem.at[0]], o_vmem)

        pltpu.emit_pipeline(
            body,
            grid=(n // win,),
            in_specs=[pl.BlockSpec((1, win), index_map=lambda i: (i, 0))],
            out_specs=[pl.BlockSpec((win, d), index_map=lambda i: (i, 0))],
            core_axis_name=("core", "subcore"),
            dimension_semantics=(pltpu.PARALLEL,),
        )(i_hbm, o_hbm)

    return gather_kernel(x, idx.reshape(n // win, win))


def _combine_kernel(x_ref, y0_ref, y1_ref, y2_ref, y3_ref, p_ref, g2_ref, lg_ref, lb_ref, o_ref, *, alpha, group):
    p = p_ref[...]
    ffn = None
    for k, y_ref in enumerate((y0_ref, y1_ref, y2_ref, y3_ref)):
        w = y_ref[...]
        cols = []
        for c0 in range(0, w.shape[1], group):
            cols.extend(_unpack_halves(w[:, c0:c0 + group]))
        term = p[:, k:k + 1] * jnp.concatenate(cols, axis=1)
        ffn = term if ffn is None else ffn + term
    o_ref[...] = _ln_rows(alpha * x_ref[...] + g2_ref[...] * ffn) * lg_ref[...] + lb_ref[...]


def combine_ln(x1, yg, prob, mod3, row_of_block, ln_g, ln_b, *, alpha, tm, group):
    t, d = x1.shape
    nb = t // tm
    vec = pl.BlockSpec((1, d), lambda i: (0, 0))

    def sel(k):
        return pl.BlockSpec((tm, d // 2), lambda i: (k * nb + i, 0))

    return pl.pallas_call(
        functools.partial(_combine_kernel, alpha=alpha, group=group),
        grid=(nb,),
        in_specs=[pl.BlockSpec((tm, d), lambda i: (i, 0)), sel(0), sel(1), sel(2), sel(3),
                  pl.BlockSpec((tm, LANES), lambda i: (i, 0)),
                  pl.BlockSpec((None, 1, d), lambda i: (row_of_block(i), 0, 5)),
                  vec, vec],
        out_specs=pl.BlockSpec((tm, d), lambda i: (i, 0)),
        out_shape=jax.ShapeDtypeStruct((t, d), F32),
        compiler_params=_cparams("parallel"),
        name="combine_ln",
    )(x1, yg, yg, yg, yg, prob, mod3, ln_g, ln_b)


def _rank_kernel(idx_ref, rank_ref, cnt_ref, carry_ref):
    @pl.when(pl.program_id(0) == 0)
    def _():
        carry_ref[...] = jnp.zeros_like(carry_ref)

    idx = idx_ref[...]
    tb = idx.shape[0]
    lane = lax.broadcasted_iota(jnp.int32, idx.shape, 1)
    hot = [(lane == idx[:, k:k + 1]) for k in range(TOP_K)]
    tot = hot[0]
    for k in range(1, TOP_K):
        tot = jnp.logical_or(tot, hot[k])
    tot = jnp.where(tot, 1.0, 0.0)
    ti = lax.broadcasted_iota(jnp.int32, (tb, tb), 0)
    si = lax.broadcasted_iota(jnp.int32, (tb, tb), 1)
    before = jnp.where(ti > si, 1.0, 0.0).astype(BF16)
    prefix = jnp.dot(before, tot.astype(BF16), preferred_element_type=F32) + carry_ref[...]
    out = jnp.zeros(idx.shape, F32)
    for k in range(TOP_K):
        rk = jnp.sum(jnp.where(hot[k], prefix, 0.0), axis=-1, keepdims=True)
        out = jnp.where(lane == k, rk, out)
    rank_ref[...] = out.astype(jnp.int32)
    carry_ref[...] += jnp.sum(tot, axis=0, keepdims=True)
    cnt_ref[...] = carry_ref[...].astype(jnp.int32)


def route_rank(idx, *, tb):
    t = idx.shape[0]
    return pl.pallas_call(
        _rank_kernel,
        grid=(t // tb,),
        in_specs=[pl.BlockSpec((tb, LANES), lambda i: (i, 0))],
        out_specs=[pl.BlockSpec((tb, LANES), lambda i: (i, 0)), pl.BlockSpec((1, LANES), lambda i: (0, 0))],
        out_shape=[jax.ShapeDtypeStruct((t, LANES), jnp.int32), jax.ShapeDtypeStruct((1, LANES), jnp.int32)],
        scratch_shapes=[pltpu.VMEM((1, LANES), F32)],
        compiler_params=_cparams("arbitrary"),
        name="route_rank",
    )(idx)


def route_layout(top_idx, rank, counts, *, bm, n_rows):
    t = top_idx.shape[0]
    padded = (counts + bm - 1) // bm * bm
    pad_end = jnp.cumsum(padded)
    pad_start = pad_end - padded
    dest = (pad_start[top_idx] + rank).T.reshape(-1)
    row_tok = (jnp.arange(n_rows, dtype=jnp.int32) % t).at[dest].set(jnp.tile(jnp.arange(t, dtype=jnp.int32), TOP_K))
    nb = n_rows // bm
    blk_start = jnp.arange(nb, dtype=jnp.int32) * bm
    block_exp = jnp.sum((blk_start[:, None] >= pad_end[None, :]).astype(jnp.int32), axis=1)
    block_exp = jnp.minimum(block_exp, N_EXP - 1)
    n_valid = (pad_end[-1] // bm).astype(jnp.int32).reshape(1)
    return dest, row_tok, block_exp, n_valid


def grid_pos_embed(n_tok, d):
    rows = n_tok // GRID_W
    r = jnp.repeat(jnp.arange(rows, dtype=F32), GRID_W)
    col = jnp.tile(jnp.arange(GRID_W, dtype=F32), rows)
    quarter = d // 4
    omega = 1.0 / (10000.0 ** (jnp.arange(quarter, dtype=F32) / quarter))

    def emb(p):
        a = p[:, None] * omega[None, :]
        return jnp.concatenate([jnp.sin(a), jnp.cos(a)], axis=-1)

    return jnp.concatenate([emb(r), emb(col)], axis=-1)


def kernel(x_prompt, x_sample, state_hgrn, c, c_ctx, w_ada, b_ada, w_in, b_in, lb_logits, g_norm_a, conv_w, conv_b, conv_ln_g, conv_ln_b, w_br_a, w_br_b, w_br_c, b_br, w_gate, b_gate, w_o, b_o, ln1_g, ln1_b, w_router, b_router, w_gu, b_gu, w_down, b_down, ln2_g, ln2_b):
    depth = w_in.shape[0]
    bp, lp, d = x_prompt.shape
    bs, ls, _ = x_sample.shape
    tp, ts = bp * lp, bs * ls
    t = tp + ts
    alpha = (2 * depth) ** 0.25
    w_in_cols = w_in.shape[2]
    col_b = 3 * H_A * DK + 2 * W_A
    col_c = col_b + W_B

    tm, tn, tm_o, bm = TM, TN, TM_OUT, MOE_BM
    n_rows = (t * TOP_K // bm + N_EXP) * bm

    def row_of_block_for(rows_per_block):
        npb = tp // rows_per_block
        per = ls // rows_per_block
        return lambda i: jnp.where(i < npb, 0, 1 + (i - npb) // per)

    sm = jax.nn.softmax(lb_logits.astype(F32), axis=0)
    cs = jnp.cumsum(sm, axis=0)
    lower = cs - cs[0:1]

    cond = jnp.zeros((16, d), F32).at[0].set(c_ctx).at[1:1 + bs].set(c)
    mod = ada_mod(cond, w_ada, b_ada)

    xs = x_sample + grid_pos_embed(ls, d)[None]
    x = jnp.concatenate([x_prompt.reshape(tp, d), xs.reshape(ts, d)], axis=0)

    b_gu4 = b_gu[:, :, None, :]
    b_down4 = b_down[:, :, None, :]
    dl_p, cm = dft_tables(lp)
    dl_s, _ = dft_tables(ls)

    states = []
    for l in range(depth):
        mod3 = mod[l].reshape(16, 1, 6 * d)
        w_cat = jnp.concatenate([w_in[l], w_gate[l]], axis=1).astype(BF16)
        b_cat = jnp.concatenate([b_in[l], b_gate[l]])[None, :]
        u = ln_mod_matmul(x, mod3, row_of_block_for(TM_IN), w_cat, b_cat, n_plain_cols=w_in_cols, tm=TM_IN, tn=tn)

        gn = g_norm_a[l][None, :]
        ya_p, s_ctx = hgrn_mixer(u, lower[l], gn, None, row0=0, batch=bp, seq=lp, want_state=True)
        (ya_s,) = hgrn_mixer(u, lower[l], gn, state_hgrn[:, l], row0=tp, batch=bs, seq=ls, want_state=False)
        states.append(s_ctx)
        yb_p = fourier_mixer(u, dl_p, cm, row0=0, col0=col_b, batch=bp, seq=lp)
        yb_s = fourier_mixer(u, dl_s, cm, row0=tp, col0=col_b, batch=bs, seq=ls)
        cw = (conv_w[l], conv_b[l][None, :], conv_ln_g[l][None, :], conv_ln_b[l][None, :])
        yc_p = conv_mixer(u, *cw, row0=0, col0=col_c, batch=bp, seq=lp)
        yc_s = conv_mixer(u, *cw, row0=tp, col0=col_c, batch=bs, seq=ls)
        merged = branch_merge((ya_p, yb_p, yc_p), (ya_s, yb_s, yc_s), u, w_br_a[l].astype(BF16),
                              w_br_b[l].astype(BF16), w_br_c[l].astype(BF16), b_br[l],
                              gate_col0=w_in_cols, d=d, tm=tm, tn=tn)

        wr = jnp.zeros((d, LANES), BF16).at[:, :N_EXP].set(w_router[l].astype(BF16))
        br = jnp.full((1, LANES), -1e30, F32).at[0, :N_EXP].set(b_router[l])
        x1, h2, idx, prob = out_router(x, merged, mod3, row_of_block_for(tm_o), w_o[l].astype(BF16),
                                       b_o[l][None, :], ln1_g[l][None, :], ln1_b[l][None, :], wr, br,
                                       alpha=alpha, tm=tm_o)

        rank, counts = route_rank(idx, tb=RANK_ROWS)
        dest, row_tok, block_exp, n_valid = route_layout(idx[:, :TOP_K], rank[:, :TOP_K], counts[0, :N_EXP],
                                                         bm=bm, n_rows=n_rows)
        xsort = sc_gather(h2, row_tok)
        act = moe_up(xsort, block_exp, n_valid, w_gu, b_gu4, l, bm=bm, tn=TN_UP)
        yrow = moe_down(act, block_exp, n_valid, w_down, b_down4, l, bm=bm, tn=TN_DOWN)
        yg = sc_gather(yrow, dest)
        x = combine_ln(x1, yg, prob, mod3, row_of_block_for(tm_o), ln2_g[l][None, :], ln2_b[l][None, :],
                       alpha=alpha, tm=tm_o, group=TN_DOWN // 2)

    y_prompt = x[:tp].reshape(bp, lp, d)
    y_sample = x[tp:].reshape(bs, ls, d)
    return y_prompt, y_sample, jnp.stack(states, axis=1)
```

```python
import functools
import math

import jax
import jax.numpy as jnp
from jax import lax
from jax.experimental import pallas as pl
from jax.experimental.pallas import tpu as pltpu
from jax.experimental.pallas import tpu_sc as plsc

F32 = jnp.float32
BF16 = jnp.bfloat16

GRID_W = 64
H_A = 8
DK = 128
DV = 128
W_A = H_A * DV
W_B = 512
W_C = 512
CONV_K = 31
N_BRANCH = 3
N_EXP = 32
TOP_K = 4
SWIGLU_LIMIT = 7.0
SWIGLU_ALPHA = 1.702
LN_EPS = 1e-5
RMS_EPS = 1e-6

LANES = 128
SUBLANES = 8
VMEM_LIMIT = 56 * 1024 * 1024
HGRN_CHUNK = 16
HGRN_UNROLL = 16
CONV_PAD = 16
CONV_ROWS = 32
LOGF_FLOOR = -1e4
LOG2E = 1.4426950408889634
TM = 1024
TM_IN = 2048
TN = 512
TM_OUT = 256
MOE_BM = 512
TN_UP = 1024
TN_DOWN = 2048
RANK_ROWS = 512
SC_GATHER_ROWS = 32


def _cparams(*sem):
    return pltpu.CompilerParams(dimension_semantics=sem, vmem_limit_bytes=VMEM_LIMIT)


def _ln_rows(x):
    mu = jnp.mean(x, axis=-1, keepdims=True)
    xc = x - mu
    var = jnp.mean(xc * xc, axis=-1, keepdims=True)
    return xc * lax.rsqrt(var + LN_EPS)


def _sigmoid(x):
    return 1.0 / (1.0 + jnp.exp(-x))


def _bf16_bits(x):
    u = lax.bitcast_convert_type(x, jnp.uint32)
    r = u + jnp.uint32(0x7FFF) + ((u >> 16) & jnp.uint32(1))
    return r & jnp.uint32(0xFFFF0000)


def _pack_halves(x):
    n = x.shape[1] // 2
    return (_bf16_bits(x[:, :n]) >> 16) | _bf16_bits(x[:, n:])


def _unpack_halves(w):
    lo = lax.bitcast_convert_type(w << 16, F32)
    hi = lax.bitcast_convert_type(w & jnp.uint32(0xFFFF0000), F32)
    return lo, hi


def _ada_kernel(c_ref, w_ref, b_ref, o_ref):
    c = c_ref[...]
    a = (c * _sigmoid(c)).astype(BF16)
    o_ref[...] = jnp.dot(a, w_ref[...].astype(BF16), preferred_element_type=F32) + b_ref[...]


def ada_mod(cond, w_ada, b_ada, *, tn=1024):
    depth, d, n = w_ada.shape
    r = cond.shape[0]
    return pl.pallas_call(
        _ada_kernel,
        grid=(depth, n // tn),
        in_specs=[
            pl.BlockSpec((r, d), lambda l, j: (0, 0)),
            pl.BlockSpec((None, d, tn), lambda l, j: (l, 0, j)),
            pl.BlockSpec((None, 1, tn), lambda l, j: (l, 0, j)),
        ],
        out_specs=pl.BlockSpec((None, r, tn), lambda l, j: (l, 0, j)),
        out_shape=jax.ShapeDtypeStruct((depth, r, n), F32),
        compiler_params=_cparams("parallel", "parallel"),
        name="ada_mod",
    )(cond, w_ada, b_ada.reshape(depth, 1, n))


def _lnmm_kernel(x_ref, sh_ref, sc_ref, wi_ref, bi_ref, wg_ref, bg_ref, o_ref, h_ref, *, n_plain):
    j = pl.program_id(1)

    @pl.when(j == 0)
    def _():
        y = _ln_rows(x_ref[...])
        h_ref[...] = (y * (1.0 + sc_ref[...]) + sh_ref[...]).astype(BF16)

    def project(w_ref, b_ref):
        return jnp.dot(h_ref[...], w_ref[...], preferred_element_type=F32) + b_ref[...]

    @pl.when(j < n_plain)
    def _():
        o_ref[...] = project(wi_ref, bi_ref).astype(BF16)

    @pl.when(j >= n_plain)
    def _():
        o_ref[...] = _sigmoid(project(wg_ref, bg_ref)).astype(BF16)


def ln_mod_matmul(x, mod3, row_of_block, w_in, b_in, w_gate, b_gate, *, tm, tn):
    t, d = x.shape
    n1 = w_in.shape[1] // tn
    n2 = w_gate.shape[1] // tn

    def first(j):
        return jnp.minimum(j, n1 - 1)

    def second(j):
        return jnp.maximum(j - n1, 0)

    return pl.pallas_call(
        functools.partial(_lnmm_kernel, n_plain=n1),
        grid=(t // tm, n1 + n2),
        in_specs=[
            pl.BlockSpec((tm, d), lambda i, j: (i, 0), pipeline_mode=pl.Buffered(1)),
            pl.BlockSpec((None, 1, d), lambda i, j: (row_of_block(i), 0, 0)),
            pl.BlockSpec((None, 1, d), lambda i, j: (row_of_block(i), 0, 1)),
            pl.BlockSpec((d, tn), lambda i, j: (0, first(j))),
            pl.BlockSpec((1, tn), lambda i, j: (0, first(j))),
            pl.BlockSpec((d, tn), lambda i, j: (0, second(j))),
            pl.BlockSpec((1, tn), lambda i, j: (0, second(j))),
        ],
        out_specs=pl.BlockSpec((tm, tn), lambda i, j: (i, j)),
        out_shape=jax.ShapeDtypeStruct((t, (n1 + n2) * tn), BF16),
        scratch_shapes=[pltpu.VMEM((tm, d), BF16)],
        compiler_params=_cparams("parallel", "arbitrary"),
        name="ln_mod_matmul",
    )(x, mod3, mod3, w_in, b_in, w_gate, b_gate)


def _chunk_scan(x_h, direction):
    half = SUBLANES
    rid = lax.broadcasted_iota(jnp.int32, x_h[0].shape, 0)
    out = []
    for x in x_h:
        step = 1
        while step < half:
            if direction == 0:
                x = x + jnp.where(rid >= step, pltpu.roll(x, step, axis=0), 0.0)
            else:
                x = x + jnp.where(rid < half - step, pltpu.roll(x, half - step, axis=0), 0.0)
            step *= 2
        out.append(x)
    if direction == 0:
        out[1] = out[1] + out[0][half - 1:half, :]
    else:
        out[0] = out[0] + out[1][0:1, :]
    return out


def _hgrn_chunk(q_ref, z_ref, v_ref, lb, oml, st_ref, o_ref, rows_ref, c0, *, direction):
    half = SUBLANES
    rows = pl.ds(c0, HGRN_CHUNK)
    q = q_ref[rows, :].astype(F32) * (DK ** -0.5)
    z = z_ref[rows, :].astype(F32)
    v = v_ref[rows, :]
    e = jnp.exp2(jnp.abs(z) * (-LOG2E))
    r = 1.0 / (1.0 + e)
    er = e * r
    pos = z >= 0
    lf2 = jnp.maximum(jnp.log2(lb + oml * jnp.where(pos, r, er)), LOGF_FLOOR)
    lk2 = jnp.log2(oml * jnp.where(pos, er, r))

    b_h = _chunk_scan([lf2[0:half, :], lf2[half:HGRN_CHUNK, :]], direction)
    b2 = jnp.concatenate(b_h, axis=0)
    b_end = b_h[1][half - 1:half, :] if direction == 0 else b_h[0][0:1, :]

    st = st_ref[direction]
    qe = (q * jnp.exp2(b2)).astype(BF16)
    o_inter = lax.dot_general(qe, st.astype(BF16), (((1,), (1,)), ((), ())), preferred_element_type=F32)

    rows_ref[0] = b2 - lk2
    rows_ref[1] = v.astype(F32)
    q_h = (q[0:half, :], q[half:HGRN_CHUNK, :])
    acc = [jnp.zeros((half, DV), F32), jnp.zeros((half, DV), F32)]
    rid = lax.broadcasted_iota(jnp.int32, (half, DK), 0)
    for s in range(HGRN_CHUNK):
        hs, rs = divmod(s, half)
        bs = jnp.broadcast_to(rows_ref[0, s:s + 1, :], (half, DK))
        vs = jnp.broadcast_to(rows_ref[1, s:s + 1, :], (half, DV))
        for ht in range(2):
            if (ht < hs) if direction == 0 else (ht > hs):
                continue
            d = b_h[ht] - bs
            if ht == hs:
                keep = (rid >= rs) if direction == 0 else (rid <= rs)
                d = jnp.where(keep, d, -1e30)
            w = jnp.sum(q_h[ht] * jnp.exp2(d), axis=-1, keepdims=True)
            acc[ht] = acc[ht] + w * vs
    o_ref[pl.ds(c0, half), :] = acc[0] + o_inter[0:half, :]
    o_ref[pl.ds(c0 + half, half), :] = acc[1] + o_inter[half:HGRN_CHUNK, :]

    kk = jnp.exp2(lk2 + (b_end - b2)).astype(BF16)
    upd = lax.dot_general(v, kk, (((0,), (0,)), ((), ())), preferred_element_type=F32)
    st_ref[direction] = st * jnp.exp2(b_end) + upd


def _hgrn_kernel(*refs, seq, unroll, has_s0, want_state):
    q_ref, zf_ref, zb_ref, v_ref, g_ref, lb_ref, gn_ref = refs[:7]
    pos = 7
    s0_ref = None
    if has_s0:
        s0_ref = refs[pos]
        pos += 1
    y_ref = refs[pos]
    pos += 1
    sfin_ref = None
    if want_state:
        sfin_ref = refs[pos]
        pos += 1
    of_ref, ob_ref, st_ref, rows_ref = refs[pos:pos + 4]

    for d in range(2):
        st_ref[d] = s0_ref[d].T if has_s0 else jnp.zeros((DV, DK), F32)

    n = seq // HGRN_CHUNK
    lb_f = lb_ref[0:1, :]
    lb_b = lb_ref[1:2, :]
    oml_f = 1.0 - lb_f
    oml_b = 1.0 - lb_b

    def body(i, carry):
        for j in range(unroll):
            cf = pl.multiple_of((i * unroll + j) * HGRN_CHUNK, HGRN_CHUNK)
            cb = pl.multiple_of((n - 1 - (i * unroll + j)) * HGRN_CHUNK, HGRN_CHUNK)
            _hgrn_chunk(q_ref, zf_ref, v_ref, lb_f, oml_f, st_ref, of_ref, rows_ref.at[2 * j], cf, direction=0)
            _hgrn_chunk(q_ref, zb_ref, v_ref, lb_b, oml_b, st_ref, ob_ref, rows_ref.at[2 * j + 1], cb, direction=1)
        return carry

    lax.fori_loop(0, n // unroll, body, 0)

    blk = min(seq, 256)
    for r0 in range(0, seq, blk):
        o = of_ref[r0:r0 + blk, :] + ob_ref[r0:r0 + blk, :]
        g = g_ref[r0:r0 + blk, :].astype(F32)
        o = o * lax.rsqrt(jnp.mean(o * o, axis=-1, keepdims=True) + RMS_EPS) * gn_ref[...]
        y_ref[r0:r0 + blk, :] = (o * (g * _sigmoid(g))).astype(BF16)
    if want_state:
        sfin_ref[0] = st_ref[0].T
        sfin_ref[1] = st_ref[1].T


def hgrn_mixer(u, lb, g_norm, s0, *, row0, batch, seq, want_state, unroll=HGRN_UNROLL):
    rb0 = row0 // seq
    has_s0 = s0 is not None
    unroll = math.gcd(seq // HGRN_CHUNK, unroll)

    def sec(k):
        return pl.BlockSpec((seq, LANES), lambda b, h: (rb0 + b, k * H_A + h))

    in_specs = [sec(0), sec(1), sec(2), sec(3), sec(4),
                pl.BlockSpec((2, DK), lambda b, h: (0, h)),
                pl.BlockSpec((1, DV), lambda b, h: (0, 0))]
    args = [u, u, u, u, u, lb, g_norm]
    state_spec = pl.BlockSpec((None, 2, None, DK, DV), lambda b, h: (b, 0, h, 0, 0))
    if has_s0:
        in_specs.append(state_spec)
        args.append(s0)
    out_specs = [pl.BlockSpec((seq, DV), lambda b, h: (b, h))]
    out_shape = [jax.ShapeDtypeStruct((batch * seq, W_A), BF16)]
    if want_state:
        out_specs.append(state_spec)
        out_shape.append(jax.ShapeDtypeStruct((batch, 2, H_A, DK, DV), F32))
    return pl.pallas_call(
        functools.partial(_hgrn_kernel, seq=seq, unroll=unroll, has_s0=has_s0, want_state=want_state),
        grid=(batch, H_A),
        in_specs=in_specs,
        out_specs=out_specs,
        out_shape=out_shape,
        scratch_shapes=[pltpu.VMEM((seq, DV), F32), pltpu.VMEM((seq, DV), F32), pltpu.VMEM((2, DV, DK), F32),
                        pltpu.VMEM((2 * unroll, 2, HGRN_CHUNK, DK), F32)],
        compiler_params=_cparams("parallel", "parallel"),
        name="hgrn_mixer",
    )(*args)


def _fourier_kernel(u_ref, cm_ref, dl_ref, o_ref, ab_ref, *, seq, rows, scale):
    @pl.when(pl.program_id(1) == 0)
    def _():
        for r0 in range(0, seq, rows):
            ab = jnp.dot(u_ref[r0:r0 + rows, :], cm_ref[...], preferred_element_type=F32)
            ab_ref[r0:r0 + rows, :] = ab[:, :W_B].astype(BF16)
            ab_ref[seq + r0:seq + r0 + rows, :] = ab[:, W_B:].astype(BF16)

    o_ref[...] = (jnp.dot(dl_ref[...], ab_ref[...], preferred_element_type=F32) * scale).astype(BF16)


def dft_tables(seq):
    def cs(n):
        i = lax.iota(jnp.int32, n)
        ang = ((i[:, None] * i[None, :]) % n).astype(F32) * (2.0 * math.pi / n)
        return jnp.cos(ang), jnp.sin(ang)
    cl, sl = cs(seq)
    cc, sc = cs(W_B)
    return jnp.concatenate([cl, -sl], axis=1).astype(BF16), jnp.concatenate([cc, sc], axis=1).astype(BF16)


def fourier_mixer(u, dl, cm, *, row0, col0, batch, seq, rows=256):
    rows = min(rows, seq)
    rb0 = row0 // seq
    cb = col0 // W_B
    nt = seq // rows
    return pl.pallas_call(
        functools.partial(_fourier_kernel, seq=seq, rows=rows, scale=1.0 / math.sqrt(seq * W_B)),
        grid=(batch, nt),
        in_specs=[
            pl.BlockSpec((seq, W_B), lambda b, i: (rb0 + b, cb)),
            pl.BlockSpec((W_B, 2 * W_B), lambda b, i: (0, 0)),
            pl.BlockSpec((rows, 2 * seq), lambda b, i: (i, 0)),
        ],
        out_specs=pl.BlockSpec((rows, W_B), lambda b, i: (b * nt + i, 0)),
        out_shape=jax.ShapeDtypeStruct((batch * seq, W_B), BF16),
        scratch_shapes=[pltpu.VMEM((2 * seq, W_B), BF16)],
        compiler_params=_cparams("parallel", "arbitrary"),
        name="fourier_mixer",
    )(u, cm, dl)


def _conv_kernel(a_ref, gt_ref, w_ref, b_ref, g_ref, be_ref, o_ref, z_ref, sh_ref, *, seq):
    zeros = jnp.zeros((CONV_PAD, W_C), F32)
    z_ref[0:CONV_PAD, :] = zeros
    z_ref[CONV_PAD + seq:2 * CONV_PAD + seq, :] = zeros
    blk = min(seq, 256)
    for r0 in range(0, seq, blk):
        a = a_ref[r0:r0 + blk, :].astype(F32)
        gt = gt_ref[r0:r0 + blk, :].astype(F32)
        z_ref[CONV_PAD + r0:CONV_PAD + r0 + blk, :] = a * _sigmoid(gt)

    off = CONV_PAD - CONV_K // 2

    def tile(i, carry):
        r0 = pl.multiple_of(i * CONV_ROWS, CONV_ROWS)
        win = z_ref[pl.ds(r0, CONV_ROWS + 2 * CONV_PAD), :]
        span = CONV_ROWS + 2 * CONV_PAD - SUBLANES
        for r in range(SUBLANES):
            sh_ref[r] = win[r:r + span, :]
        acc = jnp.zeros((CONV_ROWS, W_C), F32)
        for k in range(CONV_K):
            m, r = divmod(off + k, SUBLANES)
            acc = acc + sh_ref[r, m * SUBLANES:m * SUBLANES + CONV_ROWS, :] * w_ref[k:k + 1, :]
        y = _ln_rows(acc + b_ref[...]) * g_ref[...] + be_ref[...]
        o_ref[pl.ds(r0, CONV_ROWS), :] = (y * _sigmoid(y)).astype(BF16)
        return carry

    lax.fori_loop(0, seq // CONV_ROWS, tile, 0)


def conv_mixer(u, conv_w, conv_b, ln_g, ln_b, *, row0, col0, batch, seq):
    rb0 = row0 // seq
    cb = col0 // W_C
    vec = pl.BlockSpec((1, W_C), lambda b: (0, 0))
    return pl.pallas_call(
        functools.partial(_conv_kernel, seq=seq),
        grid=(batch,),
        in_specs=[
            pl.BlockSpec((seq, W_C), lambda b: (rb0 + b, cb)),
            pl.BlockSpec((seq, W_C), lambda b: (rb0 + b, cb + 1)),
            pl.BlockSpec((CONV_K, W_C), lambda b: (0, 0)),
            vec, vec, vec,
        ],
        out_specs=pl.BlockSpec((seq, W_C), lambda b: (b, 0)),
        out_shape=jax.ShapeDtypeStruct((batch * seq, W_C), BF16),
        scratch_shapes=[pltpu.VMEM((seq + 2 * CONV_PAD, W_C), F32),
                        pltpu.VMEM((SUBLANES, CONV_ROWS + 2 * CONV_PAD - SUBLANES, W_C), F32)],
        compiler_params=_cparams("parallel"),
        name="conv_mixer",
    )(u, u, conv_w, conv_b, ln_g, ln_b)


def _branch_kernel(yap_ref, ybp_ref, ycp_ref, yas_ref, ybs_ref, ycs_ref, g0_ref, g1_ref, g2_ref,
                   wa_ref, wb_ref, wc_ref, b_ref, o_ref, *, n_first):
    def merge(ya_ref, yb_ref, yc_ref):
        def proj(y_ref, w_ref, k):
            return jnp.dot(y_ref[...], w_ref[...], preferred_element_type=F32) + b_ref[k:k + 1, :]
        m = g0_ref[...].astype(F32) * proj(ya_ref, wa_ref, 0)
        m = m + g1_ref[...].astype(F32) * proj(yb_ref, wb_ref, 1)
        m = m + g2_ref[...].astype(F32) * proj(yc_ref, wc_ref, 2)
        o_ref[...] = m.astype(BF16)

    first = pl.program_id(0) < n_first

    @pl.when(first)
    def _():
        merge(yap_ref, ybp_ref, ycp_ref)

    @pl.when(jnp.logical_not(first))
    def _():
        merge(yas_ref, ybs_ref, ycs_ref)


def branch_merge(y_first, y_second, u, wa, wb, wc, b_br, *, gate_col0, d, tm, tn):
    t1 = y_first[0].shape[0]
    t2 = y_second[0].shape[0]
    n1 = t1 // tm
    n2 = t2 // tm
    gb = gate_col0 // tn
    nd = d // tn

    def gate(k):
        return pl.BlockSpec((tm, tn), lambda i, j: (i, gb + k * nd + j))

    def first(y):
        return pl.BlockSpec((tm, y.shape[1]), lambda i, j: (jnp.minimum(i, n1 - 1), 0))

    def second(y):
        return pl.BlockSpec((tm, y.shape[1]), lambda i, j: (jnp.maximum(i - n1, 0), 0))

    return pl.pallas_call(
        functools.partial(_branch_kernel, n_first=n1),
        grid=(n1 + n2, nd),
        in_specs=[
            first(y_first[0]), first(y_first[1]), first(y_first[2]),
            second(y_second[0]), second(y_second[1]), second(y_second[2]),
            gate(0), gate(1), gate(2),
            pl.BlockSpec((wa.shape[0], tn), lambda i, j: (0, j)),
            pl.BlockSpec((wb.shape[0], tn), lambda i, j: (0, j)),
            pl.BlockSpec((wc.shape[0], tn), lambda i, j: (0, j)),
            pl.BlockSpec((N_BRANCH, tn), lambda i, j: (0, j)),
        ],
        out_specs=pl.BlockSpec((tm, tn), lambda i, j: (i, j)),
        out_shape=jax.ShapeDtypeStruct((t1 + t2, d), BF16),
        compiler_params=_cparams("parallel", "arbitrary"),
        name="branch_merge",
    )(*y_first, *y_second, u, u, u, wa, wb, wc, b_br)


def _out_router_kernel(x_ref, m_ref, g1_ref, sh2_ref, sc2_ref, wo_ref, bo_ref, lg_ref, lb_ref,
                       wr_ref, br_ref, x1_ref, h2_ref, idx_ref, p_ref, *, alpha):
    mix = jnp.dot(m_ref[...], wo_ref[...], preferred_element_type=F32) + bo_ref[...]
    x1 = _ln_rows(alpha * x_ref[...] + g1_ref[...] * mix) * lg_ref[...] + lb_ref[...]
    x1_ref[...] = x1
    h2 = _ln_rows(x1) * (1.0 + sc2_ref[...]) + sh2_ref[...]
    h2_ref[...] = _pack_halves(h2)
    logits = jnp.dot(h2.astype(BF16), wr_ref[...], preferred_element_type=F32) + br_ref[...]
    lane = lax.broadcasted_iota(jnp.int32, logits.shape, 1)
    idx_out = jnp.zeros(logits.shape, jnp.int32)
    p_out = jnp.zeros(logits.shape, F32)
    top = None
    denom = None
    for k in range(TOP_K):
        m = jnp.max(logits, axis=-1, keepdims=True)
        i = jnp.min(jnp.where(logits == m, lane, LANES), axis=-1, keepdims=True)
        if k == 0:
            top = m
        ek = jnp.exp(m - top)
        denom = ek if k == 0 else denom + ek
        idx_out = jnp.where(lane == k, i, idx_out)
        p_out = jnp.where(lane == k, ek, p_out)
        logits = jnp.where(lane == i, -jnp.inf, logits)
    idx_ref[...] = idx_out
    p_ref[...] = p_out / denom


def out_router(x, merged, mod3, row_of_block, wo, bo, ln_g, ln_b, wr, br, *, alpha, tm):
    t, d = x.shape

    def modspec(k):
        return pl.BlockSpec((None, 1, d), lambda i: (row_of_block(i), 0, k))

    vec = pl.BlockSpec((1, d), lambda i: (0, 0))
    row_f = pl.BlockSpec((tm, d), lambda i: (i, 0))
    row_s = pl.BlockSpec((tm, LANES), lambda i: (i, 0))
    return pl.pallas_call(
        functools.partial(_out_router_kernel, alpha=alpha),
        grid=(t // tm,),
        in_specs=[row_f, row_f, modspec(2), modspec(3), modspec(4),
                  pl.BlockSpec((d, d), lambda i: (0, 0)), vec, vec, vec,
                  pl.BlockSpec((d, LANES), lambda i: (0, 0)),
                  pl.BlockSpec((1, LANES), lambda i: (0, 0))],
        out_specs=[row_f, pl.BlockSpec((tm, d // 2), lambda i: (i, 0)), row_s, row_s],
        out_shape=[jax.ShapeDtypeStruct((t, d), F32), jax.ShapeDtypeStruct((t, d // 2), jnp.uint32),
                   jax.ShapeDtypeStruct((t, LANES), jnp.int32), jax.ShapeDtypeStruct((t, LANES), F32)],
        compiler_params=_cparams("parallel"),
        name="out_router",
    )(x, merged, mod3, mod3, mod3, wo, bo, ln_g, ln_b, wr, br)


def _moe_up_kernel(be_ref, nv_ref, x_ref, wg_ref, wu_ref, bg_ref, bu_ref, o_ref, wgs_ref, wus_ref):
    b = pl.program_id(1)
    fresh = jnp.logical_or(b == 0, be_ref[b] != be_ref[jnp.maximum(b - 1, 0)])

    @pl.when(fresh)
    def _():
        wgs_ref[...] = wg_ref[...].astype(BF16)
        wus_ref[...] = wu_ref[...].astype(BF16)

    @pl.when(b < nv_ref[0])
    def _():
        lo, hi = _unpack_halves(x_ref[...])
        lo = lo.astype(BF16)
        hi = hi.astype(BF16)
        kh = lo.shape[1]

        def proj(w_ref, b_ref):
            return (jnp.dot(lo, w_ref[0:kh, :], preferred_element_type=F32)
                    + jnp.dot(hi, w_ref[kh:2 * kh, :], preferred_element_type=F32) + b_ref[...])

        gt = proj(wgs_ref, bg_ref)
        up = proj(wus_ref, bu_ref)
        gt = jnp.minimum(gt, SWIGLU_LIMIT)
        up = jnp.clip(up, -SWIGLU_LIMIT, SWIGLU_LIMIT)
        o_ref[...] = ((up + 1.0) * (gt * _sigmoid(SWIGLU_ALPHA * gt))).astype(BF16)

    @pl.when(b >= nv_ref[0])
    def _():
        o_ref[...] = jnp.zeros_like(o_ref)


def moe_up(xs, block_exp, n_valid, w_gu, b_gu, layer, *, bm, tn):
    r = xs.shape[0]
    d = w_gu.shape[2]
    f = w_gu.shape[3] // 2
    nf = f // tn
    grid_spec = pltpu.PrefetchScalarGridSpec(
        num_scalar_prefetch=2,
        grid=(nf, r // bm),
        in_specs=[
            pl.BlockSpec((bm, d // 2), lambda n, b, be, nv: (b, 0)),
            pl.BlockSpec((None, None, d, tn), lambda n, b, be, nv: (layer, be[b], 0, n)),
            pl.BlockSpec((None, None, d, tn), lambda n, b, be, nv: (layer, be[b], 0, nf + n)),
            pl.BlockSpec((None, None, 1, tn), lambda n, b, be, nv: (layer, be[b], 0, n)),
            pl.BlockSpec((None, None, 1, tn), lambda n, b, be, nv: (layer, be[b], 0, nf + n)),
        ],
        out_specs=pl.BlockSpec((bm, tn), lambda n, b, be, nv: (b, n)),
        scratch_shapes=[pltpu.VMEM((d, tn), BF16), pltpu.VMEM((d, tn), BF16)],
    )
    return pl.pallas_call(
        _moe_up_kernel,
        grid_spec=grid_spec,
        out_shape=jax.ShapeDtypeStruct((r, f), BF16),
        compiler_params=_cparams("arbitrary", "arbitrary"),
        name="moe_up",
    )(block_exp, n_valid, xs, w_gu, w_gu, b_gu, b_gu)


def _moe_down_kernel(be_ref, nv_ref, a_ref, w_ref, b_ref, o_ref, ws_ref):
    b = pl.program_id(1)
    fresh = jnp.logical_or(b == 0, be_ref[b] != be_ref[jnp.maximum(b - 1, 0)])

    @pl.when(fresh)
    def _():
        ws_ref[...] = w_ref[...].astype(BF16)

    @pl.when(b < nv_ref[0])
    def _():
        y = jnp.dot(a_ref[...], ws_ref[...], preferred_element_type=F32) + b_ref[...]
        o_ref[...] = _pack_halves(y)

    @pl.when(b >= nv_ref[0])
    def _():
        o_ref[...] = jnp.zeros_like(o_ref)


def moe_down(act, block_exp, n_valid, w_down, b_down, layer, *, bm, tn):
    r, f = act.shape
    d = w_down.shape[3]
    grid_spec = pltpu.PrefetchScalarGridSpec(
        num_scalar_prefetch=2,
        grid=(d // tn, r // bm),
        in_specs=[
            pl.BlockSpec((bm, f), lambda n, b, be, nv: (b, 0)),
            pl.BlockSpec((None, None, f, tn), lambda n, b, be, nv: (layer, be[b], 0, n)),
            pl.BlockSpec((None, None, 1, tn), lambda n, b, be, nv: (layer, be[b], 0, n)),
        ],
        out_specs=pl.BlockSpec((bm, tn // 2), lambda n, b, be, nv: (b, n)),
        scratch_shapes=[pltpu.VMEM((f, tn), BF16)],
    )
    return pl.pallas_call(
        _moe_down_kernel,
        grid_spec=grid_spec,
        out_shape=jax.ShapeDtypeStruct((r, d // 2), jnp.uint32),
        compiler_params=_cparams("arbitrary", "arbitrary"),
        name="moe_down",
    )(block_exp, n_valid, act, w_down, b_down)


def sc_gather(x, idx):
    n = idx.shape[0]
    d = x.shape[1]
    win = SC_GATHER_ROWS
    mesh = plsc.VectorSubcoreMesh(core_axis_name="core", subcore_axis_name="subcore")

    @pl.kernel(out_type=jax.ShapeDtypeStruct((n, d), x.dtype), mesh=mesh)
    def gather_kernel(x_hbm, i_hbm, o_hbm):
        def body(i_vmem, o_vmem):
            pltpu.sync_copy(x_hbm.at[i_vmem.at[0]], o_vmem)

        pltpu.emit_pipeline(
            body,
            grid=(n // win,),
            in_specs=[pl.BlockSpec((1, win), index_map=lambda i: (i, 0))],
            out_specs=[pl.BlockSpec((win, d), index_map=lambda i: (i, 0))],
            core_axis_name=("core", "subcore"),
            dimension_semantics=(pltpu.PARALLEL,),
        )(i_hbm, o_hbm)

    return gather_kernel(x, idx.reshape(n // win, win))


def _combine_kernel(x_ref, y0_ref, y1_ref, y2_ref, y3_ref, p_ref, g2_ref, lg_ref, lb_ref, o_ref, *, alpha, group):
    p = p_ref[...]
    ffn = None
    for k, y_ref in enumerate((y0_ref, y1_ref, y2_ref, y3_ref)):
        w = y_ref[...]
        cols = []
        for c0 in range(0, w.shape[1], group):
            cols.extend(_unpack_halves(w[:, c0:c0 + group]))
        term = p[:, k:k + 1] * jnp.concatenate(cols, axis=1)
        ffn = term if ffn is None else ffn + term
    o_ref[...] = _ln_rows(alpha * x_ref[...] + g2_ref[...] * ffn) * lg_ref[...] + lb_ref[...]


def combine_ln(x1, yg, prob, mod3, row_of_block, ln_g, ln_b, *, alpha, tm, group, row0=0, rows=None):
    t, d = x1.shape
    rows = t if rows is None else rows
    nb = t // tm
    b0 = row0 // tm
    vec = pl.BlockSpec((1, d), lambda i: (0, 0))

    def sel(k):
        return pl.BlockSpec((tm, d // 2), lambda i: (k * nb + b0 + i, 0))

    return pl.pallas_call(
        functools.partial(_combine_kernel, alpha=alpha, group=group),
        grid=(rows // tm,),
        in_specs=[pl.BlockSpec((tm, d), lambda i: (b0 + i, 0)), sel(0), sel(1), sel(2), sel(3),
                  pl.BlockSpec((tm, LANES), lambda i: (b0 + i, 0)),
                  pl.BlockSpec((None, 1, d), lambda i: (row_of_block(b0 + i), 0, 5)),
                  vec, vec],
        out_specs=pl.BlockSpec((tm, d), lambda i: (i, 0)),
        out_shape=jax.ShapeDtypeStruct((rows, d), F32),
        compiler_params=_cparams("parallel"),
        name="combine_ln",
    )(x1, yg, yg, yg, yg, prob, mod3, ln_g, ln_b)


def _rank_kernel(idx_ref, rank_ref, cnt_ref, carry_ref):
    @pl.when(pl.program_id(0) == 0)
    def _():
        carry_ref[...] = jnp.zeros_like(carry_ref)

    idx = idx_ref[...]
    tb = idx.shape[0]
    lane = lax.broadcasted_iota(jnp.int32, idx.shape, 1)
    hot = [(lane == idx[:, k:k + 1]) for k in range(TOP_K)]
    tot = hot[0]
    for k in range(1, TOP_K):
        tot = jnp.logical_or(tot, hot[k])
    tot = jnp.where(tot, 1.0, 0.0)
    ti = lax.broadcasted_iota(jnp.int32, (tb, tb), 0)
    si = lax.broadcasted_iota(jnp.int32, (tb, tb), 1)
    before = jnp.where(ti > si, 1.0, 0.0).astype(BF16)
    prefix = jnp.dot(before, tot.astype(BF16), preferred_element_type=F32) + carry_ref[...]
    out = jnp.zeros(idx.shape, F32)
    for k in range(TOP_K):
        rk = jnp.sum(jnp.where(hot[k], prefix, 0.0), axis=-1, keepdims=True)
        out = jnp.where(lane == k, rk, out)
    rank_ref[...] = out.astype(jnp.int32)
    carry_ref[...] += jnp.sum(tot, axis=0, keepdims=True)
    cnt_ref[...] = carry_ref[...].astype(jnp.int32)


def route_rank(idx, *, tb):
    t = idx.shape[0]
    return pl.pallas_call(
        _rank_kernel,
        grid=(t // tb,),
        in_specs=[pl.BlockSpec((tb, LANES), lambda i: (i, 0))],
        out_specs=[pl.BlockSpec((tb, LANES), lambda i: (i, 0)), pl.BlockSpec((1, LANES), lambda i: (0, 0))],
        out_shape=[jax.ShapeDtypeStruct((t, LANES), jnp.int32), jax.ShapeDtypeStruct((1, LANES), jnp.int32)],
        scratch_shapes=[pltpu.VMEM((1, LANES), F32)],
        compiler_params=_cparams("arbitrary"),
        name="route_rank",
    )(idx)


def route_layout(top_idx, rank, counts, *, bm, n_rows):
    t = top_idx.shape[0]
    padded = (counts + bm - 1) // bm * bm
    pad_end = jnp.cumsum(padded)
    pad_start = pad_end - padded
    dest = (pad_start[top_idx] + rank).T.reshape(-1)
    row_tok = (jnp.arange(n_rows, dtype=jnp.int32) % t).at[dest].set(jnp.tile(jnp.arange(t, dtype=jnp.int32), TOP_K))
    nb = n_rows // bm
    blk_start = jnp.arange(nb, dtype=jnp.int32) * bm
    block_exp = jnp.sum((blk_start[:, None] >= pad_end[None, :]).astype(jnp.int32), axis=1)
    block_exp = jnp.minimum(block_exp, N_EXP - 1)
    n_valid = (pad_end[-1] // bm).astype(jnp.int32).reshape(1)
    return dest, row_tok, block_exp, n_valid


def grid_pos_embed(n_tok, d):
    rows = n_tok // GRID_W
    r = jnp.repeat(jnp.arange(rows, dtype=F32), GRID_W)
    col = jnp.tile(jnp.arange(GRID_W, dtype=F32), rows)
    quarter = d // 4
    omega = 1.0 / (10000.0 ** (jnp.arange(quarter, dtype=F32) / quarter))

    def emb(p):
        a = p[:, None] * omega[None, :]
        return jnp.concatenate([jnp.sin(a), jnp.cos(a)], axis=-1)

    return jnp.concatenate([emb(r), emb(col)], axis=-1)


def kernel(x_prompt, x_sample, state_hgrn, c, c_ctx, w_ada, b_ada, w_in, b_in, lb_logits, g_norm_a, conv_w, conv_b, conv_ln_g, conv_ln_b, w_br_a, w_br_b, w_br_c, b_br, w_gate, b_gate, w_o, b_o, ln1_g, ln1_b, w_router, b_router, w_gu, b_gu, w_down, b_down, ln2_g, ln2_b):
    depth = w_in.shape[0]
    bp, lp, d = x_prompt.shape
    bs, ls, _ = x_sample.shape
    tp, ts = bp * lp, bs * ls
    t = tp + ts
    alpha = (2 * depth) ** 0.25
    w_in_cols = w_in.shape[2]
    col_b = 3 * H_A * DK + 2 * W_A
    col_c = col_b + W_B

    tm, tn, tm_o, bm = TM, TN, TM_OUT, MOE_BM
    n_rows = (t * TOP_K // bm + N_EXP) * bm

    def row_of_block_for(rows_per_block):
        npb = tp // rows_per_block
        per = ls // rows_per_block
        return lambda i: jnp.where(i < npb, 0, 1 + (i - npb) // per)

    sm = jax.nn.softmax(lb_logits.astype(F32), axis=0)
    cs = jnp.cumsum(sm, axis=0)
    lower = cs - cs[0:1]

    cond = jnp.zeros((16, d), F32).at[0].set(c_ctx).at[1:1 + bs].set(c)
    mod = ada_mod(cond, w_ada, b_ada)

    xs = x_sample + grid_pos_embed(ls, d)[None]
    x = jnp.concatenate([x_prompt.reshape(tp, d), xs.reshape(ts, d)], axis=0)

    b_gu4 = b_gu[:, :, None, :]
    b_down4 = b_down[:, :, None, :]
    dl_p, cm = dft_tables(lp)
    dl_s, _ = dft_tables(ls)

    states = []
    for l in range(depth):
        mod3 = mod[l].reshape(16, 1, 6 * d)
        u = ln_mod_matmul(x, mod3, row_of_block_for(TM_IN), w_in[l].astype(BF16), b_in[l][None, :],
                          w_gate[l].astype(BF16), b_gate[l][None, :], tm=TM_IN, tn=tn)

        gn = g_norm_a[l][None, :]
        ya_p, s_ctx = hgrn_mixer(u, lower[l], gn, None, row0=0, batch=bp, seq=lp, want_state=True)
        (ya_s,) = hgrn_mixer(u, lower[l], gn, state_hgrn[:, l], row0=tp, batch=bs, seq=ls, want_state=False)
        states.append(s_ctx)
        yb_p = fourier_mixer(u, dl_p, cm, row0=0, col0=col_b, batch=bp, seq=lp)
        yb_s = fourier_mixer(u, dl_s, cm, row0=tp, col0=col_b, batch=bs, seq=ls)
        cw = (conv_w[l], conv_b[l][None, :], conv_ln_g[l][None, :], conv_ln_b[l][None, :])
        yc_p = conv_mixer(u, *cw, row0=0, col0=col_c, batch=bp, seq=lp)
        yc_s = conv_mixer(u, *cw, row0=tp, col0=col_c, batch=bs, seq=ls)
        merged = branch_merge((ya_p, yb_p, yc_p), (ya_s, yb_s, yc_s), u, w_br_a[l].astype(BF16),
                              w_br_b[l].astype(BF16), w_br_c[l].astype(BF16), b_br[l],
                              gate_col0=w_in_cols, d=d, tm=tm, tn=tn)

        wr = jnp.zeros((d, LANES), BF16).at[:, :N_EXP].set(w_router[l].astype(BF16))
        br = jnp.full((1, LANES), -1e30, F32).at[0, :N_EXP].set(b_router[l])
        x1, h2, idx, prob = out_router(x, merged, mod3, row_of_block_for(tm_o), w_o[l].astype(BF16),
                                       b_o[l][None, :], ln1_g[l][None, :], ln1_b[l][None, :], wr, br,
                                       alpha=alpha, tm=tm_o)

        rank, counts = route_rank(idx, tb=RANK_ROWS)
        dest, row_tok, block_exp, n_valid = route_layout(idx[:, :TOP_K], rank[:, :TOP_K], counts[0, :N_EXP],
                                                         bm=bm, n_rows=n_rows)
        xsort = sc_gather(h2, row_tok)
        act = moe_up(xsort, block_exp, n_valid, w_gu, b_gu4, l, bm=bm, tn=TN_UP)
        yrow = moe_down(act, block_exp, n_valid, w_down, b_down4, l, bm=bm, tn=TN_DOWN)
        yg = sc_gather(yrow, dest)
        combine = functools.partial(combine_ln, x1, yg, prob, mod3, row_of_block_for(tm_o), ln2_g[l][None, :],
                                    ln2_b[l][None, :], alpha=alpha, tm=tm_o, group=TN_DOWN // 2)
        if l + 1 < depth:
            x = combine()

    y_prompt = combine(row0=0, rows=tp).reshape(bp, lp, d)
    y_sample = combine(row0=tp, rows=ts).reshape(bs, ls, d)
    return y_prompt, y_sample, jnp.stack(states, axis=1)
```

```python
import functools
import math

import jax
import jax.numpy as jnp
from jax import lax
from jax.experimental import pallas as pl
from jax.experimental.pallas import tpu as pltpu
from jax.experimental.pallas import tpu_sc as plsc

F32 = jnp.float32
BF16 = jnp.bfloat16

GRID_W = 64
H_A = 8
DK = 128
DV = 128
W_A = H_A * DV
W_B = 512
W_C = 512
CONV_K = 31
N_BRANCH = 3
N_EXP = 32
TOP_K = 4
SWIGLU_LIMIT = 7.0
SWIGLU_ALPHA = 1.702
LN_EPS = 1e-5
RMS_EPS = 1e-6

LANES = 128
SUBLANES = 8
VMEM_LIMIT = 56 * 1024 * 1024
HGRN_CHUNK = 16
HGRN_UNROLL = 16
CONV_PAD = 16
CONV_ROWS = 32
LOGF_FLOOR = -1e4
LOG2E = 1.4426950408889634
TM = 1024
TM_IN = 2048
TN = 512
TM_OUT = 512
MOE_BM = 512
TN_UP = 1024
TN_DOWN = 2048
RANK_ROWS = 512
SC_GATHER_ROWS = 32


def _cparams(*sem):
    return pltpu.CompilerParams(dimension_semantics=sem, vmem_limit_bytes=VMEM_LIMIT)


def _ln_rows(x):
    mu = jnp.mean(x, axis=-1, keepdims=True)
    xc = x - mu
    var = jnp.mean(xc * xc, axis=-1, keepdims=True)
    return xc * lax.rsqrt(var + LN_EPS)


def _sigmoid(x):
    return 1.0 / (1.0 + jnp.exp(-x))


def _bf16_bits(x):
    u = lax.bitcast_convert_type(x, jnp.uint32)
    r = u + jnp.uint32(0x7FFF) + ((u >> 16) & jnp.uint32(1))
    return r & jnp.uint32(0xFFFF0000)


def _pack_halves(x):
    n = x.shape[1] // 2
    return (_bf16_bits(x[:, :n]) >> 16) | _bf16_bits(x[:, n:])


def _unpack_halves(w):
    lo = lax.bitcast_convert_type(w << 16, F32)
    hi = lax.bitcast_convert_type(w & jnp.uint32(0xFFFF0000), F32)
    return lo, hi


def _ada_kernel(c_ref, w_ref, b_ref, o_ref):
    c = c_ref[...]
    a = (c * _sigmoid(c)).astype(BF16)
    o_ref[...] = jnp.dot(a, w_ref[...].astype(BF16), preferred_element_type=F32) + b_ref[...]


def ada_mod(cond, w_ada, b_ada, *, tn=1024):
    depth, d, n = w_ada.shape
    r = cond.shape[0]
    return pl.pallas_call(
        _ada_kernel,
        grid=(depth, n // tn),
        in_specs=[
            pl.BlockSpec((r, d), lambda l, j: (0, 0)),
            pl.BlockSpec((None, d, tn), lambda l, j: (l, 0, j)),
            pl.BlockSpec((None, 1, tn), lambda l, j: (l, 0, j)),
        ],
        out_specs=pl.BlockSpec((None, r, tn), lambda l, j: (l, 0, j)),
        out_shape=jax.ShapeDtypeStruct((depth, r, n), F32),
        compiler_params=_cparams("parallel", "parallel"),
        name="ada_mod",
    )(cond, w_ada, b_ada.reshape(depth, 1, n))


def _lnmm_kernel(x_ref, sh_ref, sc_ref, wi_ref, bi_ref, wg_ref, bg_ref, o_ref, h_ref, *, n_plain):
    j = pl.program_id(1)

    @pl.when(j == 0)
    def _():
        y = _ln_rows(x_ref[...])
        h_ref[...] = (y * (1.0 + sc_ref[...]) + sh_ref[...]).astype(BF16)

    def project(w_ref, b_ref):
        return jnp.dot(h_ref[...], w_ref[...], preferred_element_type=F32) + b_ref[...]

    @pl.when(j < n_plain)
    def _():
        o_ref[...] = project(wi_ref, bi_ref).astype(BF16)

    @pl.when(j >= n_plain)
    def _():
        o_ref[...] = _sigmoid(project(wg_ref, bg_ref)).astype(BF16)


def ln_mod_matmul(x, mod3, row_of_block, w_in, b_in, w_gate, b_gate, *, tm, tn):
    t, d = x.shape
    n1 = w_in.shape[1] // tn
    n2 = w_gate.shape[1] // tn

    def first(j):
        return jnp.minimum(j, n1 - 1)

    def second(j):
        return jnp.maximum(j - n1, 0)

    return pl.pallas_call(
        functools.partial(_lnmm_kernel, n_plain=n1),
        grid=(t // tm, n1 + n2),
        in_specs=[
            pl.BlockSpec((tm, d), lambda i, j: (i, 0), pipeline_mode=pl.Buffered(1)),
            pl.BlockSpec((None, 1, d), lambda i, j: (row_of_block(i), 0, 0)),
            pl.BlockSpec((None, 1, d), lambda i, j: (row_of_block(i), 0, 1)),
            pl.BlockSpec((d, tn), lambda i, j: (0, first(j))),
            pl.BlockSpec((1, tn), lambda i, j: (0, first(j))),
            pl.BlockSpec((d, tn), lambda i, j: (0, second(j))),
            pl.BlockSpec((1, tn), lambda i, j: (0, second(j))),
        ],
        out_specs=pl.BlockSpec((tm, tn), lambda i, j: (i, j)),
        out_shape=jax.ShapeDtypeStruct((t, (n1 + n2) * tn), BF16),
        scratch_shapes=[pltpu.VMEM((tm, d), BF16)],
        compiler_params=_cparams("parallel", "arbitrary"),
        name="ln_mod_matmul",
    )(x, mod3, mod3, w_in, b_in, w_gate, b_gate)


def _chunk_scan(x_h, direction):
    half = SUBLANES
    rid = lax.broadcasted_iota(jnp.int32, x_h[0].shape, 0)
    out = []
    for x in x_h:
        step = 1
        while step < half:
            if direction == 0:
                x = x + jnp.where(rid >= step, pltpu.roll(x, step, axis=0), 0.0)
            else:
                x = x + jnp.where(rid < half - step, pltpu.roll(x, half - step, axis=0), 0.0)
            step *= 2
        out.append(x)
    if direction == 0:
        out[1] = out[1] + out[0][half - 1:half, :]
    else:
        out[0] = out[0] + out[1][0:1, :]
    return out


def _hgrn_chunk(q_ref, z_ref, v_ref, lb, oml, st_ref, o_ref, rows_ref, c0, *, direction):
    half = SUBLANES
    rows = pl.ds(c0, HGRN_CHUNK)
    q = q_ref[rows, :].astype(F32) * (DK ** -0.5)
    z = z_ref[rows, :].astype(F32)
    v = v_ref[rows, :]
    e = jnp.exp2(jnp.abs(z) * (-LOG2E))
    r = 1.0 / (1.0 + e)
    er = e * r
    pos = z >= 0
    lf2 = jnp.maximum(jnp.log2(lb + oml * jnp.where(pos, r, er)), LOGF_FLOOR)
    lk2 = jnp.log2(oml * jnp.where(pos, er, r))

    b_h = _chunk_scan([lf2[0:half, :], lf2[half:HGRN_CHUNK, :]], direction)
    b2 = jnp.concatenate(b_h, axis=0)
    b_end = b_h[1][half - 1:half, :] if direction == 0 else b_h[0][0:1, :]

    st = st_ref[direction]
    qe = (q * jnp.exp2(b2)).astype(BF16)
    o_inter = lax.dot_general(qe, st.astype(BF16), (((1,), (1,)), ((), ())), preferred_element_type=F32)

    rows_ref[0] = b2 - lk2
    rows_ref[1] = v.astype(F32)
    q_h = (q[0:half, :], q[half:HGRN_CHUNK, :])
    acc = [jnp.zeros((half, DV), F32), jnp.zeros((half, DV), F32)]
    rid = lax.broadcasted_iota(jnp.int32, (half, DK), 0)
    for s in range(HGRN_CHUNK):
        hs, rs = divmod(s, half)
        bs = jnp.broadcast_to(rows_ref[0, s:s + 1, :], (half, DK))
        vs = jnp.broadcast_to(rows_ref[1, s:s + 1, :], (half, DV))
        for ht in range(2):
            if (ht < hs) if direction == 0 else (ht > hs):
                continue
            d = b_h[ht] - bs
            if ht == hs:
                keep = (rid >= rs) if direction == 0 else (rid <= rs)
                d = jnp.where(keep, d, -1e30)
            w = jnp.sum(q_h[ht] * jnp.exp2(d), axis=-1, keepdims=True)
            acc[ht] = acc[ht] + w * vs
    o_ref[pl.ds(c0, half), :] = acc[0] + o_inter[0:half, :]
    o_ref[pl.ds(c0 + half, half), :] = acc[1] + o_inter[half:HGRN_CHUNK, :]

    kk = jnp.exp2(lk2 + (b_end - b2)).astype(BF16)
    upd = lax.dot_general(v, kk, (((0,), (0,)), ((), ())), preferred_element_type=F32)
    st_ref[direction] = st * jnp.exp2(b_end) + upd


def _hgrn_kernel(*refs, seq, unroll, has_s0, want_state):
    q_ref, zf_ref, zb_ref, v_ref, g_ref, lb_ref, gn_ref = refs[:7]
    pos = 7
    s0_ref = None
    if has_s0:
        s0_ref = refs[pos]
        pos += 1
    y_ref = refs[pos]
    pos += 1
    sfin_ref = None
    if want_state:
        sfin_ref = refs[pos]
        pos += 1
    of_ref, ob_ref, st_ref, rows_ref = refs[pos:pos + 4]

    for d in range(2):
        st_ref[d] = s0_ref[d].T if has_s0 else jnp.zeros((DV, DK), F32)

    n = seq // HGRN_CHUNK
    lb_f = lb_ref[0:1, :]
    lb_b = lb_ref[1:2, :]
    oml_f = 1.0 - lb_f
    oml_b = 1.0 - lb_b

    def body(i, carry):
        for j in range(unroll):
            cf = pl.multiple_of((i * unroll + j) * HGRN_CHUNK, HGRN_CHUNK)
            cb = pl.multiple_of((n - 1 - (i * unroll + j)) * HGRN_CHUNK, HGRN_CHUNK)
            _hgrn_chunk(q_ref, zf_ref, v_ref, lb_f, oml_f, st_ref, of_ref, rows_ref.at[2 * j], cf, direction=0)
            _hgrn_chunk(q_ref, zb_ref, v_ref, lb_b, oml_b, st_ref, ob_ref, rows_ref.at[2 * j + 1], cb, direction=1)
        return carry

    lax.fori_loop(0, n // unroll, body, 0)

    blk = min(seq, 256)
    for r0 in range(0, seq, blk):
        o = of_ref[r0:r0 + blk, :] + ob_ref[r0:r0 + blk, :]
        g = g_ref[r0:r0 + blk, :].astype(F32)
        o = o * lax.rsqrt(jnp.mean(o * o, axis=-1, keepdims=True) + RMS_EPS) * gn_ref[...]
        y_ref[r0:r0 + blk, :] = (o * (g * _sigmoid(g))).astype(BF16)
    if want_state:
        sfin_ref[0] = st_ref[0].T
        sfin_ref[1] = st_ref[1].T


def hgrn_mixer(u, lb, g_norm, s0, *, row0, batch, seq, want_state, unroll=HGRN_UNROLL):
    rb0 = row0 // seq
    has_s0 = s0 is not None
    unroll = math.gcd(seq // HGRN_CHUNK, unroll)

    def sec(k):
        return pl.BlockSpec((seq, LANES), lambda b, h: (rb0 + b, k * H_A + h))

    in_specs = [sec(0), sec(1), sec(2), sec(3), sec(4),
                pl.BlockSpec((2, DK), lambda b, h: (0, h)),
                pl.BlockSpec((1, DV), lambda b, h: (0, 0))]
    args = [u, u, u, u, u, lb, g_norm]
    state_spec = pl.BlockSpec((None, 2, None, DK, DV), lambda b, h: (b, 0, h, 0, 0))
    if has_s0:
        in_specs.append(state_spec)
        args.append(s0)
    out_specs = [pl.BlockSpec((seq, DV), lambda b, h: (b, h))]
    out_shape = [jax.ShapeDtypeStruct((batch * seq, W_A), BF16)]
    if want_state:
        out_specs.append(state_spec)
        out_shape.append(jax.ShapeDtypeStruct((batch, 2, H_A, DK, DV), F32))
    return pl.pallas_call(
        functools.partial(_hgrn_kernel, seq=seq, unroll=unroll, has_s0=has_s0, want_state=want_state),
        grid=(batch, H_A),
        in_specs=in_specs,
        out_specs=out_specs,
        out_shape=out_shape,
        scratch_shapes=[pltpu.VMEM((seq, DV), F32), pltpu.VMEM((seq, DV), F32), pltpu.VMEM((2, DV, DK), F32),
                        pltpu.VMEM((2 * unroll, 2, HGRN_CHUNK, DK), F32)],
        compiler_params=_cparams("parallel", "parallel"),
        name="hgrn_mixer",
    )(*args)


def _fourier_kernel(u_ref, cm_ref, dl_ref, o_ref, ab_ref, *, seq, rows, scale):
    @pl.when(pl.program_id(1) == 0)
    def _():
        for r0 in range(0, seq, rows):
            ab = jnp.dot(u_ref[r0:r0 + rows, :], cm_ref[...], preferred_element_type=F32)
            ab_ref[r0:r0 + rows, :] = ab[:, :W_B].astype(BF16)
            ab_ref[seq + r0:seq + r0 + rows, :] = ab[:, W_B:].astype(BF16)

    o_ref[...] = (jnp.dot(dl_ref[...], ab_ref[...], preferred_element_type=F32) * scale).astype(BF16)


def dft_tables(seq):
    def cs(n):
        i = lax.iota(jnp.int32, n)
        ang = ((i[:, None] * i[None, :]) % n).astype(F32) * (2.0 * math.pi / n)
        return jnp.cos(ang), jnp.sin(ang)
    cl, sl = cs(seq)
    cc, sc = cs(W_B)
    return jnp.concatenate([cl, -sl], axis=1).astype(BF16), jnp.concatenate([cc, sc], axis=1).astype(BF16)


def fourier_mixer(u, dl, cm, *, row0, col0, batch, seq, rows=256):
    rows = min(rows, seq)
    rb0 = row0 // seq
    cb = col0 // W_B
    nt = seq // rows
    return pl.pallas_call(
        functools.partial(_fourier_kernel, seq=seq, rows=rows, scale=1.0 / math.sqrt(seq * W_B)),
        grid=(batch, nt),
        in_specs=[
            pl.BlockSpec((seq, W_B), lambda b, i: (rb0 + b, cb)),
            pl.BlockSpec((W_B, 2 * W_B), lambda b, i: (0, 0)),
            pl.BlockSpec((rows, 2 * seq), lambda b, i: (i, 0)),
        ],
        out_specs=pl.BlockSpec((rows, W_B), lambda b, i: (b * nt + i, 0)),
        out_shape=jax.ShapeDtypeStruct((batch * seq, W_B), BF16),
        scratch_shapes=[pltpu.VMEM((2 * seq, W_B), BF16)],
        compiler_params=_cparams("parallel", "arbitrary"),
        name="fourier_mixer",
    )(u, cm, dl)


def _conv_kernel(a_ref, gt_ref, w_ref, b_ref, g_ref, be_ref, o_ref, z_ref, sh_ref, *, seq):
    zeros = jnp.zeros((CONV_PAD, W_C), F32)
    z_ref[0:CONV_PAD, :] = zeros
    z_ref[CONV_PAD + seq:2 * CONV_PAD + seq, :] = zeros
    blk = min(seq, 256)
    for r0 in range(0, seq, blk):
        a = a_ref[r0:r0 + blk, :].astype(F32)
        gt = gt_ref[r0:r0 + blk, :].astype(F32)
        z_ref[CONV_PAD + r0:CONV_PAD + r0 + blk, :] = a * _sigmoid(gt)

    off = CONV_PAD - CONV_K // 2

    def tile(i, carry):
        r0 = pl.multiple_of(i * CONV_ROWS, CONV_ROWS)
        win = z_ref[pl.ds(r0, CONV_ROWS + 2 * CONV_PAD), :]
        span = CONV_ROWS + 2 * CONV_PAD - SUBLANES
        for r in range(SUBLANES):
            sh_ref[r] = win[r:r + span, :]
        acc = jnp.zeros((CONV_ROWS, W_C), F32)
        for k in range(CONV_K):
            m, r = divmod(off + k, SUBLANES)
            acc = acc + sh_ref[r, m * SUBLANES:m * SUBLANES + CONV_ROWS, :] * w_ref[k:k + 1, :]
        y = _ln_rows(acc + b_ref[...]) * g_ref[...] + be_ref[...]
        o_ref[pl.ds(r0, CONV_ROWS), :] = (y * _sigmoid(y)).astype(BF16)
        return carry

    lax.fori_loop(0, seq // CONV_ROWS, tile, 0)


def conv_mixer(u, conv_w, conv_b, ln_g, ln_b, *, row0, col0, batch, seq):
    rb0 = row0 // seq
    cb = col0 // W_C
    vec = pl.BlockSpec((1, W_C), lambda b: (0, 0))
    return pl.pallas_call(
        functools.partial(_conv_kernel, seq=seq),
        grid=(batch,),
        in_specs=[
            pl.BlockSpec((seq, W_C), lambda b: (rb0 + b, cb)),
            pl.BlockSpec((seq, W_C), lambda b: (rb0 + b, cb + 1)),
            pl.BlockSpec((CONV_K, W_C), lambda b: (0, 0)),
            vec, vec, vec,
        ],
        out_specs=pl.BlockSpec((seq, W_C), lambda b: (b, 0)),
        out_shape=jax.ShapeDtypeStruct((batch * seq, W_C), BF16),
        scratch_shapes=[pltpu.VMEM((seq + 2 * CONV_PAD, W_C), F32),
                        pltpu.VMEM((SUBLANES, CONV_ROWS + 2 * CONV_PAD - SUBLANES, W_C), F32)],
        compiler_params=_cparams("parallel"),
        name="conv_mixer",
    )(u, u, conv_w, conv_b, ln_g, ln_b)


def _branch_kernel(yap_ref, ybp_ref, ycp_ref, yas_ref, ybs_ref, ycs_ref, g0_ref, g1_ref, g2_ref,
                   wa_ref, wb_ref, wc_ref, b_ref, o_ref, *, n_first):
    def merge(ya_ref, yb_ref, yc_ref):
        def proj(y_ref, w_ref, k):
            return jnp.dot(y_ref[...], w_ref[...], preferred_element_type=F32) + b_ref[k:k + 1, :]
        m = g0_ref[...].astype(F32) * proj(ya_ref, wa_ref, 0)
        m = m + g1_ref[...].astype(F32) * proj(yb_ref, wb_ref, 1)
        m = m + g2_ref[...].astype(F32) * proj(yc_ref, wc_ref, 2)
        o_ref[...] = m.astype(BF16)

    first = pl.program_id(0) < n_first

    @pl.when(first)
    def _():
        merge(yap_ref, ybp_ref, ycp_ref)

    @pl.when(jnp.logical_not(first))
    def _():
        merge(yas_ref, ybs_ref, ycs_ref)


def branch_merge(y_first, y_second, u, wa, wb, wc, b_br, *, gate_col0, d, tm, tn):
    t1 = y_first[0].shape[0]
    t2 = y_second[0].shape[0]
    n1 = t1 // tm
    n2 = t2 // tm
    gb = gate_col0 // tn
    nd = d // tn

    def gate(k):
        return pl.BlockSpec((tm, tn), lambda i, j: (i, gb + k * nd + j))

    def first(y):
        return pl.BlockSpec((tm, y.shape[1]), lambda i, j: (jnp.minimum(i, n1 - 1), 0))

    def second(y):
        return pl.BlockSpec((tm, y.shape[1]), lambda i, j: (jnp.maximum(i - n1, 0), 0))

    return pl.pallas_call(
        functools.partial(_branch_kernel, n_first=n1),
        grid=(n1 + n2, nd),
        in_specs=[
            first(y_first[0]), first(y_first[1]), first(y_first[2]),
            second(y_second[0]), second(y_second[1]), second(y_second[2]),
            gate(0), gate(1), gate(2),
            pl.BlockSpec((wa.shape[0], tn), lambda i, j: (0, j)),
            pl.BlockSpec((wb.shape[0], tn), lambda i, j: (0, j)),
            pl.BlockSpec((wc.shape[0], tn), lambda i, j: (0, j)),
            pl.BlockSpec((N_BRANCH, tn), lambda i, j: (0, j)),
        ],
        out_specs=pl.BlockSpec((tm, tn), lambda i, j: (i, j)),
        out_shape=jax.ShapeDtypeStruct((t1 + t2, d), BF16),
        compiler_params=_cparams("parallel", "arbitrary"),
        name="branch_merge",
    )(*y_first, *y_second, u, u, u, wa, wb, wc, b_br)


def _out_router_kernel(x_ref, m_ref, g1_ref, sh2_ref, sc2_ref, wo_ref, bo_ref, lg_ref, lb_ref,
                       wr_ref, br_ref, x1_ref, h2_ref, idx_ref, p_ref, *, alpha):
    mix = jnp.dot(m_ref[...], wo_ref[...], preferred_element_type=F32) + bo_ref[...]
    x1 = _ln_rows(alpha * x_ref[...] + g1_ref[...] * mix) * lg_ref[...] + lb_ref[...]
    x1_ref[...] = x1
    h2 = _ln_rows(x1) * (1.0 + sc2_ref[...]) + sh2_ref[...]
    h2_ref[...] = _pack_halves(h2)
    logits = jnp.dot(h2.astype(BF16), wr_ref[...], preferred_element_type=F32) + br_ref[...]
    lane = lax.broadcasted_iota(jnp.int32, logits.shape, 1)
    idx_out = jnp.zeros(logits.shape, jnp.int32)
    p_out = jnp.zeros(logits.shape, F32)
    top = None
    denom = None
    for k in range(TOP_K):
        m = jnp.max(logits, axis=-1, keepdims=True)
        i = jnp.min(jnp.where(logits == m, lane, LANES), axis=-1, keepdims=True)
        if k == 0:
            top = m
        ek = jnp.exp(m - top)
        denom = ek if k == 0 else denom + ek
        idx_out = jnp.where(lane == k, i, idx_out)
        p_out = jnp.where(lane == k, ek, p_out)
        logits = jnp.where(lane == i, -jnp.inf, logits)
    idx_ref[...] = idx_out
    p_ref[...] = p_out / denom


def out_router(x, merged, mod3, row_of_block, wo, bo, ln_g, ln_b, wr, br, *, alpha, tm):
    t, d = x.shape

    def modspec(k):
        return pl.BlockSpec((None, 1, d), lambda i: (row_of_block(i), 0, k))

    vec = pl.BlockSpec((1, d), lambda i: (0, 0))
    row_f = pl.BlockSpec((tm, d), lambda i: (i, 0))
    row_s = pl.BlockSpec((tm, LANES), lambda i: (i, 0))
    return pl.pallas_call(
        functools.partial(_out_router_kernel, alpha=alpha),
        grid=(t // tm,),
        in_specs=[row_f, row_f, modspec(2), modspec(3), modspec(4),
                  pl.BlockSpec((d, d), lambda i: (0, 0)), vec, vec, vec,
                  pl.BlockSpec((d, LANES), lambda i: (0, 0)),
                  pl.BlockSpec((1, LANES), lambda i: (0, 0))],
        out_specs=[row_f, pl.BlockSpec((tm, d // 2), lambda i: (i, 0)), row_s, row_s],
        out_shape=[jax.ShapeDtypeStruct((t, d), F32), jax.ShapeDtypeStruct((t, d // 2), jnp.uint32),
                   jax.ShapeDtypeStruct((t, LANES), jnp.int32), jax.ShapeDtypeStruct((t, LANES), F32)],
        compiler_params=_cparams("parallel"),
        name="out_router",
    )(x, merged, mod3, mod3, mod3, wo, bo, ln_g, ln_b, wr, br)


def _moe_up_kernel(be_ref, nv_ref, x_ref, wg_ref, wu_ref, bg_ref, bu_ref, o_ref, wgs_ref, wus_ref):
    b = pl.program_id(1)
    fresh = jnp.logical_or(b == 0, be_ref[b] != be_ref[jnp.maximum(b - 1, 0)])

    @pl.when(fresh)
    def _():
        wgs_ref[...] = wg_ref[...].astype(BF16)
        wus_ref[...] = wu_ref[...].astype(BF16)

    @pl.when(b < nv_ref[0])
    def _():
        lo, hi = _unpack_halves(x_ref[...])
        lo = lo.astype(BF16)
        hi = hi.astype(BF16)
        kh = lo.shape[1]

        def proj(w_ref, b_ref):
            return (jnp.dot(lo, w_ref[0:kh, :], preferred_element_type=F32)
                    + jnp.dot(hi, w_ref[kh:2 * kh, :], preferred_element_type=F32) + b_ref[...])

        gt = proj(wgs_ref, bg_ref)
        up = proj(wus_ref, bu_ref)
        gt = jnp.minimum(gt, SWIGLU_LIMIT)
        up = jnp.clip(up, -SWIGLU_LIMIT, SWIGLU_LIMIT)
        o_ref[...] = ((up + 1.0) * (gt * _sigmoid(SWIGLU_ALPHA * gt))).astype(BF16)

    @pl.when(b >= nv_ref[0])
    def _():
        o_ref[...] = jnp.zeros_like(o_ref)


def moe_up(xs, block_exp, n_valid, w_gu, b_gu, layer, *, bm, tn):
    r = xs.shape[0]
    d = w_gu.shape[2]
    f = w_gu.shape[3] // 2
    nf = f // tn
    grid_spec = pltpu.PrefetchScalarGridSpec(
        num_scalar_prefetch=2,
        grid=(nf, r // bm),
        in_specs=[
            pl.BlockSpec((bm, d // 2), lambda n, b, be, nv: (b, 0)),
            pl.BlockSpec((None, None, d, tn), lambda n, b, be, nv: (layer, be[b], 0, n)),
            pl.BlockSpec((None, None, d, tn), lambda n, b, be, nv: (layer, be[b], 0, nf + n)),
            pl.BlockSpec((None, None, 1, tn), lambda n, b, be, nv: (layer, be[b], 0, n)),
            pl.BlockSpec((None, None, 1, tn), lambda n, b, be, nv: (layer, be[b], 0, nf + n)),
        ],
        out_specs=pl.BlockSpec((bm, tn), lambda n, b, be, nv: (b, n)),
        scratch_shapes=[pltpu.VMEM((d, tn), BF16), pltpu.VMEM((d, tn), BF16)],
    )
    return pl.pallas_call(
        _moe_up_kernel,
        grid_spec=grid_spec,
        out_shape=jax.ShapeDtypeStruct((r, f), BF16),
        compiler_params=_cparams("arbitrary", "arbitrary"),
        name="moe_up",
    )(block_exp, n_valid, xs, w_gu, w_gu, b_gu, b_gu)


def _moe_down_kernel(be_ref, nv_ref, a_ref, w_ref, b_ref, o_ref, ws_ref):
    b = pl.program_id(1)
    fresh = jnp.logical_or(b == 0, be_ref[b] != be_ref[jnp.maximum(b - 1, 0)])

    @pl.when(fresh)
    def _():
        ws_ref[...] = w_ref[...].astype(BF16)

    @pl.when(b < nv_ref[0])
    def _():
        y = jnp.dot(a_ref[...], ws_ref[...], preferred_element_type=F32) + b_ref[...]
        o_ref[...] = _pack_halves(y)

    @pl.when(b >= nv_ref[0])
    def _():
        o_ref[...] = jnp.zeros_like(o_ref)


def moe_down(act, block_exp, n_valid, w_down, b_down, layer, *, bm, tn):
    r, f = act.shape
    d = w_down.shape[3]
    grid_spec = pltpu.PrefetchScalarGridSpec(
        num_scalar_prefetch=2,
        grid=(d // tn, r // bm),
        in_specs=[
            pl.BlockSpec((bm, f), lambda n, b, be, nv: (b, 0)),
            pl.BlockSpec((None, None, f, tn), lambda n, b, be, nv: (layer, be[b], 0, n)),
            pl.BlockSpec((None, None, 1, tn), lambda n, b, be, nv: (layer, be[b], 0, n)),
        ],
        out_specs=pl.BlockSpec((bm, tn // 2), lambda n, b, be, nv: (b, n)),
        scratch_shapes=[pltpu.VMEM((f, tn), BF16)],
    )
    return pl.pallas_call(
        _moe_down_kernel,
        grid_spec=grid_spec,
        out_shape=jax.ShapeDtypeStruct((r, d // 2), jnp.uint32),
        compiler_params=_cparams("arbitrary", "arbitrary"),
        name="moe_down",
    )(block_exp, n_valid, act, w_down, b_down)


def sc_gather(x, idx):
    n = idx.shape[0]
    d = x.shape[1]
    win = SC_GATHER_ROWS
    mesh = plsc.VectorSubcoreMesh(core_axis_name="core", subcore_axis_name="subcore")

    @pl.kernel(out_type=jax.ShapeDtypeStruct((n, d), x.dtype), mesh=mesh)
    def gather_kernel(x_hbm, i_hbm, o_hbm):
        def body(i_vmem, o_vmem):
            pltpu.sync_copy(x_hbm.at[i_vmem.at[0]], o_vmem)

        pltpu.emit_pipeline(
            body,
            grid=(n // win,),
            in_specs=[pl.BlockSpec((1, win), index_map=lambda i: (i, 0))],
            out_specs=[pl.BlockSpec((win, d), index_map=lambda i: (i, 0))],
            core_axis_name=("core", "subcore"),
            dimension_semantics=(pltpu.PARALLEL,),
        )(i_hbm, o_hbm)

    return gather_kernel(x, idx.reshape(n // win, win))


def _combine_kernel(x_ref, y0_ref, y1_ref, y2_ref, y3_ref, p_ref, g2_ref, lg_ref, lb_ref, o_ref, *, alpha, group):
    p = p_ref[...]
    ffn = None
    for k, y_ref in enumerate((y0_ref, y1_ref, y2_ref, y3_ref)):
        w = y_ref[...]
        cols = []
        for c0 in range(0, w.shape[1], group):
            cols.extend(_unpack_halves(w[:, c0:c0 + group]))
        term = p[:, k:k + 1] * jnp.concatenate(cols, axis=1)
        ffn = term if ffn is None else ffn + term
    o_ref[...] = _ln_rows(alpha * x_ref[...] + g2_ref[...] * ffn) * lg_ref[...] + lb_ref[...]


def combine_ln(x1, yg, prob, mod3, row_of_block, ln_g, ln_b, *, alpha, tm, group, row0=0, rows=None):
    t, d = x1.shape
    rows = t if rows is None else rows
    nb = t // tm
    b0 = row0 // tm
    vec = pl.BlockSpec((1, d), lambda i: (0, 0))

    def sel(k):
        return pl.BlockSpec((tm, d // 2), lambda i: (k * nb + b0 + i, 0))

    return pl.pallas_call(
        functools.partial(_combine_kernel, alpha=alpha, group=group),
        grid=(rows // tm,),
        in_specs=[pl.BlockSpec((tm, d), lambda i: (b0 + i, 0)), sel(0), sel(1), sel(2), sel(3),
                  pl.BlockSpec((tm, LANES), lambda i: (b0 + i, 0)),
                  pl.BlockSpec((None, 1, d), lambda i: (row_of_block(b0 + i), 0, 5)),
                  vec, vec],
        out_specs=pl.BlockSpec((tm, d), lambda i: (i, 0)),
        out_shape=jax.ShapeDtypeStruct((rows, d), F32),
        compiler_params=_cparams("parallel"),
        name="combine_ln",
    )(x1, yg, yg, yg, yg, prob, mod3, ln_g, ln_b)


def _rank_kernel(idx_ref, rank_ref, cnt_ref, carry_ref):
    @pl.when(pl.program_id(0) == 0)
    def _():
        carry_ref[...] = jnp.zeros_like(carry_ref)

    idx = idx_ref[...]
    tb = idx.shape[0]
    lane = lax.broadcasted_iota(jnp.int32, idx.shape, 1)
    hot = [(lane == idx[:, k:k + 1]) for k in range(TOP_K)]
    tot = hot[0]
    for k in range(1, TOP_K):
        tot = jnp.logical_or(tot, hot[k])
    tot = jnp.where(tot, 1.0, 0.0)
    ti = lax.broadcasted_iota(jnp.int32, (tb, tb), 0)
    si = lax.broadcasted_iota(jnp.int32, (tb, tb), 1)
    before = jnp.where(ti > si, 1.0, 0.0).astype(BF16)
    prefix = jnp.dot(before, tot.astype(BF16), preferred_element_type=F32) + carry_ref[...]
    out = jnp.zeros(idx.shape, F32)
    for k in range(TOP_K):
        rk = jnp.sum(jnp.where(hot[k], prefix, 0.0), axis=-1, keepdims=True)
        out = jnp.where(lane == k, rk, out)
    rank_ref[...] = out.astype(jnp.int32)
    carry_ref[...] += jnp.sum(tot, axis=0, keepdims=True)
    cnt_ref[...] = carry_ref[...].astype(jnp.int32)


def route_rank(idx, *, tb):
    t = idx.shape[0]
    return pl.pallas_call(
        _rank_kernel,
        grid=(t // tb,),
        in_specs=[pl.BlockSpec((tb, LANES), lambda i: (i, 0))],
        out_specs=[pl.BlockSpec((tb, LANES), lambda i: (i, 0)), pl.BlockSpec((1, LANES), lambda i: (0, 0))],
        out_shape=[jax.ShapeDtypeStruct((t, LANES), jnp.int32), jax.ShapeDtypeStruct((1, LANES), jnp.int32)],
        scratch_shapes=[pltpu.VMEM((1, LANES), F32)],
        compiler_params=_cparams("arbitrary"),
        name="route_rank",
    )(idx)


def route_layout(top_idx, rank, counts, *, bm, n_rows):
    t = top_idx.shape[0]
    padded = (counts + bm - 1) // bm * bm
    pad_end = jnp.cumsum(padded)
    pad_start = pad_end - padded
    dest = (pad_start[top_idx] + rank).T.reshape(-1)
    row_tok = (jnp.arange(n_rows, dtype=jnp.int32) % t).at[dest].set(
        jnp.tile(jnp.arange(t, dtype=jnp.int32), TOP_K), unique_indices=True, mode="promise_in_bounds")
    nb = n_rows // bm
    blk_start = jnp.arange(nb, dtype=jnp.int32) * bm
    block_exp = jnp.sum((blk_start[:, None] >= pad_end[None, :]).astype(jnp.int32), axis=1)
    block_exp = jnp.minimum(block_exp, N_EXP - 1)
    n_valid = (pad_end[-1] // bm).astype(jnp.int32).reshape(1)
    return dest, row_tok, block_exp, n_valid


def grid_pos_embed(n_tok, d):
    rows = n_tok // GRID_W
    r = jnp.repeat(jnp.arange(rows, dtype=F32), GRID_W)
    col = jnp.tile(jnp.arange(GRID_W, dtype=F32), rows)
    quarter = d // 4
    omega = 1.0 / (10000.0 ** (jnp.arange(quarter, dtype=F32) / quarter))

    def emb(p):
        a = p[:, None] * omega[None, :]
        return jnp.concatenate([jnp.sin(a), jnp.cos(a)], axis=-1)

    return jnp.concatenate([emb(r), emb(col)], axis=-1)


def kernel(x_prompt, x_sample, state_hgrn, c, c_ctx, w_ada, b_ada, w_in, b_in, lb_logits, g_norm_a, conv_w, conv_b, conv_ln_g, conv_ln_b, w_br_a, w_br_b, w_br_c, b_br, w_gate, b_gate, w_o, b_o, ln1_g, ln1_b, w_router, b_router, w_gu, b_gu, w_down, b_down, ln2_g, ln2_b):
    depth = w_in.shape[0]
    bp, lp, d = x_prompt.shape
    bs, ls, _ = x_sample.shape
    tp, ts = bp * lp, bs * ls
    t = tp + ts
    alpha = (2 * depth) ** 0.25
    w_in_cols = w_in.shape[2]
    col_b = 3 * H_A * DK + 2 * W_A
    col_c = col_b + W_B

    tm, tn, tm_o, bm = TM, TN, TM_OUT, MOE_BM
    n_rows = (t * TOP_K // bm + N_EXP) * bm

    def row_of_block_for(rows_per_block):
        npb = tp // rows_per_block
        per = ls // rows_per_block
        return lambda i: jnp.where(i < npb, 0, 1 + (i - npb) // per)

    sm = jax.nn.softmax(lb_logits.astype(F32), axis=0)
    cs = jnp.cumsum(sm, axis=0)
    lower = cs - cs[0:1]

    cond = jnp.zeros((16, d), F32).at[0].set(c_ctx).at[1:1 + bs].set(c)
    mod = ada_mod(cond, w_ada, b_ada)

    xs = x_sample + grid_pos_embed(ls, d)[None]
    x = jnp.concatenate([x_prompt.reshape(tp, d), xs.reshape(ts, d)], axis=0)

    b_gu4 = b_gu[:, :, None, :]
    b_down4 = b_down[:, :, None, :]
    dl_p, cm = dft_tables(lp)
    dl_s, _ = dft_tables(ls)

    states = []
    for l in range(depth):
        mod3 = mod[l].reshape(16, 1, 6 * d)
        u = ln_mod_matmul(x, mod3, row_of_block_for(TM_IN), w_in[l].astype(BF16), b_in[l][None, :],
                          w_gate[l].astype(BF16), b_gate[l][None, :], tm=TM_IN, tn=tn)

        gn = g_norm_a[l][None, :]
        ya_p, s_ctx = hgrn_mixer(u, lower[l], gn, None, row0=0, batch=bp, seq=lp, want_state=True)
        (ya_s,) = hgrn_mixer(u, lower[l], gn, state_hgrn[:, l], row0=tp, batch=bs, seq=ls, want_state=False)
        states.append(s_ctx)
        yb_p = fourier_mixer(u, dl_p, cm, row0=0, col0=col_b, batch=bp, seq=lp)
        yb_s = fourier_mixer(u, dl_s, cm, row0=tp, col0=col_b, batch=bs, seq=ls)
        cw = (conv_w[l], conv_b[l][None, :], conv_ln_g[l][None, :], conv_ln_b[l][None, :])
        yc_p = conv_mixer(u, *cw, row0=0, col0=col_c, batch=bp, seq=lp)
        yc_s = conv_mixer(u, *cw, row0=tp, col0=col_c, batch=bs, seq=ls)
        merged = branch_merge((ya_p, yb_p, yc_p), (ya_s, yb_s, yc_s), u, w_br_a[l].astype(BF16),
                              w_br_b[l].astype(BF16), w_br_c[l].astype(BF16), b_br[l],
                              gate_col0=w_in_cols, d=d, tm=tm, tn=tn)

        wr = jnp.zeros((d, LANES), BF16).at[:, :N_EXP].set(w_router[l].astype(BF16))
        br = jnp.full((1, LANES), -1e30, F32).at[0, :N_EXP].set(b_router[l])
        x1, h2, idx, prob = out_router(x, merged, mod3, row_of_block_for(tm_o), w_o[l].astype(BF16),
                                       b_o[l][None, :], ln1_g[l][None, :], ln1_b[l][None, :], wr, br,
                                       alpha=alpha, tm=tm_o)

        rank, counts = route_rank(idx, tb=RANK_ROWS)
        dest, row_tok, block_exp, n_valid = route_layout(idx[:, :TOP_K], rank[:, :TOP_K], counts[0, :N_EXP],
                                                         bm=bm, n_rows=n_rows)
        xsort = sc_gather(h2, row_tok)
        act = moe_up(xsort, block_exp, n_valid, w_gu, b_gu4, l, bm=bm, tn=TN_UP)
        yrow = moe_down(act, block_exp, n_valid, w_down, b_down4, l, bm=bm, tn=TN_DOWN)
        yg = sc_gather(yrow, dest)
        combine = functools.partial(combine_ln, x1, yg, prob, mod3, row_of_block_for(tm_o), ln2_g[l][None, :],
                                    ln2_b[l][None, :], alpha=alpha, tm=tm_o, group=TN_DOWN // 2)
        if l + 1 < depth:
            x = combine()

    y_prompt = combine(row0=0, rows=tp).reshape(bp, lp, d)
    y_sample = combine(row0=tp, rows=ts).reshape(bs, ls, d)
    return y_prompt, y_sample, jnp.stack(states, axis=1)
```

```python
import functools
import math

import jax
import jax.numpy as jnp
from jax import lax
from jax.experimental import pallas as pl
from jax.experimental.pallas import tpu as pltpu
from jax.experimental.pallas import tpu_sc as plsc

F32 = jnp.float32
BF16 = jnp.bfloat16

GRID_W = 64
H_A = 8
DK = 128
DV = 128
W_A = H_A * DV
W_B = 512
W_C = 512
CONV_K = 31
N_BRANCH = 3
N_EXP = 32
TOP_K = 4
SWIGLU_LIMIT = 7.0
SWIGLU_ALPHA = 1.702
LN_EPS = 1e-5
RMS_EPS = 1e-6

LANES = 128
SUBLANES = 8
VMEM_LIMIT = 56 * 1024 * 1024
HGRN_CHUNK = 16
HGRN_UNROLL = 16
CONV_PAD = 16
CONV_ROWS = 32
LOGF_FLOOR = -1e4
LOG2E = 1.4426950408889634
TM = 1024
TM_IN = 2048
TN = 512
TM_OUT = 512
MOE_BM = 512
TN_UP = 1024
TN_DOWN = 2048
RANK_ROWS = 512
SC_GATHER_ROWS = 32


def _cparams(*sem):
    return pltpu.CompilerParams(dimension_semantics=sem, vmem_limit_bytes=VMEM_LIMIT)


def _ln_rows(x):
    mu = jnp.mean(x, axis=-1, keepdims=True)
    xc = x - mu
    var = jnp.mean(xc * xc, axis=-1, keepdims=True)
    return xc * lax.rsqrt(var + LN_EPS)


def _sigmoid(x):
    return 1.0 / (1.0 + jnp.exp(-x))


def _bf16_bits(x):
    u = lax.bitcast_convert_type(x, jnp.uint32)
    r = u + jnp.uint32(0x7FFF) + ((u >> 16) & jnp.uint32(1))
    return r & jnp.uint32(0xFFFF0000)


def _pack_halves(x):
    n = x.shape[1] // 2
    return (_bf16_bits(x[:, :n]) >> 16) | _bf16_bits(x[:, n:])


def _unpack_halves(w):
    lo = lax.bitcast_convert_type(w << 16, F32)
    hi = lax.bitcast_convert_type(w & jnp.uint32(0xFFFF0000), F32)
    return lo, hi


def _ada_kernel(c_ref, w_ref, b_ref, o_ref):
    c = c_ref[...]
    a = (c * _sigmoid(c)).astype(BF16)
    o_ref[...] = jnp.dot(a, w_ref[...].astype(BF16), preferred_element_type=F32) + b_ref[...]


def ada_mod(cond, w_ada, b_ada, *, tn=1024):
    depth, d, n = w_ada.shape
    r = cond.shape[0]
    return pl.pallas_call(
        _ada_kernel,
        grid=(depth, n // tn),
        in_specs=[
            pl.BlockSpec((r, d), lambda l, j: (0, 0)),
            pl.BlockSpec((None, d, tn), lambda l, j: (l, 0, j)),
            pl.BlockSpec((None, 1, tn), lambda l, j: (l, 0, j)),
        ],
        out_specs=pl.BlockSpec((None, r, tn), lambda l, j: (l, 0, j)),
        out_shape=jax.ShapeDtypeStruct((depth, r, n), F32),
        compiler_params=_cparams("parallel", "parallel"),
        name="ada_mod",
    )(cond, w_ada, b_ada.reshape(depth, 1, n))


def _lnmm_kernel(x_ref, sh_ref, sc_ref, wi_ref, bi_ref, wg_ref, bg_ref, o_ref, h_ref, *, n_plain):
    j = pl.program_id(1)

    @pl.when(j == 0)
    def _():
        y = _ln_rows(x_ref[...])
        h_ref[...] = (y * (1.0 + sc_ref[...]) + sh_ref[...]).astype(BF16)

    def project(w_ref, b_ref):
        return jnp.dot(h_ref[...], w_ref[...], preferred_element_type=F32) + b_ref[...]

    @pl.when(j < n_plain)
    def _():
        o_ref[...] = project(wi_ref, bi_ref).astype(BF16)

    @pl.when(j >= n_plain)
    def _():
        o_ref[...] = _sigmoid(project(wg_ref, bg_ref)).astype(BF16)


def ln_mod_matmul(x, mod3, row_of_block, w_in, b_in, w_gate, b_gate, *, tm, tn):
    t, d = x.shape
    n1 = w_in.shape[1] // tn
    n2 = w_gate.shape[1] // tn

    def first(j):
        return jnp.minimum(j, n1 - 1)

    def second(j):
        return jnp.maximum(j - n1, 0)

    return pl.pallas_call(
        functools.partial(_lnmm_kernel, n_plain=n1),
        grid=(t // tm, n1 + n2),
        in_specs=[
            pl.BlockSpec((tm, d), lambda i, j: (i, 0), pipeline_mode=pl.Buffered(1)),
            pl.BlockSpec((None, 1, d), lambda i, j: (row_of_block(i), 0, 0)),
            pl.BlockSpec((None, 1, d), lambda i, j: (row_of_block(i), 0, 1)),
            pl.BlockSpec((d, tn), lambda i, j: (0, first(j))),
            pl.BlockSpec((1, tn), lambda i, j: (0, first(j))),
            pl.BlockSpec((d, tn), lambda i, j: (0, second(j))),
            pl.BlockSpec((1, tn), lambda i, j: (0, second(j))),
        ],
        out_specs=pl.BlockSpec((tm, tn), lambda i, j: (i, j)),
        out_shape=jax.ShapeDtypeStruct((t, (n1 + n2) * tn), BF16),
        scratch_shapes=[pltpu.VMEM((tm, d), BF16)],
        compiler_params=_cparams("parallel", "arbitrary"),
        name="ln_mod_matmul",
    )(x, mod3, mod3, w_in, b_in, w_gate, b_gate)


def _chunk_scan(x_h, direction):
    half = SUBLANES
    rid = lax.broadcasted_iota(jnp.int32, x_h[0].shape, 0)
    out = []
    for x in x_h:
        step = 1
        while step < half:
            if direction == 0:
                x = x + jnp.where(rid >= step, pltpu.roll(x, step, axis=0), 0.0)
            else:
                x = x + jnp.where(rid < half - step, pltpu.roll(x, half - step, axis=0), 0.0)
            step *= 2
        out.append(x)
    if direction == 0:
        out[1] = out[1] + out[0][half - 1:half, :]
    else:
        out[0] = out[0] + out[1][0:1, :]
    return out


def _hgrn_chunk(q_ref, z_ref, v_ref, lb, oml, st_ref, o_ref, rows_ref, c0, *, direction):
    half = SUBLANES
    rows = pl.ds(c0, HGRN_CHUNK)
    q = q_ref[rows, :].astype(F32) * (DK ** -0.5)
    z = z_ref[rows, :].astype(F32)
    v = v_ref[rows, :]
    e = jnp.exp2(jnp.abs(z) * (-LOG2E))
    r = 1.0 / (1.0 + e)
    er = e * r
    pos = z >= 0
    lf2 = jnp.maximum(jnp.log2(lb + oml * jnp.where(pos, r, er)), LOGF_FLOOR)
    lk2 = jnp.log2(oml * jnp.where(pos, er, r))

    b_h = _chunk_scan([lf2[0:half, :], lf2[half:HGRN_CHUNK, :]], direction)
    b2 = jnp.concatenate(b_h, axis=0)
    b_end = b_h[1][half - 1:half, :] if direction == 0 else b_h[0][0:1, :]

    st = st_ref[direction]
    qe = (q * jnp.exp2(b2)).astype(BF16)
    o_inter = lax.dot_general(qe, st.astype(BF16), (((1,), (1,)), ((), ())), preferred_element_type=F32)

    rows_ref[0] = b2 - lk2
    rows_ref[1] = v.astype(F32)
    q_h = (q[0:half, :], q[half:HGRN_CHUNK, :])
    acc = [jnp.zeros((half, DV), F32), jnp.zeros((half, DV), F32)]
    rid = lax.broadcasted_iota(jnp.int32, (half, DK), 0)
    for s in range(HGRN_CHUNK):
        hs, rs = divmod(s, half)
        bs = jnp.broadcast_to(rows_ref[0, s:s + 1, :], (half, DK))
        vs = jnp.broadcast_to(rows_ref[1, s:s + 1, :], (half, DV))
        for ht in range(2):
            if (ht < hs) if direction == 0 else (ht > hs):
                continue
            d = b_h[ht] - bs
            if ht == hs:
                keep = (rid >= rs) if direction == 0 else (rid <= rs)
                d = jnp.where(keep, d, -1e30)
            w = jnp.sum(q_h[ht] * jnp.exp2(d), axis=-1, keepdims=True)
            acc[ht] = acc[ht] + w * vs
    o_ref[pl.ds(c0, half), :] = acc[0] + o_inter[0:half, :]
    o_ref[pl.ds(c0 + half, half), :] = acc[1] + o_inter[half:HGRN_CHUNK, :]

    kk = jnp.exp2(lk2 + (b_end - b2)).astype(BF16)
    upd = lax.dot_general(v, kk, (((0,), (0,)), ((), ())), preferred_element_type=F32)
    st_ref[direction] = st * jnp.exp2(b_end) + upd


def _hgrn_kernel(*refs, seq, unroll, has_s0, want_state):
    q_ref, zf_ref, zb_ref, v_ref, g_ref, lb_ref, gn_ref = refs[:7]
    pos = 7
    s0_ref = None
    if has_s0:
        s0_ref = refs[pos]
        pos += 1
    y_ref = refs[pos]
    pos += 1
    sfin_ref = None
    if want_state:
        sfin_ref = refs[pos]
        pos += 1
    of_ref, ob_ref, st_ref, rows_ref = refs[pos:pos + 4]

    for d in range(2):
        st_ref[d] = s0_ref[d].T if has_s0 else jnp.zeros((DV, DK), F32)

    n = seq // HGRN_CHUNK
    lb_f = lb_ref[0:1, :]
    lb_b = lb_ref[1:2, :]
    oml_f = 1.0 - lb_f
    oml_b = 1.0 - lb_b

    def body(i, carry):
        for j in range(unroll):
            cf = pl.multiple_of((i * unroll + j) * HGRN_CHUNK, HGRN_CHUNK)
            cb = pl.multiple_of((n - 1 - (i * unroll + j)) * HGRN_CHUNK, HGRN_CHUNK)
            _hgrn_chunk(q_ref, zf_ref, v_ref, lb_f, oml_f, st_ref, of_ref, rows_ref.at[2 * j], cf, direction=0)
            _hgrn_chunk(q_ref, zb_ref, v_ref, lb_b, oml_b, st_ref, ob_ref, rows_ref.at[2 * j + 1], cb, direction=1)
        return carry

    lax.fori_loop(0, n // unroll, body, 0)

    blk = min(seq, 256)
    for r0 in range(0, seq, blk):
        o = of_ref[r0:r0 + blk, :] + ob_ref[r0:r0 + blk, :]
        g = g_ref[r0:r0 + blk, :].astype(F32)
        o = o * lax.rsqrt(jnp.mean(o * o, axis=-1, keepdims=True) + RMS_EPS) * gn_ref[...]
        y_ref[r0:r0 + blk, :] = (o * (g * _sigmoid(g))).astype(BF16)
    if want_state:
        sfin_ref[0] = st_ref[0].T
        sfin_ref[1] = st_ref[1].T


def hgrn_mixer(u, lb, g_norm, s0, *, row0, batch, seq, want_state, unroll=HGRN_UNROLL):
    rb0 = row0 // seq
    has_s0 = s0 is not None
    unroll = math.gcd(seq // HGRN_CHUNK, unroll)

    def sec(k):
        return pl.BlockSpec((seq, LANES), lambda b, h: (rb0 + b, k * H_A + h))

    in_specs = [sec(0), sec(1), sec(2), sec(3), sec(4),
                pl.BlockSpec((2, DK), lambda b, h: (0, h)),
                pl.BlockSpec((1, DV), lambda b, h: (0, 0))]
    args = [u, u, u, u, u, lb, g_norm]
    state_spec = pl.BlockSpec((None, 2, None, DK, DV), lambda b, h: (b, 0, h, 0, 0))
    if has_s0:
        in_specs.append(state_spec)
        args.append(s0)
    out_specs = [pl.BlockSpec((seq, DV), lambda b, h: (b, h))]
    out_shape = [jax.ShapeDtypeStruct((batch * seq, W_A), BF16)]
    if want_state:
        out_specs.append(state_spec)
        out_shape.append(jax.ShapeDtypeStruct((batch, 2, H_A, DK, DV), F32))
    return pl.pallas_call(
        functools.partial(_hgrn_kernel, seq=seq, unroll=unroll, has_s0=has_s0, want_state=want_state),
        grid=(batch, H_A),
        in_specs=in_specs,
        out_specs=out_specs,
        out_shape=out_shape,
        scratch_shapes=[pltpu.VMEM((seq, DV), F32), pltpu.VMEM((seq, DV), F32), pltpu.VMEM((2, DV, DK), F32),
                        pltpu.VMEM((2 * unroll, 2, HGRN_CHUNK, DK), F32)],
        compiler_params=_cparams("parallel", "parallel"),
        name="hgrn_mixer",
    )(*args)


def _fourier_kernel(u_ref, cm_ref, dl_ref, o_ref, ab_ref, *, seq, rows, scale):
    @pl.when(pl.program_id(1) == 0)
    def _():
        for r0 in range(0, seq, rows):
            ab = jnp.dot(u_ref[r0:r0 + rows, :], cm_ref[...], preferred_element_type=F32)
            ab_ref[r0:r0 + rows, :] = ab[:, :W_B].astype(BF16)
            ab_ref[seq + r0:seq + r0 + rows, :] = ab[:, W_B:].astype(BF16)

    o_ref[...] = (jnp.dot(dl_ref[...], ab_ref[...], preferred_element_type=F32) * scale).astype(BF16)


def dft_tables(seq):
    def cs(n):
        i = lax.iota(jnp.int32, n)
        ang = ((i[:, None] * i[None, :]) % n).astype(F32) * (2.0 * math.pi / n)
        return jnp.cos(ang), jnp.sin(ang)
    cl, sl = cs(seq)
    cc, sc = cs(W_B)
    return jnp.concatenate([cl, -sl], axis=1).astype(BF16), jnp.concatenate([cc, sc], axis=1).astype(BF16)


def fourier_mixer(u, dl, cm, *, row0, col0, batch, seq, rows=256):
    rows = min(rows, seq)
    rb0 = row0 // seq
    cb = col0 // W_B
    nt = seq // rows
    return pl.pallas_call(
        functools.partial(_fourier_kernel, seq=seq, rows=rows, scale=1.0 / math.sqrt(seq * W_B)),
        grid=(batch, nt),
        in_specs=[
            pl.BlockSpec((seq, W_B), lambda b, i: (rb0 + b, cb)),
            pl.BlockSpec((W_B, 2 * W_B), lambda b, i: (0, 0)),
            pl.BlockSpec((rows, 2 * seq), lambda b, i: (i, 0)),
        ],
        out_specs=pl.BlockSpec((rows, W_B), lambda b, i: (b * nt + i, 0)),
        out_shape=jax.ShapeDtypeStruct((batch * seq, W_B), BF16),
        scratch_shapes=[pltpu.VMEM((2 * seq, W_B), BF16)],
        compiler_params=_cparams("parallel", "arbitrary"),
        name="fourier_mixer",
    )(u, cm, dl)


def _conv_kernel(a_ref, gt_ref, w_ref, b_ref, g_ref, be_ref, o_ref, z_ref, sh_ref, *, seq):
    zeros = jnp.zeros((CONV_PAD, W_C), F32)
    z_ref[0:CONV_PAD, :] = zeros
    z_ref[CONV_PAD + seq:2 * CONV_PAD + seq, :] = zeros
    blk = min(seq, 256)
    for r0 in range(0, seq, blk):
        a = a_ref[r0:r0 + blk, :].astype(F32)
        gt = gt_ref[r0:r0 + blk, :].astype(F32)
        z_ref[CONV_PAD + r0:CONV_PAD + r0 + blk, :] = a * _sigmoid(gt)

    off = CONV_PAD - CONV_K // 2

    def tile(i, carry):
        r0 = pl.multiple_of(i * CONV_ROWS, CONV_ROWS)
        win = z_ref[pl.ds(r0, CONV_ROWS + 2 * CONV_PAD), :]
        span = CONV_ROWS + 2 * CONV_PAD - SUBLANES
        for r in range(SUBLANES):
            sh_ref[r] = win[r:r + span, :]
        acc = jnp.zeros((CONV_ROWS, W_C), F32)
        for k in range(CONV_K):
            m, r = divmod(off + k, SUBLANES)
            acc = acc + sh_ref[r, m * SUBLANES:m * SUBLANES + CONV_ROWS, :] * w_ref[k:k + 1, :]
        y = _ln_rows(acc + b_ref[...]) * g_ref[...] + be_ref[...]
        o_ref[pl.ds(r0, CONV_ROWS), :] = (y * _sigmoid(y)).astype(BF16)
        return carry

    lax.fori_loop(0, seq // CONV_ROWS, tile, 0)


def conv_mixer(u, conv_w, conv_b, ln_g, ln_b, *, row0, col0, batch, seq):
    rb0 = row0 // seq
    cb = col0 // W_C
    vec = pl.BlockSpec((1, W_C), lambda b: (0, 0))
    return pl.pallas_call(
        functools.partial(_conv_kernel, seq=seq),
        grid=(batch,),
        in_specs=[
            pl.BlockSpec((seq, W_C), lambda b: (rb0 + b, cb)),
            pl.BlockSpec((seq, W_C), lambda b: (rb0 + b, cb + 1)),
            pl.BlockSpec((CONV_K, W_C), lambda b: (0, 0)),
            vec, vec, vec,
        ],
        out_specs=pl.BlockSpec((seq, W_C), lambda b: (b, 0)),
        out_shape=jax.ShapeDtypeStruct((batch * seq, W_C), BF16),
        scratch_shapes=[pltpu.VMEM((seq + 2 * CONV_PAD, W_C), F32),
                        pltpu.VMEM((SUBLANES, CONV_ROWS + 2 * CONV_PAD - SUBLANES, W_C), F32)],
        compiler_params=_cparams("parallel"),
        name="conv_mixer",
    )(u, u, conv_w, conv_b, ln_g, ln_b)


def _branch_kernel(yap_ref, ybp_ref, ycp_ref, yas_ref, ybs_ref, ycs_ref, g0_ref, g1_ref, g2_ref,
                   wa_ref, wb_ref, wc_ref, b_ref, o_ref, *, n_first):
    def merge(ya_ref, yb_ref, yc_ref):
        def proj(y_ref, w_ref, k):
            return jnp.dot(y_ref[...], w_ref[...], preferred_element_type=F32) + b_ref[k:k + 1, :]
        m = g0_ref[...].astype(F32) * proj(ya_ref, wa_ref, 0)
        m = m + g1_ref[...].astype(F32) * proj(yb_ref, wb_ref, 1)
        m = m + g2_ref[...].astype(F32) * proj(yc_ref, wc_ref, 2)
        o_ref[...] = m.astype(BF16)

    first = pl.program_id(0) < n_first

    @pl.when(first)
    def _():
        merge(yap_ref, ybp_ref, ycp_ref)

    @pl.when(jnp.logical_not(first))
    def _():
        merge(yas_ref, ybs_ref, ycs_ref)


def branch_merge(y_first, y_second, u, wa, wb, wc, b_br, *, gate_col0, d, tm, tn):
    t1 = y_first[0].shape[0]
    t2 = y_second[0].shape[0]
    n1 = t1 // tm
    n2 = t2 // tm
    gb = gate_col0 // tn
    nd = d // tn

    def gate(k):
        return pl.BlockSpec((tm, tn), lambda i, j: (i, gb + k * nd + j))

    def first(y):
        return pl.BlockSpec((tm, y.shape[1]), lambda i, j: (jnp.minimum(i, n1 - 1), 0))

    def second(y):
        return pl.BlockSpec((tm, y.shape[1]), lambda i, j: (jnp.maximum(i - n1, 0), 0))

    return pl.pallas_call(
        functools.partial(_branch_kernel, n_first=n1),
        grid=(n1 + n2, nd),
        in_specs=[
            first(y_first[0]), first(y_first[1]), first(y_first[2]),
            second(y_second[0]), second(y_second[1]), second(y_second[2]),
            gate(0), gate(1), gate(2),
            pl.BlockSpec((wa.shape[0], tn), lambda i, j: (0, j)),
            pl.BlockSpec((wb.shape[0], tn), lambda i, j: (0, j)),
            pl.BlockSpec((wc.shape[0], tn), lambda i, j: (0, j)),
            pl.BlockSpec((N_BRANCH, tn), lambda i, j: (0, j)),
        ],
        out_specs=pl.BlockSpec((tm, tn), lambda i, j: (i, j)),
        out_shape=jax.ShapeDtypeStruct((t1 + t2, d), BF16),
        compiler_params=_cparams("parallel", "arbitrary"),
        name="branch_merge",
    )(*y_first, *y_second, u, u, u, wa, wb, wc, b_br)


def _out_router_kernel(x_ref, m_ref, g1_ref, sh2_ref, sc2_ref, wo_ref, bo_ref, lg_ref, lb_ref,
                       wr_ref, br_ref, x1_ref, h2_ref, idx_ref, p_ref, *, alpha):
    mix = jnp.dot(m_ref[...], wo_ref[...], preferred_element_type=F32) + bo_ref[...]
    x1 = _ln_rows(alpha * x_ref[...] + g1_ref[...] * mix) * lg_ref[...] + lb_ref[...]
    x1_ref[...] = x1
    h2 = _ln_rows(x1) * (1.0 + sc2_ref[...]) + sh2_ref[...]
    h2_ref[...] = _pack_halves(h2)
    logits = jnp.dot(h2.astype(BF16), wr_ref[...], preferred_element_type=F32) + br_ref[...]
    lane = lax.broadcasted_iota(jnp.int32, logits.shape, 1)
    idx_out = jnp.zeros(logits.shape, jnp.int32)
    p_out = jnp.zeros(logits.shape, F32)
    top = None
    denom = None
    for k in range(TOP_K):
        m = jnp.max(logits, axis=-1, keepdims=True)
        i = jnp.min(jnp.where(logits == m, lane, LANES), axis=-1, keepdims=True)
        if k == 0:
            top = m
        ek = jnp.exp(m - top)
        denom = ek if k == 0 else denom + ek
        idx_out = jnp.where(lane == k, i, idx_out)
        p_out = jnp.where(lane == k, ek, p_out)
        logits = jnp.where(lane == i, -jnp.inf, logits)
    idx_ref[...] = idx_out
    p_ref[...] = p_out / denom


def out_router(x, merged, mod3, row_of_block, wo, bo, ln_g, ln_b, wr, br, *, alpha, tm):
    t, d = x.shape

    def modspec(k):
        return pl.BlockSpec((None, 1, d), lambda i: (row_of_block(i), 0, k))

    vec = pl.BlockSpec((1, d), lambda i: (0, 0))
    row_f = pl.BlockSpec((tm, d), lambda i: (i, 0))
    row_s = pl.BlockSpec((tm, LANES), lambda i: (i, 0))
    return pl.pallas_call(
        functools.partial(_out_router_kernel, alpha=alpha),
        grid=(t // tm,),
        in_specs=[row_f, row_f, modspec(2), modspec(3), modspec(4),
                  pl.BlockSpec((d, d), lambda i: (0, 0)), vec, vec, vec,
                  pl.BlockSpec((d, LANES), lambda i: (0, 0)),
                  pl.BlockSpec((1, LANES), lambda i: (0, 0))],
        out_specs=[row_f, pl.BlockSpec((tm, d // 2), lambda i: (i, 0)), row_s, row_s],
        out_shape=[jax.ShapeDtypeStruct((t, d), F32), jax.ShapeDtypeStruct((t, d // 2), jnp.uint32),
                   jax.ShapeDtypeStruct((t, LANES), jnp.int32), jax.ShapeDtypeStruct((t, LANES), F32)],
        compiler_params=_cparams("parallel"),
        name="out_router",
    )(x, merged, mod3, mod3, mod3, wo, bo, ln_g, ln_b, wr, br)


def _moe_up_kernel(be_ref, nv_ref, rv_ref, x_ref, wg_ref, wu_ref, bg_ref, bu_ref, o_ref, wgs_ref, wus_ref):
    b = pl.program_id(1)
    fresh = jnp.logical_or(b == 0, be_ref[b] != be_ref[jnp.maximum(b - 1, 0)])

    @pl.when(fresh)
    def _():
        wgs_ref[...] = wg_ref[...].astype(BF16)
        wus_ref[...] = wu_ref[...].astype(BF16)

    @pl.when(b < nv_ref[0])
    def _():
        lo, hi = _unpack_halves(x_ref[...])
        live = lax.broadcasted_iota(jnp.int32, lo.shape, 0) < rv_ref[b]
        lo = jnp.where(live, lo, 0.0).astype(BF16)
        hi = jnp.where(live, hi, 0.0).astype(BF16)
        kh = lo.shape[1]

        def proj(w_ref, b_ref):
            return (jnp.dot(lo, w_ref[0:kh, :], preferred_element_type=F32)
                    + jnp.dot(hi, w_ref[kh:2 * kh, :], preferred_element_type=F32) + b_ref[...])

        gt = proj(wgs_ref, bg_ref)
        up = proj(wus_ref, bu_ref)
        gt = jnp.minimum(gt, SWIGLU_LIMIT)
        up = jnp.clip(up, -SWIGLU_LIMIT, SWIGLU_LIMIT)
        o_ref[...] = ((up + 1.0) * (gt * _sigmoid(SWIGLU_ALPHA * gt))).astype(BF16)

    @pl.when(b >= nv_ref[0])
    def _():
        o_ref[...] = jnp.zeros_like(o_ref)


def moe_up(xs, block_exp, n_valid, rows_valid, w_gu, b_gu, layer, *, bm, tn):
    r = xs.shape[0]
    d = w_gu.shape[2]
    f = w_gu.shape[3] // 2
    nf = f // tn
    grid_spec = pltpu.PrefetchScalarGridSpec(
        num_scalar_prefetch=3,
        grid=(nf, r // bm),
        in_specs=[
            pl.BlockSpec((bm, d // 2), lambda n, b, be, nv, rv: (b, 0)),
            pl.BlockSpec((None, None, d, tn), lambda n, b, be, nv, rv: (layer, be[b], 0, n)),
            pl.BlockSpec((None, None, d, tn), lambda n, b, be, nv, rv: (layer, be[b], 0, nf + n)),
            pl.BlockSpec((None, None, 1, tn), lambda n, b, be, nv, rv: (layer, be[b], 0, n)),
            pl.BlockSpec((None, None, 1, tn), lambda n, b, be, nv, rv: (layer, be[b], 0, nf + n)),
        ],
        out_specs=pl.BlockSpec((bm, tn), lambda n, b, be, nv, rv: (b, n)),
        scratch_shapes=[pltpu.VMEM((d, tn), BF16), pltpu.VMEM((d, tn), BF16)],
    )
    return pl.pallas_call(
        _moe_up_kernel,
        grid_spec=grid_spec,
        out_shape=jax.ShapeDtypeStruct((r, f), BF16),
        compiler_params=_cparams("arbitrary", "arbitrary"),
        name="moe_up",
    )(block_exp, n_valid, rows_valid, xs, w_gu, w_gu, b_gu, b_gu)


def _moe_down_kernel(be_ref, nv_ref, a_ref, w_ref, b_ref, o_ref, ws_ref):
    b = pl.program_id(1)
    fresh = jnp.logical_or(b == 0, be_ref[b] != be_ref[jnp.maximum(b - 1, 0)])

    @pl.when(fresh)
    def _():
        ws_ref[...] = w_ref[...].astype(BF16)

    @pl.when(b < nv_ref[0])
    def _():
        y = jnp.dot(a_ref[...], ws_ref[...], preferred_element_type=F32) + b_ref[...]
        o_ref[...] = _pack_halves(y)

    @pl.when(b >= nv_ref[0])
    def _():
        o_ref[...] = jnp.zeros_like(o_ref)


def moe_down(act, block_exp, n_valid, w_down, b_down, layer, *, bm, tn):
    r, f = act.shape
    d = w_down.shape[3]
    grid_spec = pltpu.PrefetchScalarGridSpec(
        num_scalar_prefetch=2,
        grid=(d // tn, r // bm),
        in_specs=[
            pl.BlockSpec((bm, f), lambda n, b, be, nv: (b, 0)),
            pl.BlockSpec((None, None, f, tn), lambda n, b, be, nv: (layer, be[b], 0, n)),
            pl.BlockSpec((None, None, 1, tn), lambda n, b, be, nv: (layer, be[b], 0, n)),
        ],
        out_specs=pl.BlockSpec((bm, tn // 2), lambda n, b, be, nv: (b, n)),
        scratch_shapes=[pltpu.VMEM((f, tn), BF16)],
    )
    return pl.pallas_call(
        _moe_down_kernel,
        grid_spec=grid_spec,
        out_shape=jax.ShapeDtypeStruct((r, d // 2), jnp.uint32),
        compiler_params=_cparams("arbitrary", "arbitrary"),
        name="moe_down",
    )(block_exp, n_valid, act, w_down, b_down)


def sc_gather(x, idx):
    n = idx.shape[0]
    d = x.shape[1]
    win = SC_GATHER_ROWS
    mesh = plsc.VectorSubcoreMesh(core_axis_name="core", subcore_axis_name="subcore")

    @pl.kernel(out_type=jax.ShapeDtypeStruct((n, d), x.dtype), mesh=mesh)
    def gather_kernel(x_hbm, i_hbm, o_hbm):
        def body(i_vmem, o_vmem):
            pltpu.sync_copy(x_hbm.at[i_vmem.at[0]], o_vmem)

        pltpu.emit_pipeline(
            body,
            grid=(n // win,),
            in_specs=[pl.BlockSpec((1, win), index_map=lambda i: (i, 0))],
            out_specs=[pl.BlockSpec((win, d), index_map=lambda i: (i, 0))],
            core_axis_name=("core", "subcore"),
            dimension_semantics=(pltpu.PARALLEL,),
        )(i_hbm, o_hbm)

    return gather_kernel(x, idx.reshape(n // win, win))


def sc_scatter_rows(x, dest, n_rows):
    t, d = x.shape
    n = dest.shape[0]
    win = SC_GATHER_ROWS
    src_blocks = t // win
    mesh = plsc.VectorSubcoreMesh(core_axis_name="core", subcore_axis_name="subcore")

    @pl.kernel(out_type=jax.ShapeDtypeStruct((n_rows, d), x.dtype), mesh=mesh, scratch_types=[])
    def scatter_kernel(x_hbm, i_hbm, o_hbm):
        def body(x_vmem, i_vmem):
            pltpu.sync_copy(x_vmem, o_hbm.at[i_vmem.at[0]])

        pltpu.emit_pipeline(
            body,
            grid=(n // win,),
            in_specs=[pl.BlockSpec((win, d), index_map=lambda i: (i % src_blocks, 0)),
                      pl.BlockSpec((1, win), index_map=lambda i: (i, 0))],
            out_specs=[],
            core_axis_name=("core", "subcore"),
            dimension_semantics=(pltpu.PARALLEL,),
        )(x_hbm, i_hbm)

    return scatter_kernel(x, dest.reshape(n // win, win))


def _combine_kernel(x_ref, y0_ref, y1_ref, y2_ref, y3_ref, p_ref, g2_ref, lg_ref, lb_ref, o_ref, *, alpha, group):
    p = p_ref[...]
    ffn = None
    for k, y_ref in enumerate((y0_ref, y1_ref, y2_ref, y3_ref)):
        w = y_ref[...]
        cols = []
        for c0 in range(0, w.shape[1], group):
            cols.extend(_unpack_halves(w[:, c0:c0 + group]))
        term = p[:, k:k + 1] * jnp.concatenate(cols, axis=1)
        ffn = term if ffn is None else ffn + term
    o_ref[...] = _ln_rows(alpha * x_ref[...] + g2_ref[...] * ffn) * lg_ref[...] + lb_ref[...]


def combine_ln(x1, yg, prob, mod3, row_of_block, ln_g, ln_b, *, alpha, tm, group, row0=0, rows=None):
    t, d = x1.shape
    rows = t if rows is None else rows
    nb = t // tm
    b0 = row0 // tm
    vec = pl.BlockSpec((1, d), lambda i: (0, 0))

    def sel(k):
        return pl.BlockSpec((tm, d // 2), lambda i: (k * nb + b0 + i, 0))

    return pl.pallas_call(
        functools.partial(_combine_kernel, alpha=alpha, group=group),
        grid=(rows // tm,),
        in_specs=[pl.BlockSpec((tm, d), lambda i: (b0 + i, 0)), sel(0), sel(1), sel(2), sel(3),
                  pl.BlockSpec((tm, LANES), lambda i: (b0 + i, 0)),
                  pl.BlockSpec((None, 1, d), lambda i: (row_of_block(b0 + i), 0, 5)),
                  vec, vec],
        out_specs=pl.BlockSpec((tm, d), lambda i: (i, 0)),
        out_shape=jax.ShapeDtypeStruct((rows, d), F32),
        compiler_params=_cparams("parallel"),
        name="combine_ln",
    )(x1, yg, yg, yg, yg, prob, mod3, ln_g, ln_b)


def _rank_kernel(idx_ref, rank_ref, cnt_ref, carry_ref):
    @pl.when(pl.program_id(0) == 0)
    def _():
        carry_ref[...] = jnp.zeros_like(carry_ref)

    idx = idx_ref[...]
    tb = idx.shape[0]
    lane = lax.broadcasted_iota(jnp.int32, idx.shape, 1)
    hot = [(lane == idx[:, k:k + 1]) for k in range(TOP_K)]
    tot = hot[0]
    for k in range(1, TOP_K):
        tot = jnp.logical_or(tot, hot[k])
    tot = jnp.where(tot, 1.0, 0.0)
    ti = lax.broadcasted_iota(jnp.int32, (tb, tb), 0)
    si = lax.broadcasted_iota(jnp.int32, (tb, tb), 1)
    before = jnp.where(ti > si, 1.0, 0.0).astype(BF16)
    prefix = jnp.dot(before, tot.astype(BF16), preferred_element_type=F32) + carry_ref[...]
    out = jnp.zeros(idx.shape, F32)
    for k in range(TOP_K):
        rk = jnp.sum(jnp.where(hot[k], prefix, 0.0), axis=-1, keepdims=True)
        out = jnp.where(lane == k, rk, out)
    rank_ref[...] = out.astype(jnp.int32)
    carry_ref[...] += jnp.sum(tot, axis=0, keepdims=True)
    cnt_ref[...] = carry_ref[...].astype(jnp.int32)


def route_rank(idx, *, tb):
    t = idx.shape[0]
    return pl.pallas_call(
        _rank_kernel,
        grid=(t // tb,),
        in_specs=[pl.BlockSpec((tb, LANES), lambda i: (i, 0))],
        out_specs=[pl.BlockSpec((tb, LANES), lambda i: (i, 0)), pl.BlockSpec((1, LANES), lambda i: (0, 0))],
        out_shape=[jax.ShapeDtypeStruct((t, LANES), jnp.int32), jax.ShapeDtypeStruct((1, LANES), jnp.int32)],
        scratch_shapes=[pltpu.VMEM((1, LANES), F32)],
        compiler_params=_cparams("arbitrary"),
        name="route_rank",
    )(idx)


def route_layout(top_idx, rank, counts, *, bm, n_rows):
    padded = (counts + bm - 1) // bm * bm
    pad_end = jnp.cumsum(padded)
    pad_start = pad_end - padded
    dest = (pad_start[top_idx] + rank).T.reshape(-1)
    nb = n_rows // bm
    blk_start = jnp.arange(nb, dtype=jnp.int32) * bm
    block_exp = jnp.sum((blk_start[:, None] >= pad_end[None, :]).astype(jnp.int32), axis=1)
    block_exp = jnp.minimum(block_exp, N_EXP - 1)
    n_valid = (pad_end[-1] // bm).astype(jnp.int32).reshape(1)
    rows_valid = jnp.clip(pad_start[block_exp] + counts[block_exp] - blk_start, 0, bm).astype(jnp.int32)
    return dest, block_exp, n_valid, rows_valid


def grid_pos_embed(n_tok, d):
    rows = n_tok // GRID_W
    r = jnp.repeat(jnp.arange(rows, dtype=F32), GRID_W)
    col = jnp.tile(jnp.arange(GRID_W, dtype=F32), rows)
    quarter = d // 4
    omega = 1.0 / (10000.0 ** (jnp.arange(quarter, dtype=F32) / quarter))

    def emb(p):
        a = p[:, None] * omega[None, :]
        return jnp.concatenate([jnp.sin(a), jnp.cos(a)], axis=-1)

    return jnp.concatenate([emb(r), emb(col)], axis=-1)


def kernel(x_prompt, x_sample, state_hgrn, c, c_ctx, w_ada, b_ada, w_in, b_in, lb_logits, g_norm_a, conv_w, conv_b, conv_ln_g, conv_ln_b, w_br_a, w_br_b, w_br_c, b_br, w_gate, b_gate, w_o, b_o, ln1_g, ln1_b, w_router, b_router, w_gu, b_gu, w_down, b_down, ln2_g, ln2_b):
    depth = w_in.shape[0]
    bp, lp, d = x_prompt.shape
    bs, ls, _ = x_sample.shape
    tp, ts = bp * lp, bs * ls
    t = tp + ts
    alpha = (2 * depth) ** 0.25
    w_in_cols = w_in.shape[2]
    col_b = 3 * H_A * DK + 2 * W_A
    col_c = col_b + W_B

    tm, tn, tm_o, bm = TM, TN, TM_OUT, MOE_BM
    n_rows = (t * TOP_K // bm + N_EXP) * bm

    def row_of_block_for(rows_per_block):
        npb = tp // rows_per_block
        per = ls // rows_per_block
        return lambda i: jnp.where(i < npb, 0, 1 + (i - npb) // per)

    sm = jax.nn.softmax(lb_logits.astype(F32), axis=0)
    cs = jnp.cumsum(sm, axis=0)
    lower = cs - cs[0:1]

    cond = jnp.zeros((16, d), F32).at[0].set(c_ctx).at[1:1 + bs].set(c)
    mod = ada_mod(cond, w_ada, b_ada)

    xs = x_sample + grid_pos_embed(ls, d)[None]
    x = jnp.concatenate([x_prompt.reshape(tp, d), xs.reshape(ts, d)], axis=0)

    b_gu4 = b_gu[:, :, None, :]
    b_down4 = b_down[:, :, None, :]
    dl_p, cm = dft_tables(lp)
    dl_s, _ = dft_tables(ls)

    states = []
    for l in range(depth):
        mod3 = mod[l].reshape(16, 1, 6 * d)
        u = ln_mod_matmul(x, mod3, row_of_block_for(TM_IN), w_in[l].astype(BF16), b_in[l][None, :],
                          w_gate[l].astype(BF16), b_gate[l][None, :], tm=TM_IN, tn=tn)

        gn = g_norm_a[l][None, :]
        ya_p, s_ctx = hgrn_mixer(u, lower[l], gn, None, row0=0, batch=bp, seq=lp, want_state=True)
        (ya_s,) = hgrn_mixer(u, lower[l], gn, state_hgrn[:, l], row0=tp, batch=bs, seq=ls, want_state=False)
        states.append(s_ctx)
        yb_p = fourier_mixer(u, dl_p, cm, row0=0, col0=col_b, batch=bp, seq=lp)
        yb_s = fourier_mixer(u, dl_s, cm, row0=tp, col0=col_b, batch=bs, seq=ls)
        cw = (conv_w[l], conv_b[l][None, :], conv_ln_g[l][None, :], conv_ln_b[l][None, :])
        yc_p = conv_mixer(u, *cw, row0=0, col0=col_c, batch=bp, seq=lp)
        yc_s = conv_mixer(u, *cw, row0=tp, col0=col_c, batch=bs, seq=ls)
        merged = branch_merge((ya_p, yb_p, yc_p), (ya_s, yb_s, yc_s), u, w_br_a[l].astype(BF16),
                              w_br_b[l].astype(BF16), w_br_c[l].astype(BF16), b_br[l],
                              gate_col0=w_in_cols, d=d, tm=tm, tn=tn)

        wr = jnp.zeros((d, LANES), BF16).at[:, :N_EXP].set(w_router[l].astype(BF16))
        br = jnp.full((1, LANES), -1e30, F32).at[0, :N_EXP].set(b_router[l])
        x1, h2, idx, prob = out_router(x, merged, mod3, row_of_block_for(tm_o), w_o[l].astype(BF16),
                                       b_o[l][None, :], ln1_g[l][None, :], ln1_b[l][None, :], wr, br,
                                       alpha=alpha, tm=tm_o)

        rank, counts = route_rank(idx, tb=RANK_ROWS)
        dest, block_exp, n_valid, rows_valid = route_layout(idx[:, :TOP_K], rank[:, :TOP_K], counts[0, :N_EXP],
                                                            bm=bm, n_rows=n_rows)
        xsort = sc_scatter_rows(h2, dest, n_rows)
        act = moe_up(xsort, block_exp, n_valid, rows_valid, w_gu, b_gu4, l, bm=bm, tn=TN_UP)
        yrow = moe_down(act, block_exp, n_valid, w_down, b_down4, l, bm=bm, tn=TN_DOWN)
        yg = sc_gather(yrow, dest)
        combine = functools.partial(combine_ln, x1, yg, prob, mod3, row_of_block_for(tm_o), ln2_g[l][None, :],
                                    ln2_b[l][None, :], alpha=alpha, tm=tm_o, group=TN_DOWN // 2)
        if l + 1 < depth:
            x = combine()

    y_prompt = combine(row0=0, rows=tp).reshape(bp, lp, d)
    y_sample = combine(row0=tp, rows=ts).reshape(bs, ls, d)
    return y_prompt, y_sample, jnp.stack(states, axis=1)
```

```python
import functools
import math

import jax
import jax.numpy as jnp
from jax import lax
from jax.experimental import pallas as pl
from jax.experimental.pallas import tpu as pltpu
from jax.experimental.pallas import tpu_sc as plsc

F32 = jnp.float32
BF16 = jnp.bfloat16

GRID_W = 64
H_A = 8
DK = 128
DV = 128
W_A = H_A * DV
W_B = 512
W_C = 512
CONV_K = 31
N_BRANCH = 3
N_EXP = 32
TOP_K = 4
SWIGLU_LIMIT = 7.0
SWIGLU_ALPHA = 1.702
LN_EPS = 1e-5
RMS_EPS = 1e-6

LANES = 128
SUBLANES = 8
VMEM_LIMIT = 56 * 1024 * 1024
HGRN_CHUNK = 16
HGRN_UNROLL = 16
HGRN_HEADS = 2
CONV_PAD = 16
CONV_ROWS = 32
LOGF_FLOOR = -1e4
LOG2E = 1.4426950408889634
TM = 1024
TM_IN = 2048
TN = 512
TM_OUT = 512
MOE_BM = 512
TN_UP = 1024
TN_DOWN = 2048
RANK_ROWS = 512
SC_GATHER_ROWS = 32


def _cparams(*sem):
    return pltpu.CompilerParams(dimension_semantics=sem, vmem_limit_bytes=VMEM_LIMIT)


def _ln_rows(x):
    mu = jnp.mean(x, axis=-1, keepdims=True)
    xc = x - mu
    var = jnp.mean(xc * xc, axis=-1, keepdims=True)
    return xc * lax.rsqrt(var + LN_EPS)


def _sigmoid(x):
    return 1.0 / (1.0 + jnp.exp(-x))


def _bf16_bits(x):
    u = lax.bitcast_convert_type(x, jnp.uint32)
    r = u + jnp.uint32(0x7FFF) + ((u >> 16) & jnp.uint32(1))
    return r & jnp.uint32(0xFFFF0000)


def _pack_halves(x):
    n = x.shape[1] // 2
    return (_bf16_bits(x[:, :n]) >> 16) | _bf16_bits(x[:, n:])


def _unpack_halves(w):
    lo = lax.bitcast_convert_type(w << 16, F32)
    hi = lax.bitcast_convert_type(w & jnp.uint32(0xFFFF0000), F32)
    return lo, hi


def _ada_kernel(c_ref, w_ref, b_ref, o_ref):
    c = c_ref[...]
    a = (c * _sigmoid(c)).astype(BF16)
    o_ref[...] = jnp.dot(a, w_ref[...].astype(BF16), preferred_element_type=F32) + b_ref[...]


def ada_mod(cond, w_ada, b_ada, *, tn=1024):
    depth, d, n = w_ada.shape
    r = cond.shape[0]
    return pl.pallas_call(
        _ada_kernel,
        grid=(depth, n // tn),
        in_specs=[
            pl.BlockSpec((r, d), lambda l, j: (0, 0)),
            pl.BlockSpec((None, d, tn), lambda l, j: (l, 0, j)),
            pl.BlockSpec((None, 1, tn), lambda l, j: (l, 0, j)),
        ],
        out_specs=pl.BlockSpec((None, r, tn), lambda l, j: (l, 0, j)),
        out_shape=jax.ShapeDtypeStruct((depth, r, n), F32),
        compiler_params=_cparams("parallel", "parallel"),
        name="ada_mod",
    )(cond, w_ada, b_ada.reshape(depth, 1, n))


def _lnmm_kernel(x_ref, sh_ref, sc_ref, wi_ref, bi_ref, wg_ref, bg_ref, o_ref, h_ref, *, n_plain):
    j = pl.program_id(1)

    @pl.when(j == 0)
    def _():
        y = _ln_rows(x_ref[...])
        h_ref[...] = (y * (1.0 + sc_ref[...]) + sh_ref[...]).astype(BF16)

    def project(w_ref, b_ref):
        return jnp.dot(h_ref[...], w_ref[...], preferred_element_type=F32) + b_ref[...]

    @pl.when(j < n_plain)
    def _():
        o_ref[...] = project(wi_ref, bi_ref).astype(BF16)

    @pl.when(j >= n_plain)
    def _():
        o_ref[...] = _sigmoid(project(wg_ref, bg_ref)).astype(BF16)


def ln_mod_matmul(x, mod3, row_of_block, w_in, b_in, w_gate, b_gate, *, tm, tn):
    t, d = x.shape
    n1 = w_in.shape[1] // tn
    n2 = w_gate.shape[1] // tn

    def first(j):
        return jnp.minimum(j, n1 - 1)

    def second(j):
        return jnp.maximum(j - n1, 0)

    return pl.pallas_call(
        functools.partial(_lnmm_kernel, n_plain=n1),
        grid=(t // tm, n1 + n2),
        in_specs=[
            pl.BlockSpec((tm, d), lambda i, j: (i, 0), pipeline_mode=pl.Buffered(1)),
            pl.BlockSpec((None, 1, d), lambda i, j: (row_of_block(i), 0, 0)),
            pl.BlockSpec((None, 1, d), lambda i, j: (row_of_block(i), 0, 1)),
            pl.BlockSpec((d, tn), lambda i, j: (0, first(j))),
            pl.BlockSpec((1, tn), lambda i, j: (0, first(j))),
            pl.BlockSpec((d, tn), lambda i, j: (0, second(j))),
            pl.BlockSpec((1, tn), lambda i, j: (0, second(j))),
        ],
        out_specs=pl.BlockSpec((tm, tn), lambda i, j: (i, j)),
        out_shape=jax.ShapeDtypeStruct((t, (n1 + n2) * tn), BF16),
        scratch_shapes=[pltpu.VMEM((tm, d), BF16)],
        compiler_params=_cparams("parallel", "arbitrary"),
        name="ln_mod_matmul",
    )(x, mod3, mod3, w_in, b_in, w_gate, b_gate)


def _chunk_scan(x_h, direction):
    half = SUBLANES
    rid = lax.broadcasted_iota(jnp.int32, x_h[0].shape, 0)
    out = []
    for x in x_h:
        step = 1
        while step < half:
            if direction == 0:
                x = x + jnp.where(rid >= step, pltpu.roll(x, step, axis=0), 0.0)
            else:
                x = x + jnp.where(rid < half - step, pltpu.roll(x, half - step, axis=0), 0.0)
            step *= 2
        out.append(x)
    if direction == 0:
        out[1] = out[1] + out[0][half - 1:half, :]
    else:
        out[0] = out[0] + out[1][0:1, :]
    return out


def _hgrn_chunk(q_ref, z_ref, v_ref, lb, oml, st_ref, o_ref, rows_ref, c0, *, direction, slot):
    half = SUBLANES
    rows = pl.ds(c0, HGRN_CHUNK)
    q = q_ref[rows, :].astype(F32) * (DK ** -0.5)
    z = z_ref[rows, :].astype(F32)
    v = v_ref[rows, :]
    e = jnp.exp2(jnp.abs(z) * (-LOG2E))
    r = 1.0 / (1.0 + e)
    er = e * r
    pos = z >= 0
    lf2 = jnp.maximum(jnp.log2(lb + oml * jnp.where(pos, r, er)), LOGF_FLOOR)
    lk2 = jnp.log2(oml * jnp.where(pos, er, r))

    b_h = _chunk_scan([lf2[0:half, :], lf2[half:HGRN_CHUNK, :]], direction)
    b2 = jnp.concatenate(b_h, axis=0)
    b_end = b_h[1][half - 1:half, :] if direction == 0 else b_h[0][0:1, :]

    st = st_ref[slot]
    qe = (q * jnp.exp2(b2)).astype(BF16)
    o_inter = lax.dot_general(qe, st.astype(BF16), (((1,), (1,)), ((), ())), preferred_element_type=F32)

    rows_ref[0] = b2 - lk2
    rows_ref[1] = v.astype(F32)
    q_h = (q[0:half, :], q[half:HGRN_CHUNK, :])
    acc = [jnp.zeros((half, DV), F32), jnp.zeros((half, DV), F32)]
    rid = lax.broadcasted_iota(jnp.int32, (half, DK), 0)
    for s in range(HGRN_CHUNK):
        hs, rs = divmod(s, half)
        bs = jnp.broadcast_to(rows_ref[0, s:s + 1, :], (half, DK))
        vs = jnp.broadcast_to(rows_ref[1, s:s + 1, :], (half, DV))
        for ht in range(2):
            if (ht < hs) if direction == 0 else (ht > hs):
                continue
            d = b_h[ht] - bs
            if ht == hs:
                keep = (rid >= rs) if direction == 0 else (rid <= rs)
                d = jnp.where(keep, d, -1e30)
            w = jnp.sum(q_h[ht] * jnp.exp2(d), axis=-1, keepdims=True)
            acc[ht] = acc[ht] + w * vs
    o_ref[pl.ds(c0, half), :] = acc[0] + o_inter[0:half, :]
    o_ref[pl.ds(c0 + half, half), :] = acc[1] + o_inter[half:HGRN_CHUNK, :]

    kk = jnp.exp2(lk2 + (b_end - b2)).astype(BF16)
    upd = lax.dot_general(v, kk, (((0,), (0,)), ((), ())), preferred_element_type=F32)
    st_ref[slot] = st * jnp.exp2(b_end) + upd


def _hgrn_kernel(*refs, seq, unroll, heads, has_s0, want_state):
    q_ref, zf_ref, zb_ref, v_ref, g_ref, lb_ref, gn_ref = refs[:7]
    pos = 7
    s0_ref = None
    if has_s0:
        s0_ref = refs[pos]
        pos += 1
    y_ref = refs[pos]
    pos += 1
    sfin_ref = None
    if want_state:
        sfin_ref = refs[pos]
        pos += 1
    of_ref, ob_ref, st_ref, rows_ref = refs[pos:pos + 4]

    for hh in range(heads):
        for d in range(2):
            st_ref[2 * hh + d] = s0_ref[d, hh].T if has_s0 else jnp.zeros((DV, DK), F32)

    n = seq // HGRN_CHUNK

    def cols(ref, hh):
        return ref.at[:, hh * DK:(hh + 1) * DK]

    def body(i, carry):
        for j in range(unroll):
            cf = pl.multiple_of((i * unroll + j) * HGRN_CHUNK, HGRN_CHUNK)
            cb = pl.multiple_of((n - 1 - (i * unroll + j)) * HGRN_CHUNK, HGRN_CHUNK)
            for hh in range(heads):
                lb_f = lb_ref[0:1, hh * DK:(hh + 1) * DK]
                lb_b = lb_ref[1:2, hh * DK:(hh + 1) * DK]
                stage = rows_ref.at[2 * (j * heads + hh)]
                _hgrn_chunk(cols(q_ref, hh), cols(zf_ref, hh), cols(v_ref, hh), lb_f, 1.0 - lb_f, st_ref,
                            cols(of_ref, hh), stage, cf, direction=0, slot=2 * hh)
                stage = rows_ref.at[2 * (j * heads + hh) + 1]
                _hgrn_chunk(cols(q_ref, hh), cols(zb_ref, hh), cols(v_ref, hh), lb_b, 1.0 - lb_b, st_ref,
                            cols(ob_ref, hh), stage, cb, direction=1, slot=2 * hh + 1)
        return carry

    lax.fori_loop(0, n // unroll, body, 0)

    blk = min(seq, 256)
    for hh in range(heads):
        hc = slice(hh * DV, (hh + 1) * DV)
        for r0 in range(0, seq, blk):
            o = of_ref[r0:r0 + blk, hc] + ob_ref[r0:r0 + blk, hc]
            g = g_ref[r0:r0 + blk, hc].astype(F32)
            o = o * lax.rsqrt(jnp.mean(o * o, axis=-1, keepdims=True) + RMS_EPS) * gn_ref[...]
            y_ref[r0:r0 + blk, hc] = (o * (g * _sigmoid(g))).astype(BF16)
        if want_state:
            sfin_ref[0, hh] = st_ref[2 * hh].T
            sfin_ref[1, hh] = st_ref[2 * hh + 1].T


def hgrn_mixer(u, lb, g_norm, s0, *, row0, batch, seq, want_state, unroll=HGRN_UNROLL, heads=HGRN_HEADS):
    rb0 = row0 // seq
    has_s0 = s0 is not None
    unroll = math.gcd(seq // HGRN_CHUNK, unroll // heads)
    hg = H_A // heads
    width = heads * DK

    def sec(k):
        return pl.BlockSpec((seq, width), lambda b, h: (rb0 + b, k * hg + h))

    in_specs = [sec(0), sec(1), sec(2), sec(3), sec(4),
                pl.BlockSpec((2, width), lambda b, h: (0, h)),
                pl.BlockSpec((1, DV), lambda b, h: (0, 0))]
    args = [u, u, u, u, u, lb, g_norm]
    state_spec = pl.BlockSpec((None, 2, heads, DK, DV), lambda b, h: (b, 0, h, 0, 0))
    if has_s0:
        in_specs.append(state_spec)
        args.append(s0)
    out_specs = [pl.BlockSpec((seq, width), lambda b, h: (b, h))]
    out_shape = [jax.ShapeDtypeStruct((batch * seq, W_A), BF16)]
    if want_state:
        out_specs.append(state_spec)
        out_shape.append(jax.ShapeDtypeStruct((batch, 2, H_A, DK, DV), F32))
    return pl.pallas_call(
        functools.partial(_hgrn_kernel, seq=seq, unroll=unroll, heads=heads, has_s0=has_s0, want_state=want_state),
        grid=(batch, hg),
        in_specs=in_specs,
        out_specs=out_specs,
        out_shape=out_shape,
        scratch_shapes=[pltpu.VMEM((seq, width), F32), pltpu.VMEM((seq, width), F32),
                        pltpu.VMEM((2 * heads, DV, DK), F32),
                        pltpu.VMEM((2 * unroll * heads, 2, HGRN_CHUNK, DK), F32)],
        compiler_params=_cparams("parallel", "parallel"),
        name="hgrn_mixer",
    )(*args)


def _fourier_kernel(u_ref, cm_ref, dl_ref, o_ref, ab_ref, *, seq, rows, scale):
    @pl.when(pl.program_id(1) == 0)
    def _():
        for r0 in range(0, seq, rows):
            ab = jnp.dot(u_ref[r0:r0 + rows, :], cm_ref[...], preferred_element_type=F32)
            ab_ref[r0:r0 + rows, :] = ab[:, :W_B].astype(BF16)
            ab_ref[seq + r0:seq + r0 + rows, :] = ab[:, W_B:].astype(BF16)

    o_ref[...] = (jnp.dot(dl_ref[...], ab_ref[...], preferred_element_type=F32) * scale).astype(BF16)


def dft_tables(seq):
    def cs(n):
        i = lax.iota(jnp.int32, n)
        ang = ((i[:, None] * i[None, :]) % n).astype(F32) * (2.0 * math.pi / n)
        return jnp.cos(ang), jnp.sin(ang)
    cl, sl = cs(seq)
    cc, sc = cs(W_B)
    return jnp.concatenate([cl, -sl], axis=1).astype(BF16), jnp.concatenate([cc, sc], axis=1).astype(BF16)


def fourier_mixer(u, dl, cm, *, row0, col0, batch, seq, rows=256):
    rows = min(rows, seq)
    rb0 = row0 // seq
    cb = col0 // W_B
    nt = seq // rows
    return pl.pallas_call(
        functools.partial(_fourier_kernel, seq=seq, rows=rows, scale=1.0 / math.sqrt(seq * W_B)),
        grid=(batch, nt),
        in_specs=[
            pl.BlockSpec((seq, W_B), lambda b, i: (rb0 + b, cb)),
            pl.BlockSpec((W_B, 2 * W_B), lambda b, i: (0, 0)),
            pl.BlockSpec((rows, 2 * seq), lambda b, i: (i, 0)),
        ],
        out_specs=pl.BlockSpec((rows, W_B), lambda b, i: (b * nt + i, 0)),
        out_shape=jax.ShapeDtypeStruct((batch * seq, W_B), BF16),
        scratch_shapes=[pltpu.VMEM((2 * seq, W_B), BF16)],
        compiler_params=_cparams("parallel", "arbitrary"),
        name="fourier_mixer",
    )(u, cm, dl)


def _conv_kernel(a_ref, gt_ref, w_ref, b_ref, g_ref, be_ref, o_ref, z_ref, sh_ref, *, seq):
    zeros = jnp.zeros((CONV_PAD, W_C), F32)
    z_ref[0:CONV_PAD, :] = zeros
    z_ref[CONV_PAD + seq:2 * CONV_PAD + seq, :] = zeros
    blk = min(seq, 256)
    for r0 in range(0, seq, blk):
        a = a_ref[r0:r0 + blk, :].astype(F32)
        gt = gt_ref[r0:r0 + blk, :].astype(F32)
        z_ref[CONV_PAD + r0:CONV_PAD + r0 + blk, :] = a * _sigmoid(gt)

    off = CONV_PAD - CONV_K // 2

    def tile(i, carry):
        r0 = pl.multiple_of(i * CONV_ROWS, CONV_ROWS)
        win = z_ref[pl.ds(r0, CONV_ROWS + 2 * CONV_PAD), :]
        span = CONV_ROWS + 2 * CONV_PAD - SUBLANES
        for r in range(SUBLANES):
            sh_ref[r] = win[r:r + span, :]
        acc = jnp.zeros((CONV_ROWS, W_C), F32)
        for k in range(CONV_K):
            m, r = divmod(off + k, SUBLANES)
            acc = acc + sh_ref[r, m * SUBLANES:m * SUBLANES + CONV_ROWS, :] * w_ref[k:k + 1, :]
        y = _ln_rows(acc + b_ref[...]) * g_ref[...] + be_ref[...]
        o_ref[pl.ds(r0, CONV_ROWS), :] = (y * _sigmoid(y)).astype(BF16)
        return carry

    lax.fori_loop(0, seq // CONV_ROWS, tile, 0)


def conv_mixer(u, conv_w, conv_b, ln_g, ln_b, *, row0, col0, batch, seq):
    rb0 = row0 // seq
    cb = col0 // W_C
    vec = pl.BlockSpec((1, W_C), lambda b: (0, 0))
    return pl.pallas_call(
        functools.partial(_conv_kernel, seq=seq),
        grid=(batch,),
        in_specs=[
            pl.BlockSpec((seq, W_C), lambda b: (rb0 + b, cb)),
            pl.BlockSpec((seq, W_C), lambda b: (rb0 + b, cb + 1)),
            pl.BlockSpec((CONV_K, W_C), lambda b: (0, 0)),
            vec, vec, vec,
        ],
        out_specs=pl.BlockSpec((seq, W_C), lambda b: (b, 0)),
        out_shape=jax.ShapeDtypeStruct((batch * seq, W_C), BF16),
        scratch_shapes=[pltpu.VMEM((seq + 2 * CONV_PAD, W_C), F32),
                        pltpu.VMEM((SUBLANES, CONV_ROWS + 2 * CONV_PAD - SUBLANES, W_C), F32)],
        compiler_params=_cparams("parallel"),
        name="conv_mixer",
    )(u, u, conv_w, conv_b, ln_g, ln_b)


def _branch_kernel(yap_ref, ybp_ref, ycp_ref, yas_ref, ybs_ref, ycs_ref, g0_ref, g1_ref, g2_ref,
                   wa_ref, wb_ref, wc_ref, b_ref, o_ref, *, n_first):
    def merge(ya_ref, yb_ref, yc_ref):
        def proj(y_ref, w_ref, k):
            return jnp.dot(y_ref[...], w_ref[...], preferred_element_type=F32) + b_ref[k:k + 1, :]
        m = g0_ref[...].astype(F32) * proj(ya_ref, wa_ref, 0)
        m = m + g1_ref[...].astype(F32) * proj(yb_ref, wb_ref, 1)
        m = m + g2_ref[...].astype(F32) * proj(yc_ref, wc_ref, 2)
        o_ref[...] = m.astype(BF16)

    first = pl.program_id(0) < n_first

    @pl.when(first)
    def _():
        merge(yap_ref, ybp_ref, ycp_ref)

    @pl.when(jnp.logical_not(first))
    def _():
        merge(yas_ref, ybs_ref, ycs_ref)


def branch_merge(y_first, y_second, u, wa, wb, wc, b_br, *, gate_col0, d, tm, tn):
    t1 = y_first[0].shape[0]
    t2 = y_second[0].shape[0]
    n1 = t1 // tm
    n2 = t2 // tm
    assert gate_col0 % tn == 0 and d % tn == 0, (gate_col0, d, tn)
    gb = gate_col0 // tn
    nd = d // tn

    def gate(k):
        return pl.BlockSpec((tm, tn), lambda i, j: (i, gb + k * nd + j))

    def first(y):
        return pl.BlockSpec((tm, y.shape[1]), lambda i, j: (jnp.minimum(i, n1 - 1), 0))

    def second(y):
        return pl.BlockSpec((tm, y.shape[1]), lambda i, j: (jnp.maximum(i - n1, 0), 0))

    return pl.pallas_call(
        functools.partial(_branch_kernel, n_first=n1),
        grid=(n1 + n2, nd),
        in_specs=[
            first(y_first[0]), first(y_first[1]), first(y_first[2]),
            second(y_second[0]), second(y_second[1]), second(y_second[2]),
            gate(0), gate(1), gate(2),
            pl.BlockSpec((wa.shape[0], tn), lambda i, j: (0, j)),
            pl.BlockSpec((wb.shape[0], tn), lambda i, j: (0, j)),
            pl.BlockSpec((wc.shape[0], tn), lambda i, j: (0, j)),
            pl.BlockSpec((N_BRANCH, tn), lambda i, j: (0, j)),
        ],
        out_specs=pl.BlockSpec((tm, tn), lambda i, j: (i, j)),
        out_shape=jax.ShapeDtypeStruct((t1 + t2, d), BF16),
        compiler_params=_cparams("parallel", "arbitrary"),
        name="branch_merge",
    )(*y_first, *y_second, u, u, u, wa, wb, wc, b_br)


def _out_router_kernel(x_ref, m_ref, g1_ref, sh2_ref, sc2_ref, wo_ref, bo_ref, lg_ref, lb_ref,
                       wr_ref, br_ref, x1_ref, h2_ref, idx_ref, p_ref, *, alpha):
    mix = jnp.dot(m_ref[...], wo_ref[...], preferred_element_type=F32) + bo_ref[...]
    x1 = _ln_rows(alpha * x_ref[...] + g1_ref[...] * mix) * lg_ref[...] + lb_ref[...]
    x1_ref[...] = x1
    h2 = _ln_rows(x1) * (1.0 + sc2_ref[...]) + sh2_ref[...]
    h2_ref[...] = _pack_halves(h2)
    logits = jnp.dot(h2.astype(BF16), wr_ref[...], preferred_element_type=F32) + br_ref[...]
    lane = lax.broadcasted_iota(jnp.int32, logits.shape, 1)
    idx_out = jnp.zeros(logits.shape, jnp.int32)
    p_out = jnp.zeros(logits.shape, F32)
    top = None
    denom = None
    for k in range(TOP_K):
        m = jnp.max(logits, axis=-1, keepdims=True)
        i = jnp.min(jnp.where(logits == m, lane, LANES), axis=-1, keepdims=True)
        if k == 0:
            top = m
        ek = jnp.exp(m - top)
        denom = ek if k == 0 else denom + ek
        idx_out = jnp.where(lane == k, i, idx_out)
        p_out = jnp.where(lane == k, ek, p_out)
        logits = jnp.where(lane == i, -jnp.inf, logits)
    idx_ref[...] = idx_out
    p_ref[...] = p_out / denom


def out_router(x, merged, mod3, row_of_block, wo, bo, ln_g, ln_b, wr, br, *, alpha, tm):
    t, d = x.shape

    def modspec(k):
        return pl.BlockSpec((None, 1, d), lambda i: (row_of_block(i), 0, k))

    vec = pl.BlockSpec((1, d), lambda i: (0, 0))
    row_f = pl.BlockSpec((tm, d), lambda i: (i, 0))
    row_s = pl.BlockSpec((tm, LANES), lambda i: (i, 0))
    return pl.pallas_call(
        functools.partial(_out_router_kernel, alpha=alpha),
        grid=(t // tm,),
        in_specs=[row_f, row_f, modspec(2), modspec(3), modspec(4),
                  pl.BlockSpec((d, d), lambda i: (0, 0)), vec, vec, vec,
                  pl.BlockSpec((d, LANES), lambda i: (0, 0)),
                  pl.BlockSpec((1, LANES), lambda i: (0, 0))],
        out_specs=[row_f, pl.BlockSpec((tm, d // 2), lambda i: (i, 0)), row_s, row_s],
        out_shape=[jax.ShapeDtypeStruct((t, d), F32), jax.ShapeDtypeStruct((t, d // 2), jnp.uint32),
                   jax.ShapeDtypeStruct((t, LANES), jnp.int32), jax.ShapeDtypeStruct((t, LANES), F32)],
        compiler_params=_cparams("parallel"),
        name="out_router",
    )(x, merged, mod3, mod3, mod3, wo, bo, ln_g, ln_b, wr, br)


def _moe_up_kernel(be_ref, nv_ref, rv_ref, x_ref, wg_ref, wu_ref, bg_ref, bu_ref, o_ref, wgs_ref, wus_ref):
    b = pl.program_id(1)
    fresh = jnp.logical_or(b == 0, be_ref[b] != be_ref[jnp.maximum(b - 1, 0)])

    @pl.when(fresh)
    def _():
        wgs_ref[...] = wg_ref[...].astype(BF16)
        wus_ref[...] = wu_ref[...].astype(BF16)

    @pl.when(b < nv_ref[0])
    def _():
        lo, hi = _unpack_halves(x_ref[...])
        live = lax.broadcasted_iota(jnp.int32, lo.shape, 0) < rv_ref[b]
        lo = jnp.where(live, lo, 0.0).astype(BF16)
        hi = jnp.where(live, hi, 0.0).astype(BF16)
        kh = lo.shape[1]

        def proj(w_ref, b_ref):
            return (jnp.dot(lo, w_ref[0:kh, :], preferred_element_type=F32)
                    + jnp.dot(hi, w_ref[kh:2 * kh, :], preferred_element_type=F32) + b_ref[...])

        gt = proj(wgs_ref, bg_ref)
        up = proj(wus_ref, bu_ref)
        gt = jnp.minimum(gt, SWIGLU_LIMIT)
        up = jnp.clip(up, -SWIGLU_LIMIT, SWIGLU_LIMIT)
        o_ref[...] = ((up + 1.0) * (gt * _sigmoid(SWIGLU_ALPHA * gt))).astype(BF16)

    @pl.when(b >= nv_ref[0])
    def _():
        o_ref[...] = jnp.zeros_like(o_ref)


def moe_up(xs, block_exp, n_valid, rows_valid, w_gu, b_gu, layer, *, bm, tn):
    r = xs.shape[0]
    d = w_gu.shape[2]
    f = w_gu.shape[3] // 2
    nf = f // tn
    grid_spec = pltpu.PrefetchScalarGridSpec(
        num_scalar_prefetch=3,
        grid=(nf, r // bm),
        in_specs=[
            pl.BlockSpec((bm, d // 2), lambda n, b, be, nv, rv: (b, 0)),
            pl.BlockSpec((None, None, d, tn), lambda n, b, be, nv, rv: (layer, be[b], 0, n)),
            pl.BlockSpec((None, None, d, tn), lambda n, b, be, nv, rv: (layer, be[b], 0, nf + n)),
            pl.BlockSpec((None, None, 1, tn), lambda n, b, be, nv, rv: (layer, be[b], 0, n)),
            pl.BlockSpec((None, None, 1, tn), lambda n, b, be, nv, rv: (layer, be[b], 0, nf + n)),
        ],
        out_specs=pl.BlockSpec((bm, tn), lambda n, b, be, nv, rv: (b, n)),
        scratch_shapes=[pltpu.VMEM((d, tn), BF16), pltpu.VMEM((d, tn), BF16)],
    )
    return pl.pallas_call(
        _moe_up_kernel,
        grid_spec=grid_spec,
        out_shape=jax.ShapeDtypeStruct((r, f), BF16),
        compiler_params=_cparams("arbitrary", "arbitrary"),
        name="moe_up",
    )(block_exp, n_valid, rows_valid, xs, w_gu, w_gu, b_gu, b_gu)


def _moe_down_kernel(be_ref, nv_ref, a_ref, w_ref, b_ref, o_ref, ws_ref):
    b = pl.program_id(1)
    fresh = jnp.logical_or(b == 0, be_ref[b] != be_ref[jnp.maximum(b - 1, 0)])

    @pl.when(fresh)
    def _():
        ws_ref[...] = w_ref[...].astype(BF16)

    @pl.when(b < nv_ref[0])
    def _():
        y = jnp.dot(a_ref[...], ws_ref[...], preferred_element_type=F32) + b_ref[...]
        o_ref[...] = _pack_halves(y)

    @pl.when(b >= nv_ref[0])
    def _():
        o_ref[...] = jnp.zeros_like(o_ref)


def moe_down(act, block_exp, n_valid, w_down, b_down, layer, *, bm, tn):
    r, f = act.shape
    d = w_down.shape[3]
    grid_spec = pltpu.PrefetchScalarGridSpec(
        num_scalar_prefetch=2,
        grid=(d // tn, r // bm),
        in_specs=[
            pl.BlockSpec((bm, f), lambda n, b, be, nv: (b, 0)),
            pl.BlockSpec((None, None, f, tn), lambda n, b, be, nv: (layer, be[b], 0, n)),
            pl.BlockSpec((None, None, 1, tn), lambda n, b, be, nv: (layer, be[b], 0, n)),
        ],
        out_specs=pl.BlockSpec((bm, tn // 2), lambda n, b, be, nv: (b, n)),
        scratch_shapes=[pltpu.VMEM((f, tn), BF16)],
    )
    return pl.pallas_call(
        _moe_down_kernel,
        grid_spec=grid_spec,
        out_shape=jax.ShapeDtypeStruct((r, d // 2), jnp.uint32),
        compiler_params=_cparams("arbitrary", "arbitrary"),
        name="moe_down",
    )(block_exp, n_valid, act, w_down, b_down)


def sc_gather(x, idx):
    n = idx.shape[0]
    d = x.shape[1]
    win = SC_GATHER_ROWS
    mesh = plsc.VectorSubcoreMesh(core_axis_name="core", subcore_axis_name="subcore")

    @pl.kernel(out_type=jax.ShapeDtypeStruct((n, d), x.dtype), mesh=mesh)
    def gather_kernel(x_hbm, i_hbm, o_hbm):
        def body(i_vmem, o_vmem):
            pltpu.sync_copy(x_hbm.at[i_vmem.at[0]], o_vmem)

        pltpu.emit_pipeline(
            body,
            grid=(n // win,),
            in_specs=[pl.BlockSpec((1, win), index_map=lambda i: (i, 0))],
            out_specs=[pl.BlockSpec((win, d), index_map=lambda i: (i, 0))],
            core_axis_name=("core", "subcore"),
            dimension_semantics=(pltpu.PARALLEL,),
        )(i_hbm, o_hbm)

    return gather_kernel(x, idx.reshape(n // win, win))


def sc_scatter_rows(x, dest, n_rows):
    t, d = x.shape
    n = dest.shape[0]
    win = SC_GATHER_ROWS
    src_blocks = t // win
    mesh = plsc.VectorSubcoreMesh(core_axis_name="core", subcore_axis_name="subcore")

    @pl.kernel(out_type=jax.ShapeDtypeStruct((n_rows, d), x.dtype), mesh=mesh, scratch_types=[])
    def scatter_kernel(x_hbm, i_hbm, o_hbm):
        def body(x_vmem, i_vmem):
            pltpu.sync_copy(x_vmem, o_hbm.at[i_vmem.at[0]])

        pltpu.emit_pipeline(
            body,
            grid=(n // win,),
            in_specs=[pl.BlockSpec((win, d), index_map=lambda i: (i % src_blocks, 0)),
                      pl.BlockSpec((1, win), index_map=lambda i: (i, 0))],
            out_specs=[],
            core_axis_name=("core", "subcore"),
            dimension_semantics=(pltpu.PARALLEL,),
        )(x_hbm, i_hbm)

    return scatter_kernel(x, dest.reshape(n // win, win))


def _combine_kernel(x_ref, y0_ref, y1_ref, y2_ref, y3_ref, p_ref, g2_ref, lg_ref, lb_ref, o_ref, *, alpha, group):
    p = p_ref[...]
    ffn = None
    for k, y_ref in enumerate((y0_ref, y1_ref, y2_ref, y3_ref)):
        w = y_ref[...]
        cols = []
        for c0 in range(0, w.shape[1], group):
            cols.extend(_unpack_halves(w[:, c0:c0 + group]))
        term = p[:, k:k + 1] * jnp.concatenate(cols, axis=1)
        ffn = term if ffn is None else ffn + term
    o_ref[...] = _ln_rows(alpha * x_ref[...] + g2_ref[...] * ffn) * lg_ref[...] + lb_ref[...]


def combine_ln(x1, yg, prob, mod3, row_of_block, ln_g, ln_b, *, alpha, tm, group, row0=0, rows=None):
    t, d = x1.shape
    rows = t if rows is None else rows
    nb = t // tm
    b0 = row0 // tm
    vec = pl.BlockSpec((1, d), lambda i: (0, 0))

    def sel(k):
        return pl.BlockSpec((tm, d // 2), lambda i: (k * nb + b0 + i, 0))

    return pl.pallas_call(
        functools.partial(_combine_kernel, alpha=alpha, group=group),
        grid=(rows // tm,),
        in_specs=[pl.BlockSpec((tm, d), lambda i: (b0 + i, 0)), sel(0), sel(1), sel(2), sel(3),
                  pl.BlockSpec((tm, LANES), lambda i: (b0 + i, 0)),
                  pl.BlockSpec((None, 1, d), lambda i: (row_of_block(b0 + i), 0, 5)),
                  vec, vec],
        out_specs=pl.BlockSpec((tm, d), lambda i: (i, 0)),
        out_shape=jax.ShapeDtypeStruct((rows, d), F32),
        compiler_params=_cparams("parallel"),
        name="combine_ln",
    )(x1, yg, yg, yg, yg, prob, mod3, ln_g, ln_b)


def _rank_kernel(idx_ref, rank_ref, cnt_ref, carry_ref):
    @pl.when(pl.program_id(0) == 0)
    def _():
        carry_ref[...] = jnp.zeros_like(carry_ref)

    idx = idx_ref[...]
    tb = idx.shape[0]
    lane = lax.broadcasted_iota(jnp.int32, idx.shape, 1)
    hot = [(lane == idx[:, k:k + 1]) for k in range(TOP_K)]
    tot = hot[0]
    for k in range(1, TOP_K):
        tot = jnp.logical_or(tot, hot[k])
    tot = jnp.where(tot, 1.0, 0.0)
    ti = lax.broadcasted_iota(jnp.int32, (tb, tb), 0)
    si = lax.broadcasted_iota(jnp.int32, (tb, tb), 1)
    before = jnp.where(ti > si, 1.0, 0.0).astype(BF16)
    prefix = jnp.dot(before, tot.astype(BF16), preferred_element_type=F32) + carry_ref[...]
    out = jnp.zeros(idx.shape, F32)
    for k in range(TOP_K):
        rk = jnp.sum(jnp.where(hot[k], prefix, 0.0), axis=-1, keepdims=True)
        out = jnp.where(lane == k, rk, out)
    rank_ref[...] = out.astype(jnp.int32)
    carry_ref[...] += jnp.sum(tot, axis=0, keepdims=True)
    cnt_ref[...] = carry_ref[...].astype(jnp.int32)


def route_rank(idx, *, tb):
    t = idx.shape[0]
    return pl.pallas_call(
        _rank_kernel,
        grid=(t // tb,),
        in_specs=[pl.BlockSpec((tb, LANES), lambda i: (i, 0))],
        out_specs=[pl.BlockSpec((tb, LANES), lambda i: (i, 0)), pl.BlockSpec((1, LANES), lambda i: (0, 0))],
        out_shape=[jax.ShapeDtypeStruct((t, LANES), jnp.int32), jax.ShapeDtypeStruct((1, LANES), jnp.int32)],
        scratch_shapes=[pltpu.VMEM((1, LANES), F32)],
        compiler_params=_cparams("arbitrary"),
        name="route_rank",
    )(idx)


def route_layout(top_idx, rank, counts, *, bm, n_rows):
    padded = (counts + bm - 1) // bm * bm
    pad_end = jnp.cumsum(padded)
    pad_start = pad_end - padded
    dest = (pad_start[top_idx] + rank).T.reshape(-1)
    nb = n_rows // bm
    blk_start = jnp.arange(nb, dtype=jnp.int32) * bm
    block_exp = jnp.sum((blk_start[:, None] >= pad_end[None, :]).astype(jnp.int32), axis=1)
    block_exp = jnp.minimum(block_exp, N_EXP - 1)
    n_valid = (pad_end[-1] // bm).astype(jnp.int32).reshape(1)
    rows_valid = jnp.clip(pad_start[block_exp] + counts[block_exp] - blk_start, 0, bm).astype(jnp.int32)
    return dest, block_exp, n_valid, rows_valid


def grid_pos_embed(n_tok, d):
    rows = n_tok // GRID_W
    r = jnp.repeat(jnp.arange(rows, dtype=F32), GRID_W)
    col = jnp.tile(jnp.arange(GRID_W, dtype=F32), rows)
    quarter = d // 4
    omega = 1.0 / (10000.0 ** (jnp.arange(quarter, dtype=F32) / quarter))

    def emb(p):
        a = p[:, None] * omega[None, :]
        return jnp.concatenate([jnp.sin(a), jnp.cos(a)], axis=-1)

    return jnp.concatenate([emb(r), emb(col)], axis=-1)


def kernel(x_prompt, x_sample, state_hgrn, c, c_ctx, w_ada, b_ada, w_in, b_in, lb_logits, g_norm_a, conv_w, conv_b, conv_ln_g, conv_ln_b, w_br_a, w_br_b, w_br_c, b_br, w_gate, b_gate, w_o, b_o, ln1_g, ln1_b, w_router, b_router, w_gu, b_gu, w_down, b_down, ln2_g, ln2_b):
    depth = w_in.shape[0]
    bp, lp, d = x_prompt.shape
    bs, ls, _ = x_sample.shape
    tp, ts = bp * lp, bs * ls
    t = tp + ts
    alpha = (2 * depth) ** 0.25
    w_in_cols = w_in.shape[2]
    col_b = 3 * H_A * DK + 2 * W_A
    col_c = col_b + W_B

    tm, tn, tm_o, bm = TM, TN, TM_OUT, MOE_BM
    n_rows = (t * TOP_K // bm + N_EXP) * bm

    def row_of_block_for(rows_per_block):
        npb = tp // rows_per_block
        per = ls // rows_per_block
        return lambda i: jnp.where(i < npb, 0, 1 + (i - npb) // per)

    sm = jax.nn.softmax(lb_logits.astype(F32), axis=0)
    cs = jnp.cumsum(sm, axis=0)
    lower = cs - cs[0:1]

    cond = jnp.zeros((16, d), F32).at[0].set(c_ctx).at[1:1 + bs].set(c)
    mod = ada_mod(cond, w_ada, b_ada)

    xs = x_sample + grid_pos_embed(ls, d)[None]
    x = jnp.concatenate([x_prompt.reshape(tp, d), xs.reshape(ts, d)], axis=0)

    b_gu4 = b_gu[:, :, None, :]
    b_down4 = b_down[:, :, None, :]
    dl_p, cm = dft_tables(lp)
    dl_s, _ = dft_tables(ls)

    states = []
    for l in range(depth):
        mod3 = mod[l].reshape(16, 1, 6 * d)
        u = ln_mod_matmul(x, mod3, row_of_block_for(TM_IN), w_in[l].astype(BF16), b_in[l][None, :],
                          w_gate[l].astype(BF16), b_gate[l][None, :], tm=TM_IN, tn=tn)

        gn = g_norm_a[l][None, :]
        ya_p, s_ctx = hgrn_mixer(u, lower[l], gn, None, row0=0, batch=bp, seq=lp, want_state=True)
        (ya_s,) = hgrn_mixer(u, lower[l], gn, state_hgrn[:, l], row0=tp, batch=bs, seq=ls, want_state=False)
        states.append(s_ctx)
        yb_p = fourier_mixer(u, dl_p, cm, row0=0, col0=col_b, batch=bp, seq=lp)
        yb_s = fourier_mixer(u, dl_s, cm, row0=tp, col0=col_b, batch=bs, seq=ls)
        cw = (conv_w[l], conv_b[l][None, :], conv_ln_g[l][None, :], conv_ln_b[l][None, :])
        yc_p = conv_mixer(u, *cw, row0=0, col0=col_c, batch=bp, seq=lp)
        yc_s = conv_mixer(u, *cw, row0=tp, col0=col_c, batch=bs, seq=ls)
        merged = branch_merge((ya_p, yb_p, yc_p), (ya_s, yb_s, yc_s), u, w_br_a[l].astype(BF16),
                              w_br_b[l].astype(BF16), w_br_c[l].astype(BF16), b_br[l],
                              gate_col0=w_in_cols, d=d, tm=tm, tn=tn)

        wr = jnp.zeros((d, LANES), BF16).at[:, :N_EXP].set(w_router[l].astype(BF16))
        br = jnp.full((1, LANES), -1e30, F32).at[0, :N_EXP].set(b_router[l])
        x1, h2, idx, prob = out_router(x, merged, mod3, row_of_block_for(tm_o), w_o[l].astype(BF16),
                                       b_o[l][None, :], ln1_g[l][None, :], ln1_b[l][None, :], wr, br,
                                       alpha=alpha, tm=tm_o)

        rank, counts = route_rank(idx, tb=RANK_ROWS)
        dest, block_exp, n_valid, rows_valid = route_layout(idx[:, :TOP_K], rank[:, :TOP_K], counts[0, :N_EXP],
                                                            bm=bm, n_rows=n_rows)
        xsort = sc_scatter_rows(h2, dest, n_rows)
        act = moe_up(xsort, block_exp, n_valid, rows_valid, w_gu, b_gu4, l, bm=bm, tn=TN_UP)
        yrow = moe_down(act, block_exp, n_valid, w_down, b_down4, l, bm=bm, tn=TN_DOWN)
        yg = sc_gather(yrow, dest)
        combine = functools.partial(combine_ln, x1, yg, prob, mod3, row_of_block_for(tm_o), ln2_g[l][None, :],
                                    ln2_b[l][None, :], alpha=alpha, tm=tm_o, group=TN_DOWN // 2)
        if l + 1 < depth:
            x = combine()

    y_prompt = combine(row0=0, rows=tp).reshape(bp, lp, d)
    y_sample = combine(row0=tp, rows=ts).reshape(bs, ls, d)
    return y_prompt, y_sample, jnp.stack(states, axis=1)
```

```python
import functools
import math

import jax
import jax.numpy as jnp
from jax import lax
from jax.experimental import pallas as pl
from jax.experimental.pallas import tpu as pltpu
from jax.experimental.pallas import tpu_sc as plsc

F32 = jnp.float32
BF16 = jnp.bfloat16

GRID_W = 64
H_A = 8
DK = 128
DV = 128
W_A = H_A * DV
W_B = 512
W_C = 512
CONV_K = 31
N_BRANCH = 3
N_EXP = 32
TOP_K = 4
SWIGLU_LIMIT = 7.0
SWIGLU_ALPHA = 1.702
LN_EPS = 1e-5
RMS_EPS = 1e-6

LANES = 128
SUBLANES = 8
VMEM_LIMIT = 56 * 1024 * 1024
HGRN_CHUNK = 16
HGRN_UNROLL = 16
HGRN_HEADS = 2
CONV_PAD = 16
CONV_ROWS = 32
LOGF_FLOOR = -1e4
LOG2E = 1.4426950408889634
TM = 1024
TM_IN = 2048
TN = 512
TM_OUT = 512
MOE_BM = 512
TN_UP = 1024
TN_DOWN = 2048
RANK_ROWS = 512
SC_GATHER_ROWS = 32


def _cparams(*sem):
    return pltpu.CompilerParams(dimension_semantics=sem, vmem_limit_bytes=VMEM_LIMIT)


def _ln_rows(x):
    mu = jnp.mean(x, axis=-1, keepdims=True)
    xc = x - mu
    var = jnp.mean(xc * xc, axis=-1, keepdims=True)
    return xc * lax.rsqrt(var + LN_EPS)


def _sigmoid(x):
    return 1.0 / (1.0 + jnp.exp(-x))


def _bf16_bits(x):
    u = lax.bitcast_convert_type(x, jnp.uint32)
    r = u + jnp.uint32(0x7FFF) + ((u >> 16) & jnp.uint32(1))
    return r & jnp.uint32(0xFFFF0000)


def _pack_halves(x):
    n = x.shape[1] // 2
    return (_bf16_bits(x[:, :n]) >> 16) | _bf16_bits(x[:, n:])


def _unpack_halves(w):
    lo = lax.bitcast_convert_type(w << 16, F32)
    hi = lax.bitcast_convert_type(w & jnp.uint32(0xFFFF0000), F32)
    return lo, hi


def _ada_kernel(c_ref, w_ref, b_ref, o_ref):
    c = c_ref[...]
    a = (c * _sigmoid(c)).astype(BF16)
    o_ref[...] = jnp.dot(a, w_ref[...].astype(BF16), preferred_element_type=F32) + b_ref[...]


def ada_mod(cond, w_ada, b_ada, *, tn=1024):
    depth, d, n = w_ada.shape
    r = cond.shape[0]
    return pl.pallas_call(
        _ada_kernel,
        grid=(depth, n // tn),
        in_specs=[
            pl.BlockSpec((r, d), lambda l, j: (0, 0)),
            pl.BlockSpec((None, d, tn), lambda l, j: (l, 0, j)),
            pl.BlockSpec((None, 1, tn), lambda l, j: (l, 0, j)),
        ],
        out_specs=pl.BlockSpec((None, r, tn), lambda l, j: (l, 0, j)),
        out_shape=jax.ShapeDtypeStruct((depth, r, n), F32),
        compiler_params=_cparams("parallel", "parallel"),
        name="ada_mod",
    )(cond, w_ada, b_ada.reshape(depth, 1, n))


def _lnmm_kernel(x_ref, sh_ref, sc_ref, wi_ref, bi_ref, wg_ref, bg_ref, o_ref, h_ref, *, n_plain):
    j = pl.program_id(1)

    @pl.when(j == 0)
    def _():
        y = _ln_rows(x_ref[...])
        h_ref[...] = (y * (1.0 + sc_ref[...]) + sh_ref[...]).astype(BF16)

    def project(w_ref, b_ref):
        return jnp.dot(h_ref[...], w_ref[...], preferred_element_type=F32) + b_ref[...]

    @pl.when(j < n_plain)
    def _():
        o_ref[...] = project(wi_ref, bi_ref).astype(BF16)

    @pl.when(j >= n_plain)
    def _():
        o_ref[...] = _sigmoid(project(wg_ref, bg_ref)).astype(BF16)


def ln_mod_matmul(x, mod3, row_of_block, w_in, b_in, w_gate, b_gate, *, tm, tn):
    t, d = x.shape
    n1 = w_in.shape[1] // tn
    n2 = w_gate.shape[1] // tn

    def first(j):
        return jnp.minimum(j, n1 - 1)

    def second(j):
        return jnp.maximum(j - n1, 0)

    return pl.pallas_call(
        functools.partial(_lnmm_kernel, n_plain=n1),
        grid=(t // tm, n1 + n2),
        in_specs=[
            pl.BlockSpec((tm, d), lambda i, j: (i, 0), pipeline_mode=pl.Buffered(1)),
            pl.BlockSpec((None, 1, d), lambda i, j: (row_of_block(i), 0, 0)),
            pl.BlockSpec((None, 1, d), lambda i, j: (row_of_block(i), 0, 1)),
            pl.BlockSpec((d, tn), lambda i, j: (0, first(j))),
            pl.BlockSpec((1, tn), lambda i, j: (0, first(j))),
            pl.BlockSpec((d, tn), lambda i, j: (0, second(j))),
            pl.BlockSpec((1, tn), lambda i, j: (0, second(j))),
        ],
        out_specs=pl.BlockSpec((tm, tn), lambda i, j: (i, j)),
        out_shape=jax.ShapeDtypeStruct((t, (n1 + n2) * tn), BF16),
        scratch_shapes=[pltpu.VMEM((tm, d), BF16)],
        compiler_params=_cparams("parallel", "arbitrary"),
        name="ln_mod_matmul",
    )(x, mod3, mod3, w_in, b_in, w_gate, b_gate)


def _chunk_scan(x_h, direction):
    half = SUBLANES
    rid = lax.broadcasted_iota(jnp.int32, x_h[0].shape, 0)
    out = []
    for x in x_h:
        step = 1
        while step < half:
            if direction == 0:
                x = x + jnp.where(rid >= step, pltpu.roll(x, step, axis=0), 0.0)
            else:
                x = x + jnp.where(rid < half - step, pltpu.roll(x, half - step, axis=0), 0.0)
            step *= 2
        out.append(x)
    if direction == 0:
        out[1] = out[1] + out[0][half - 1:half, :]
    else:
        out[0] = out[0] + out[1][0:1, :]
    return out


def _hgrn_chunk(q_ref, z_ref, v_ref, lb, oml, st_ref, o_ref, rows_ref, c0, *, direction, slot):
    half = SUBLANES
    rows = pl.ds(c0, HGRN_CHUNK)
    q = q_ref[rows, :].astype(F32) * (DK ** -0.5)
    z = z_ref[rows, :].astype(F32)
    v = v_ref[rows, :]
    e = jnp.exp2(jnp.abs(z) * (-LOG2E))
    r = 1.0 / (1.0 + e)
    er = e * r
    pos = z >= 0
    lf2 = jnp.maximum(jnp.log2(lb + oml * jnp.where(pos, r, er)), LOGF_FLOOR)
    lk2 = jnp.log2(oml * jnp.where(pos, er, r))

    b_h = _chunk_scan([lf2[0:half, :], lf2[half:HGRN_CHUNK, :]], direction)
    b2 = jnp.concatenate(b_h, axis=0)
    b_end = b_h[1][half - 1:half, :] if direction == 0 else b_h[0][0:1, :]

    st = st_ref[slot]
    qe = (q * jnp.exp2(b2)).astype(BF16)
    o_inter = lax.dot_general(qe, st.astype(BF16), (((1,), (1,)), ((), ())), preferred_element_type=F32)

    rows_ref[0] = b2 - lk2
    rows_ref[1] = v.astype(F32)
    q_h = (q[0:half, :], q[half:HGRN_CHUNK, :])
    acc = [jnp.zeros((half, DV), F32), jnp.zeros((half, DV), F32)]
    rid = lax.broadcasted_iota(jnp.int32, (half, DK), 0)
    for s in range(HGRN_CHUNK):
        hs, rs = divmod(s, half)
        bs = jnp.broadcast_to(rows_ref[0, s:s + 1, :], (half, DK))
        vs = jnp.broadcast_to(rows_ref[1, s:s + 1, :], (half, DV))
        for ht in range(2):
            if (ht < hs) if direction == 0 else (ht > hs):
                continue
            d = b_h[ht] - bs
            if ht == hs:
                keep = (rid >= rs) if direction == 0 else (rid <= rs)
                d = jnp.where(keep, d, -1e30)
            w = jnp.sum(q_h[ht] * jnp.exp2(d), axis=-1, keepdims=True)
            acc[ht] = acc[ht] + w * vs
    o_ref[pl.ds(c0, half), :] = acc[0] + o_inter[0:half, :]
    o_ref[pl.ds(c0 + half, half), :] = acc[1] + o_inter[half:HGRN_CHUNK, :]

    kk = jnp.exp2(lk2 + (b_end - b2)).astype(BF16)
    upd = lax.dot_general(v, kk, (((0,), (0,)), ((), ())), preferred_element_type=F32)
    st_ref[slot] = st * jnp.exp2(b_end) + upd


def _hgrn_kernel(*refs, seq, unroll, heads, has_s0, want_state):
    q_ref, zf_ref, zb_ref, v_ref, g_ref, lb_ref, gn_ref = refs[:7]
    pos = 7
    s0_ref = None
    if has_s0:
        s0_ref = refs[pos]
        pos += 1
    y_ref = refs[pos]
    pos += 1
    sfin_ref = None
    if want_state:
        sfin_ref = refs[pos]
        pos += 1
    of_ref, ob_ref, st_ref, rows_ref = refs[pos:pos + 4]

    for hh in range(heads):
        for d in range(2):
            st_ref[2 * hh + d] = s0_ref[d, hh].T if has_s0 else jnp.zeros((DV, DK), F32)

    n = seq // HGRN_CHUNK

    def cols(ref, hh):
        return ref.at[:, hh * DK:(hh + 1) * DK]

    def body(i, carry):
        for j in range(unroll):
            cf = pl.multiple_of((i * unroll + j) * HGRN_CHUNK, HGRN_CHUNK)
            cb = pl.multiple_of((n - 1 - (i * unroll + j)) * HGRN_CHUNK, HGRN_CHUNK)
            for hh in range(heads):
                lb_f = lb_ref[0:1, hh * DK:(hh + 1) * DK]
                lb_b = lb_ref[1:2, hh * DK:(hh + 1) * DK]
                stage = rows_ref.at[2 * (j * heads + hh)]
                _hgrn_chunk(cols(q_ref, hh), cols(zf_ref, hh), cols(v_ref, hh), lb_f, 1.0 - lb_f, st_ref,
                            cols(of_ref, hh), stage, cf, direction=0, slot=2 * hh)
                stage = rows_ref.at[2 * (j * heads + hh) + 1]
                _hgrn_chunk(cols(q_ref, hh), cols(zb_ref, hh), cols(v_ref, hh), lb_b, 1.0 - lb_b, st_ref,
                            cols(ob_ref, hh), stage, cb, direction=1, slot=2 * hh + 1)
        return carry

    lax.fori_loop(0, n // unroll, body, 0)

    blk = min(seq, 256)
    for hh in range(heads):
        hc = slice(hh * DV, (hh + 1) * DV)
        for r0 in range(0, seq, blk):
            o = of_ref[r0:r0 + blk, hc] + ob_ref[r0:r0 + blk, hc]
            g = g_ref[r0:r0 + blk, hc].astype(F32)
            o = o * lax.rsqrt(jnp.mean(o * o, axis=-1, keepdims=True) + RMS_EPS) * gn_ref[...]
            y_ref[r0:r0 + blk, hc] = (o * (g * _sigmoid(g))).astype(BF16)
        if want_state:
            sfin_ref[0, hh] = st_ref[2 * hh].T
            sfin_ref[1, hh] = st_ref[2 * hh + 1].T


def hgrn_mixer(u, lb, g_norm, s0, *, row0, batch, seq, want_state, unroll=HGRN_UNROLL, heads=HGRN_HEADS):
    rb0 = row0 // seq
    has_s0 = s0 is not None
    unroll = math.gcd(seq // HGRN_CHUNK, unroll // heads)
    hg = H_A // heads
    width = heads * DK

    def sec(k):
        return pl.BlockSpec((seq, width), lambda b, h: (rb0 + b, k * hg + h))

    in_specs = [sec(0), sec(1), sec(2), sec(3), sec(4),
                pl.BlockSpec((2, width), lambda b, h: (0, h)),
                pl.BlockSpec((1, DV), lambda b, h: (0, 0))]
    args = [u, u, u, u, u, lb, g_norm]
    state_spec = pl.BlockSpec((None, 2, heads, DK, DV), lambda b, h: (b, 0, h, 0, 0))
    if has_s0:
        in_specs.append(state_spec)
        args.append(s0)
    out_specs = [pl.BlockSpec((seq, width), lambda b, h: (b, h))]
    out_shape = [jax.ShapeDtypeStruct((batch * seq, W_A), BF16)]
    if want_state:
        out_specs.append(state_spec)
        out_shape.append(jax.ShapeDtypeStruct((batch, 2, H_A, DK, DV), F32))
    return pl.pallas_call(
        functools.partial(_hgrn_kernel, seq=seq, unroll=unroll, heads=heads, has_s0=has_s0, want_state=want_state),
        grid=(batch, hg),
        in_specs=in_specs,
        out_specs=out_specs,
        out_shape=out_shape,
        scratch_shapes=[pltpu.VMEM((seq, width), F32), pltpu.VMEM((seq, width), F32),
                        pltpu.VMEM((2 * heads, DV, DK), F32),
                        pltpu.VMEM((2 * unroll * heads, 2, HGRN_CHUNK, DK), F32)],
        compiler_params=_cparams("parallel", "parallel"),
        name="hgrn_mixer",
    )(*args)


def _fourier_kernel(u_ref, cm_ref, dl_ref, o_ref, ab_ref, *, seq, rows, scale):
    @pl.when(pl.program_id(1) == 0)
    def _():
        for r0 in range(0, seq, rows):
            ab = jnp.dot(u_ref[r0:r0 + rows, :], cm_ref[...], preferred_element_type=F32)
            ab_ref[r0:r0 + rows, :] = ab[:, :W_B].astype(BF16)
            ab_ref[seq + r0:seq + r0 + rows, :] = ab[:, W_B:].astype(BF16)

    o_ref[...] = (jnp.dot(dl_ref[...], ab_ref[...], preferred_element_type=F32) * scale).astype(BF16)


def dft_tables(seq):
    def cs(n):
        i = lax.iota(jnp.int32, n)
        ang = ((i[:, None] * i[None, :]) % n).astype(F32) * (2.0 * math.pi / n)
        return jnp.cos(ang), jnp.sin(ang)
    cl, sl = cs(seq)
    cc, sc = cs(W_B)
    return jnp.concatenate([cl, -sl], axis=1).astype(BF16), jnp.concatenate([cc, sc], axis=1).astype(BF16)


def fourier_mixer(u, dl, cm, *, row0, col0, batch, seq, rows=256):
    rows = min(rows, seq)
    rb0 = row0 // seq
    cb = col0 // W_B
    nt = seq // rows
    return pl.pallas_call(
        functools.partial(_fourier_kernel, seq=seq, rows=rows, scale=1.0 / math.sqrt(seq * W_B)),
        grid=(batch, nt),
        in_specs=[
            pl.BlockSpec((seq, W_B), lambda b, i: (rb0 + b, cb)),
            pl.BlockSpec((W_B, 2 * W_B), lambda b, i: (0, 0)),
            pl.BlockSpec((rows, 2 * seq), lambda b, i: (i, 0)),
        ],
        out_specs=pl.BlockSpec((rows, W_B), lambda b, i: (b * nt + i, 0)),
        out_shape=jax.ShapeDtypeStruct((batch * seq, W_B), BF16),
        scratch_shapes=[pltpu.VMEM((2 * seq, W_B), BF16)],
        compiler_params=_cparams("parallel", "arbitrary"),
        name="fourier_mixer",
    )(u, cm, dl)


def _conv_kernel(a_ref, gt_ref, w_ref, b_ref, g_ref, be_ref, o_ref, z_ref, sh_ref, *, seq):
    zeros = jnp.zeros((CONV_PAD, W_C), F32)
    z_ref[0:CONV_PAD, :] = zeros
    z_ref[CONV_PAD + seq:2 * CONV_PAD + seq, :] = zeros
    blk = min(seq, 256)
    for r0 in range(0, seq, blk):
        a = a_ref[r0:r0 + blk, :].astype(F32)
        gt = gt_ref[r0:r0 + blk, :].astype(F32)
        z_ref[CONV_PAD + r0:CONV_PAD + r0 + blk, :] = a * _sigmoid(gt)

    off = CONV_PAD - CONV_K // 2

    def tile(i, carry):
        r0 = pl.multiple_of(i * CONV_ROWS, CONV_ROWS)
        win = z_ref[pl.ds(r0, CONV_ROWS + 2 * CONV_PAD), :]
        span = CONV_ROWS + 2 * CONV_PAD - SUBLANES
        for r in range(SUBLANES):
            sh_ref[r] = win[r:r + span, :]
        acc = jnp.zeros((CONV_ROWS, W_C), F32)
        for k in range(CONV_K):
            m, r = divmod(off + k, SUBLANES)
            acc = acc + sh_ref[r, m * SUBLANES:m * SUBLANES + CONV_ROWS, :] * w_ref[k:k + 1, :]
        y = _ln_rows(acc + b_ref[...]) * g_ref[...] + be_ref[...]
        o_ref[pl.ds(r0, CONV_ROWS), :] = (y * _sigmoid(y)).astype(BF16)
        return carry

    lax.fori_loop(0, seq // CONV_ROWS, tile, 0, unroll=2)


def conv_mixer(u, conv_w, conv_b, ln_g, ln_b, *, row0, col0, batch, seq):
    rb0 = row0 // seq
    cb = col0 // W_C
    vec = pl.BlockSpec((1, W_C), lambda b: (0, 0))
    return pl.pallas_call(
        functools.partial(_conv_kernel, seq=seq),
        grid=(batch,),
        in_specs=[
            pl.BlockSpec((seq, W_C), lambda b: (rb0 + b, cb)),
            pl.BlockSpec((seq, W_C), lambda b: (rb0 + b, cb + 1)),
            pl.BlockSpec((CONV_K, W_C), lambda b: (0, 0)),
            vec, vec, vec,
        ],
        out_specs=pl.BlockSpec((seq, W_C), lambda b: (b, 0)),
        out_shape=jax.ShapeDtypeStruct((batch * seq, W_C), BF16),
        scratch_shapes=[pltpu.VMEM((seq + 2 * CONV_PAD, W_C), F32),
                        pltpu.VMEM((SUBLANES, CONV_ROWS + 2 * CONV_PAD - SUBLANES, W_C), F32)],
        compiler_params=_cparams("parallel"),
        name="conv_mixer",
    )(u, u, conv_w, conv_b, ln_g, ln_b)


def _branch_kernel(yap_ref, ybp_ref, ycp_ref, yas_ref, ybs_ref, ycs_ref, g0_ref, g1_ref, g2_ref,
                   wa_ref, wb_ref, wc_ref, b_ref, o_ref, *, n_first):
    def merge(ya_ref, yb_ref, yc_ref):
        def proj(y_ref, w_ref, k):
            return jnp.dot(y_ref[...], w_ref[...], preferred_element_type=F32) + b_ref[k:k + 1, :]
        m = g0_ref[...].astype(F32) * proj(ya_ref, wa_ref, 0)
        m = m + g1_ref[...].astype(F32) * proj(yb_ref, wb_ref, 1)
        m = m + g2_ref[...].astype(F32) * proj(yc_ref, wc_ref, 2)
        o_ref[...] = m.astype(BF16)

    first = pl.program_id(0) < n_first

    @pl.when(first)
    def _():
        merge(yap_ref, ybp_ref, ycp_ref)

    @pl.when(jnp.logical_not(first))
    def _():
        merge(yas_ref, ybs_ref, ycs_ref)


def branch_merge(y_first, y_second, u, wa, wb, wc, b_br, *, gate_col0, d, tm, tn):
    t1 = y_first[0].shape[0]
    t2 = y_second[0].shape[0]
    n1 = t1 // tm
    n2 = t2 // tm
    assert gate_col0 % tn == 0 and d % tn == 0, (gate_col0, d, tn)
    gb = gate_col0 // tn
    nd = d // tn

    def gate(k):
        return pl.BlockSpec((tm, tn), lambda i, j: (i, gb + k * nd + j))

    def first(y):
        return pl.BlockSpec((tm, y.shape[1]), lambda i, j: (jnp.minimum(i, n1 - 1), 0))

    def second(y):
        return pl.BlockSpec((tm, y.shape[1]), lambda i, j: (jnp.maximum(i - n1, 0), 0))

    return pl.pallas_call(
        functools.partial(_branch_kernel, n_first=n1),
        grid=(n1 + n2, nd),
        in_specs=[
            first(y_first[0]), first(y_first[1]), first(y_first[2]),
            second(y_second[0]), second(y_second[1]), second(y_second[2]),
            gate(0), gate(1), gate(2),
            pl.BlockSpec((wa.shape[0], tn), lambda i, j: (0, j)),
            pl.BlockSpec((wb.shape[0], tn), lambda i, j: (0, j)),
            pl.BlockSpec((wc.shape[0], tn), lambda i, j: (0, j)),
            pl.BlockSpec((N_BRANCH, tn), lambda i, j: (0, j)),
        ],
        out_specs=pl.BlockSpec((tm, tn), lambda i, j: (i, j)),
        out_shape=jax.ShapeDtypeStruct((t1 + t2, d), BF16),
        compiler_params=_cparams("parallel", "arbitrary"),
        name="branch_merge",
    )(*y_first, *y_second, u, u, u, wa, wb, wc, b_br)


def _out_router_kernel(x_ref, m_ref, g1_ref, sh2_ref, sc2_ref, wo_ref, bo_ref, lg_ref, lb_ref,
                       wr_ref, br_ref, x1_ref, h2_ref, idx_ref, p_ref, *, alpha):
    mix = jnp.dot(m_ref[...], wo_ref[...], preferred_element_type=F32) + bo_ref[...]
    x1 = _ln_rows(alpha * x_ref[...] + g1_ref[...] * mix) * lg_ref[...] + lb_ref[...]
    x1_ref[...] = x1
    h2 = _ln_rows(x1) * (1.0 + sc2_ref[...]) + sh2_ref[...]
    h2_ref[...] = _pack_halves(h2)
    logits = jnp.dot(h2.astype(BF16), wr_ref[...], preferred_element_type=F32) + br_ref[...]
    lane = lax.broadcasted_iota(jnp.int32, logits.shape, 1)
    idx_out = jnp.zeros(logits.shape, jnp.int32)
    p_out = jnp.zeros(logits.shape, F32)
    top = None
    denom = None
    for k in range(TOP_K):
        m = jnp.max(logits, axis=-1, keepdims=True)
        i = jnp.min(jnp.where(logits == m, lane, LANES), axis=-1, keepdims=True)
        if k == 0:
            top = m
        ek = jnp.exp(m - top)
        denom = ek if k == 0 else denom + ek
        idx_out = jnp.where(lane == k, i, idx_out)
        p_out = jnp.where(lane == k, ek, p_out)
        logits = jnp.where(lane == i, -jnp.inf, logits)
    idx_ref[...] = idx_out
    p_ref[...] = p_out / denom


def out_router(x, merged, mod3, row_of_block, wo, bo, ln_g, ln_b, wr, br, *, alpha, tm):
    t, d = x.shape

    def modspec(k):
        return pl.BlockSpec((None, 1, d), lambda i: (row_of_block(i), 0, k))

    vec = pl.BlockSpec((1, d), lambda i: (0, 0))
    row_f = pl.BlockSpec((tm, d), lambda i: (i, 0))
    row_s = pl.BlockSpec((tm, LANES), lambda i: (i, 0))
    return pl.pallas_call(
        functools.partial(_out_router_kernel, alpha=alpha),
        grid=(t // tm,),
        in_specs=[row_f, row_f, modspec(2), modspec(3), modspec(4),
                  pl.BlockSpec((d, d), lambda i: (0, 0)), vec, vec, vec,
                  pl.BlockSpec((d, LANES), lambda i: (0, 0)),
                  pl.BlockSpec((1, LANES), lambda i: (0, 0))],
        out_specs=[row_f, pl.BlockSpec((tm, d // 2), lambda i: (i, 0)), row_s, row_s],
        out_shape=[jax.ShapeDtypeStruct((t, d), F32), jax.ShapeDtypeStruct((t, d // 2), jnp.uint32),
                   jax.ShapeDtypeStruct((t, LANES), jnp.int32), jax.ShapeDtypeStruct((t, LANES), F32)],
        compiler_params=_cparams("parallel"),
        name="out_router",
    )(x, merged, mod3, mod3, mod3, wo, bo, ln_g, ln_b, wr, br)


def _moe_up_kernel(be_ref, nv_ref, rv_ref, x_ref, wg_ref, wu_ref, bg_ref, bu_ref, o_ref, wgs_ref, wus_ref):
    b = pl.program_id(1)
    fresh = jnp.logical_or(b == 0, be_ref[b] != be_ref[jnp.maximum(b - 1, 0)])

    @pl.when(fresh)
    def _():
        wgs_ref[...] = wg_ref[...].astype(BF16)
        wus_ref[...] = wu_ref[...].astype(BF16)

    @pl.when(b < nv_ref[0])
    def _():
        lo, hi = _unpack_halves(x_ref[...])
        live = lax.broadcasted_iota(jnp.int32, lo.shape, 0) < rv_ref[b]
        lo = jnp.where(live, lo, 0.0).astype(BF16)
        hi = jnp.where(live, hi, 0.0).astype(BF16)
        kh = lo.shape[1]

        def proj(w_ref, b_ref):
            return (jnp.dot(lo, w_ref[0:kh, :], preferred_element_type=F32)
                    + jnp.dot(hi, w_ref[kh:2 * kh, :], preferred_element_type=F32) + b_ref[...])

        gt = proj(wgs_ref, bg_ref)
        up = proj(wus_ref, bu_ref)
        gt = jnp.minimum(gt, SWIGLU_LIMIT)
        up = jnp.clip(up, -SWIGLU_LIMIT, SWIGLU_LIMIT)
        o_ref[...] = ((up + 1.0) * (gt * _sigmoid(SWIGLU_ALPHA * gt))).astype(BF16)

    @pl.when(b >= nv_ref[0])
    def _():
        o_ref[...] = jnp.zeros_like(o_ref)


def moe_up(xs, block_exp, n_valid, rows_valid, w_gu, b_gu, layer, *, bm, tn):
    r = xs.shape[0]
    d = w_gu.shape[2]
    f = w_gu.shape[3] // 2
    nf = f // tn
    grid_spec = pltpu.PrefetchScalarGridSpec(
        num_scalar_prefetch=3,
        grid=(nf, r // bm),
        in_specs=[
            pl.BlockSpec((bm, d // 2), lambda n, b, be, nv, rv: (b, 0)),
            pl.BlockSpec((None, None, d, tn), lambda n, b, be, nv, rv: (layer, be[b], 0, n)),
            pl.BlockSpec((None, None, d, tn), lambda n, b, be, nv, rv: (layer, be[b], 0, nf + n)),
            pl.BlockSpec((None, None, 1, tn), lambda n, b, be, nv, rv: (layer, be[b], 0, n)),
            pl.BlockSpec((None, None, 1, tn), lambda n, b, be, nv, rv: (layer, be[b], 0, nf + n)),
        ],
        out_specs=pl.BlockSpec((bm, tn), lambda n, b, be, nv, rv: (b, n)),
        scratch_shapes=[pltpu.VMEM((d, tn), BF16), pltpu.VMEM((d, tn), BF16)],
    )
    return pl.pallas_call(
        _moe_up_kernel,
        grid_spec=grid_spec,
        out_shape=jax.ShapeDtypeStruct((r, f), BF16),
        compiler_params=_cparams("arbitrary", "arbitrary"),
        name="moe_up",
    )(block_exp, n_valid, rows_valid, xs, w_gu, w_gu, b_gu, b_gu)


def _moe_down_kernel(be_ref, nv_ref, a_ref, w_ref, b_ref, o_ref, ws_ref):
    b = pl.program_id(1)
    fresh = jnp.logical_or(b == 0, be_ref[b] != be_ref[jnp.maximum(b - 1, 0)])

    @pl.when(fresh)
    def _():
        ws_ref[...] = w_ref[...].astype(BF16)

    @pl.when(b < nv_ref[0])
    def _():
        y = jnp.dot(a_ref[...], ws_ref[...], preferred_element_type=F32) + b_ref[...]
        o_ref[...] = _pack_halves(y)

    @pl.when(b >= nv_ref[0])
    def _():
        o_ref[...] = jnp.zeros_like(o_ref)


def moe_down(act, block_exp, n_valid, w_down, b_down, layer, *, bm, tn):
    r, f = act.shape
    d = w_down.shape[3]
    grid_spec = pltpu.PrefetchScalarGridSpec(
        num_scalar_prefetch=2,
        grid=(d // tn, r // bm),
        in_specs=[
            pl.BlockSpec((bm, f), lambda n, b, be, nv: (b, 0)),
            pl.BlockSpec((None, None, f, tn), lambda n, b, be, nv: (layer, be[b], 0, n)),
            pl.BlockSpec((None, None, 1, tn), lambda n, b, be, nv: (layer, be[b], 0, n)),
        ],
        out_specs=pl.BlockSpec((bm, tn // 2), lambda n, b, be, nv: (b, n)),
        scratch_shapes=[pltpu.VMEM((f, tn), BF16)],
    )
    return pl.pallas_call(
        _moe_down_kernel,
        grid_spec=grid_spec,
        out_shape=jax.ShapeDtypeStruct((r, d // 2), jnp.uint32),
        compiler_params=_cparams("arbitrary", "arbitrary"),
        name="moe_down",
    )(block_exp, n_valid, act, w_down, b_down)


def sc_gather(x, idx):
    n = idx.shape[0]
    d = x.shape[1]
    win = SC_GATHER_ROWS
    mesh = plsc.VectorSubcoreMesh(core_axis_name="core", subcore_axis_name="subcore")

    @pl.kernel(out_type=jax.ShapeDtypeStruct((n, d), x.dtype), mesh=mesh)
    def gather_kernel(x_hbm, i_hbm, o_hbm):
        def body(i_vmem, o_vmem):
            pltpu.sync_copy(x_hbm.at[i_vmem.at[0]], o_vmem)

        pltpu.emit_pipeline(
            body,
            grid=(n // win,),
            in_specs=[pl.BlockSpec((1, win), index_map=lambda i: (i, 0))],
            out_specs=[pl.BlockSpec((win, d), index_map=lambda i: (i, 0))],
            core_axis_name=("core", "subcore"),
            dimension_semantics=(pltpu.PARALLEL,),
        )(i_hbm, o_hbm)

    return gather_kernel(x, idx.reshape(n // win, win))


def sc_scatter_rows(x, dest, n_rows):
    t, d = x.shape
    n = dest.shape[0]
    win = SC_GATHER_ROWS
    src_blocks = t // win
    mesh = plsc.VectorSubcoreMesh(core_axis_name="core", subcore_axis_name="subcore")

    @pl.kernel(out_type=jax.ShapeDtypeStruct((n_rows, d), x.dtype), mesh=mesh, scratch_types=[])
    def scatter_kernel(x_hbm, i_hbm, o_hbm):
        def body(x_vmem, i_vmem):
            pltpu.sync_copy(x_vmem, o_hbm.at[i_vmem.at[0]])

        pltpu.emit_pipeline(
            body,
            grid=(n // win,),
            in_specs=[pl.BlockSpec((win, d), index_map=lambda i: (i % src_blocks, 0)),
                      pl.BlockSpec((1, win), index_map=lambda i: (i, 0))],
            out_specs=[],
            core_axis_name=("core", "subcore"),
            dimension_semantics=(pltpu.PARALLEL,),
        )(x_hbm, i_hbm)

    return scatter_kernel(x, dest.reshape(n // win, win))


def _combine_kernel(x_ref, y0_ref, y1_ref, y2_ref, y3_ref, p_ref, g2_ref, lg_ref, lb_ref, o_ref, *, alpha, group):
    p = p_ref[...]
    ffn = None
    for k, y_ref in enumerate((y0_ref, y1_ref, y2_ref, y3_ref)):
        w = y_ref[...]
        cols = []
        for c0 in range(0, w.shape[1], group):
            cols.extend(_unpack_halves(w[:, c0:c0 + group]))
        term = p[:, k:k + 1] * jnp.concatenate(cols, axis=1)
        ffn = term if ffn is None else ffn + term
    o_ref[...] = _ln_rows(alpha * x_ref[...] + g2_ref[...] * ffn) * lg_ref[...] + lb_ref[...]


def combine_ln(x1, yg, prob, mod3, row_of_block, ln_g, ln_b, *, alpha, tm, group, row0=0, rows=None):
    t, d = x1.shape
    rows = t if rows is None else rows
    nb = t // tm
    b0 = row0 // tm
    vec = pl.BlockSpec((1, d), lambda i: (0, 0))

    def sel(k):
        return pl.BlockSpec((tm, d // 2), lambda i: (k * nb + b0 + i, 0))

    return pl.pallas_call(
        functools.partial(_combine_kernel, alpha=alpha, group=group),
        grid=(rows // tm,),
        in_specs=[pl.BlockSpec((tm, d), lambda i: (b0 + i, 0)), sel(0), sel(1), sel(2), sel(3),
                  pl.BlockSpec((tm, LANES), lambda i: (b0 + i, 0)),
                  pl.BlockSpec((None, 1, d), lambda i: (row_of_block(b0 + i), 0, 5)),
                  vec, vec],
        out_specs=pl.BlockSpec((tm, d), lambda i: (i, 0)),
        out_shape=jax.ShapeDtypeStruct((rows, d), F32),
        compiler_params=_cparams("parallel"),
        name="combine_ln",
    )(x1, yg, yg, yg, yg, prob, mod3, ln_g, ln_b)


def _rank_kernel(idx_ref, rank_ref, cnt_ref, carry_ref):
    @pl.when(pl.program_id(0) == 0)
    def _():
        carry_ref[...] = jnp.zeros_like(carry_ref)

    idx = idx_ref[...]
    tb = idx.shape[0]
    lane = lax.broadcasted_iota(jnp.int32, idx.shape, 1)
    hot = [(lane == idx[:, k:k + 1]) for k in range(TOP_K)]
    tot = hot[0]
    for k in range(1, TOP_K):
        tot = jnp.logical_or(tot, hot[k])
    tot = jnp.where(tot, 1.0, 0.0)
    ti = lax.broadcasted_iota(jnp.int32, (tb, tb), 0)
    si = lax.broadcasted_iota(jnp.int32, (tb, tb), 1)
    before = jnp.where(ti > si, 1.0, 0.0).astype(BF16)
    prefix = jnp.dot(before, tot.astype(BF16), preferred_element_type=F32) + carry_ref[...]
    out = jnp.zeros(idx.shape, F32)
    for k in range(TOP_K):
        rk = jnp.sum(jnp.where(hot[k], prefix, 0.0), axis=-1, keepdims=True)
        out = jnp.where(lane == k, rk, out)
    rank_ref[...] = out.astype(jnp.int32)
    carry_ref[...] += jnp.sum(tot, axis=0, keepdims=True)
    cnt_ref[...] = carry_ref[...].astype(jnp.int32)


def route_rank(idx, *, tb):
    t = idx.shape[0]
    return pl.pallas_call(
        _rank_kernel,
        grid=(t // tb,),
        in_specs=[pl.BlockSpec((tb, LANES), lambda i: (i, 0))],
        out_specs=[pl.BlockSpec((tb, LANES), lambda i: (i, 0)), pl.BlockSpec((1, LANES), lambda i: (0, 0))],
        out_shape=[jax.ShapeDtypeStruct((t, LANES), jnp.int32), jax.ShapeDtypeStruct((1, LANES), jnp.int32)],
        scratch_shapes=[pltpu.VMEM((1, LANES), F32)],
        compiler_params=_cparams("arbitrary"),
        name="route_rank",
    )(idx)


def route_layout(top_idx, rank, counts, *, bm, n_rows):
    padded = (counts + bm - 1) // bm * bm
    pad_end = jnp.cumsum(padded)
    pad_start = pad_end - padded
    dest = (pad_start[top_idx] + rank).T.reshape(-1)
    nb = n_rows // bm
    blk_start = jnp.arange(nb, dtype=jnp.int32) * bm
    block_exp = jnp.sum((blk_start[:, None] >= pad_end[None, :]).astype(jnp.int32), axis=1)
    block_exp = jnp.minimum(block_exp, N_EXP - 1)
    n_valid = (pad_end[-1] // bm).astype(jnp.int32).reshape(1)
    rows_valid = jnp.clip(pad_start[block_exp] + counts[block_exp] - blk_start, 0, bm).astype(jnp.int32)
    return dest, block_exp, n_valid, rows_valid


def grid_pos_embed(n_tok, d):
    rows = n_tok // GRID_W
    r = jnp.repeat(jnp.arange(rows, dtype=F32), GRID_W)
    col = jnp.tile(jnp.arange(GRID_W, dtype=F32), rows)
    quarter = d // 4
    omega = 1.0 / (10000.0 ** (jnp.arange(quarter, dtype=F32) / quarter))

    def emb(p):
        a = p[:, None] * omega[None, :]
        return jnp.concatenate([jnp.sin(a), jnp.cos(a)], axis=-1)

    return jnp.concatenate([emb(r), emb(col)], axis=-1)


def kernel(x_prompt, x_sample, state_hgrn, c, c_ctx, w_ada, b_ada, w_in, b_in, lb_logits, g_norm_a, conv_w, conv_b, conv_ln_g, conv_ln_b, w_br_a, w_br_b, w_br_c, b_br, w_gate, b_gate, w_o, b_o, ln1_g, ln1_b, w_router, b_router, w_gu, b_gu, w_down, b_down, ln2_g, ln2_b):
    depth = w_in.shape[0]
    bp, lp, d = x_prompt.shape
    bs, ls, _ = x_sample.shape
    tp, ts = bp * lp, bs * ls
    t = tp + ts
    alpha = (2 * depth) ** 0.25
    w_in_cols = w_in.shape[2]
    col_b = 3 * H_A * DK + 2 * W_A
    col_c = col_b + W_B

    tm, tn, tm_o, bm = TM, TN, TM_OUT, MOE_BM
    n_rows = (t * TOP_K // bm + N_EXP) * bm

    def row_of_block_for(rows_per_block):
        npb = tp // rows_per_block
        per = ls // rows_per_block
        return lambda i: jnp.where(i < npb, 0, 1 + (i - npb) // per)

    sm = jax.nn.softmax(lb_logits.astype(F32), axis=0)
    cs = jnp.cumsum(sm, axis=0)
    lower = cs - cs[0:1]

    cond = jnp.zeros((16, d), F32).at[0].set(c_ctx).at[1:1 + bs].set(c)
    mod = ada_mod(cond, w_ada, b_ada)

    xs = x_sample + grid_pos_embed(ls, d)[None]
    x = jnp.concatenate([x_prompt.reshape(tp, d), xs.reshape(ts, d)], axis=0)

    b_gu4 = b_gu[:, :, None, :]
    b_down4 = b_down[:, :, None, :]
    dl_p, cm = dft_tables(lp)
    dl_s, _ = dft_tables(ls)

    states = []
    for l in range(depth):
        mod3 = mod[l].reshape(16, 1, 6 * d)
        u = ln_mod_matmul(x, mod3, row_of_block_for(TM_IN), w_in[l].astype(BF16), b_in[l][None, :],
                          w_gate[l].astype(BF16), b_gate[l][None, :], tm=TM_IN, tn=tn)

        gn = g_norm_a[l][None, :]
        ya_p, s_ctx = hgrn_mixer(u, lower[l], gn, None, row0=0, batch=bp, seq=lp, want_state=True)
        (ya_s,) = hgrn_mixer(u, lower[l], gn, state_hgrn[:, l], row0=tp, batch=bs, seq=ls, want_state=False)
        states.append(s_ctx)
        yb_p = fourier_mixer(u, dl_p, cm, row0=0, col0=col_b, batch=bp, seq=lp)
        yb_s = fourier_mixer(u, dl_s, cm, row0=tp, col0=col_b, batch=bs, seq=ls)
        cw = (conv_w[l], conv_b[l][None, :], conv_ln_g[l][None, :], conv_ln_b[l][None, :])
        yc_p = conv_mixer(u, *cw, row0=0, col0=col_c, batch=bp, seq=lp)
        yc_s = conv_mixer(u, *cw, row0=tp, col0=col_c, batch=bs, seq=ls)
        merged = branch_merge((ya_p, yb_p, yc_p), (ya_s, yb_s, yc_s), u, w_br_a[l].astype(BF16),
                              w_br_b[l].astype(BF16), w_br_c[l].astype(BF16), b_br[l],
                              gate_col0=w_in_cols, d=d, tm=tm, tn=tn)

        wr = jnp.zeros((d, LANES), BF16).at[:, :N_EXP].set(w_router[l].astype(BF16))
        br = jnp.full((1, LANES), -1e30, F32).at[0, :N_EXP].set(b_router[l])
        x1, h2, idx, prob = out_router(x, merged, mod3, row_of_block_for(tm_o), w_o[l].astype(BF16),
                                       b_o[l][None, :], ln1_g[l][None, :], ln1_b[l][None, :], wr, br,
                                       alpha=alpha, tm=tm_o)

        rank, counts = route_rank(idx, tb=RANK_ROWS)
        dest, block_exp, n_valid, rows_valid = route_layout(idx[:, :TOP_K], rank[:, :TOP_K], counts[0, :N_EXP],
                                                            bm=bm, n_rows=n_rows)
        xsort = sc_scatter_rows(h2, dest, n_rows)
        act = moe_up(xsort, block_exp, n_valid, rows_valid, w_gu, b_gu4, l, bm=bm, tn=TN_UP)
        yrow = moe_down(act, block_exp, n_valid, w_down, b_down4, l, bm=bm, tn=TN_DOWN)
        yg = sc_gather(yrow, dest)
        combine = functools.partial(combine_ln, x1, yg, prob, mod3, row_of_block_for(tm_o), ln2_g[l][None, :],
                                    ln2_b[l][None, :], alpha=alpha, tm=tm_o, group=TN_DOWN // 2)
        if l + 1 < depth:
            x = combine()

    y_prompt = combine(row0=0, rows=tp).reshape(bp, lp, d)
    y_sample = combine(row0=tp, rows=ts).reshape(bs, ls, d)
    return y_prompt, y_sample, jnp.stack(states, axis=1)
```
